```python
import jax, jax.numpy as jnp
from jax import lax
import numpy as np

D_MODEL = 2048
BATCH = 4
SEQ = 4096
DEPTH = 4

N_A_LAYERS = DEPTH // 2
N_B_LAYERS = DEPTH - N_A_LAYERS

GLA_HEADS = 4
GLA_QK_DIM = D_MODEL // 2
GLA_V_DIM = D_MODEL
GLA_DK = GLA_QK_DIM // GLA_HEADS
GLA_DV = GLA_V_DIM // GLA_HEADS
GLA_GATE_RANK = 16
GLA_GATE_TAU = 16.0
GLA_CHUNK = 64
GLA_IN_DIM = 2 * GLA_QK_DIM + 2 * GLA_V_DIM + GLA_GATE_RANK

SB_HEADS = 16
SB_HEAD_DIM = D_MODEL // SB_HEADS
SB_DIM = SB_HEADS * SB_HEAD_DIM
SB_Q_BLOCK = 128

D_FF = 4 * D_MODEL
N_MOD = 6
DEEPNORM_ALPHA = (2.0 * DEPTH) ** 0.25
DEEPNORM_BETA = (8.0 * DEPTH) ** -0.25
LN_EPS = 1e-5
RMS_EPS = 1e-5

kernel_name = "yoco_gla_stickbreaking_deepnorm_adaln"


def layer_norm(x, g, b):
    xf = x.astype(jnp.float32)
    mu = jnp.mean(xf, axis=-1, keepdims=True)
    xc = xf - mu
    var = jnp.mean(xc * xc, axis=-1, keepdims=True)
    return (xc * lax.rsqrt(var + LN_EPS) * g.astype(jnp.float32) + b.astype(jnp.float32)).astype(x.dtype)


def gla_mixer(h, w_in, w_gate_up, b_gate, norm_g, w_out):
    B, T, _ = h.shape
    n_chunks = T // GLA_CHUNK
    proj = h @ w_in
    q, k, v, r, gl = jnp.split(
        proj, [GLA_QK_DIM, 2 * GLA_QK_DIM, 2 * GLA_QK_DIM + GLA_V_DIM, 2 * GLA_QK_DIM + 2 * GLA_V_DIM], axis=-1)
    log_a = jax.nn.log_sigmoid((gl @ w_gate_up + b_gate).astype(jnp.float32)) / GLA_GATE_TAU

    def to_chunks(t, d):
        return t.astype(jnp.float32).reshape(B, n_chunks, GLA_CHUNK, GLA_HEADS, d).transpose(0, 3, 1, 2, 4)

    qc = to_chunks(q, GLA_DK) * (GLA_DK ** -0.5)
    kc = to_chunks(k, GLA_DK)
    vc = to_chunks(v, GLA_DV)
    bc = jnp.cumsum(to_chunks(log_a, GLA_DK), axis=3)
    b_last = bc[:, :, :, -1:, :]
    q_dec = qc * jnp.exp(bc)
    k_intra = kc * jnp.exp(-bc)
    k_state = kc * jnp.exp(b_last - bc)
    chunk_decay = jnp.exp(b_last[:, :, :, 0, :])

    causal = jnp.tril(jnp.ones((GLA_CHUNK, GLA_CHUNK), dtype=jnp.float32))
    scores = jnp.einsum('bhncd,bhnsd->bhncs', q_dec, k_intra) * causal
    o_intra = jnp.einsum('bhncs,bhnse->bhnce', scores, vc)

    def step(state, xs):
        qd, ks, vv, dec = xs
        o = jnp.einsum('bhcd,bhde->bhce', qd, state)
        state = state * dec[..., None] + jnp.einsum('bhcd,bhce->bhde', ks, vv)
        return state, o

    xs = (jnp.moveaxis(q_dec, 2, 0), jnp.moveaxis(k_state, 2, 0),
          jnp.moveaxis(vc, 2, 0), jnp.moveaxis(chunk_decay, 2, 0))
    s0 = jnp.zeros((B, GLA_HEADS, GLA_DK, GLA_DV), jnp.float32)
    _, o_inter = lax.scan(step, s0, xs)
    o = o_intra + jnp.moveaxis(o_inter, 0, 2)
    o = o.transpose(0, 2, 3, 1, 4).reshape(B, T, GLA_HEADS, GLA_DV)
    o = o * lax.rsqrt(jnp.mean(o * o, axis=-1, keepdims=True) + RMS_EPS) * norm_g.astype(jnp.float32)
    o = o.reshape(B, T, GLA_V_DIM) * jax.nn.silu(r.astype(jnp.float32))
    return o.astype(h.dtype) @ w_out


def shared_kv(x, cs, kv_ada_w, kv_ada_b, w_kv):
    B, T, _ = x.shape
    mod = (cs @ kv_ada_w + kv_ada_b)[:, None, :]
    shift, scale = jnp.split(mod, 2, axis=-1)
    kv = (x * (1.0 + scale) + shift) @ w_kv
    k, v = jnp.split(kv, 2, axis=-1)
    return (k.reshape(B, T, SB_HEADS, SB_HEAD_DIM), v.reshape(B, T, SB_HEADS, SB_HEAD_DIM))


def stick_breaking_attention(q, k, v):
    B, T, H, dh = q.shape
    nb = T // SB_Q_BLOCK
    qf = (q.astype(jnp.float32) * (dh ** -0.5)).reshape(B, nb, SB_Q_BLOCK, H, dh).transpose(1, 0, 3, 2, 4)
    kf = k.astype(jnp.float32).transpose(0, 2, 1, 3)
    vf = v.astype(jnp.float32).transpose(0, 2, 1, 3)
    k_pos = jnp.arange(T)

    def block(args):
        qb, start = args
        q_pos = start + jnp.arange(SB_Q_BLOCK)
        mask = k_pos[None, :] < q_pos[:, None]
        z = jnp.einsum('bhqd,bhkd->bhqk', qb, kf)
        log_beta = jax.nn.log_sigmoid(z)
        log_not = jnp.where(mask, log_beta - z, 0.0)
        later = lax.cumsum(log_not, axis=3, reverse=True) - log_not
        a = jnp.where(mask, jnp.exp(log_beta + later), 0.0)
        return jnp.einsum('bhqk,bhkd->bhqd', a, vf)

    starts = jnp.arange(nb) * SB_Q_BLOCK
    o = lax.map(block, (qf, starts))
    return o.transpose(1, 0, 3, 2, 4).reshape(B, T, H * dh).astype(q.dtype)


def sb_mixer(h, k, v, w_q, w_out):
    B, T, _ = h.shape
    q = (h @ w_q).reshape(B, T, SB_HEADS, SB_HEAD_DIM)
    return stick_breaking_attention(q, k, v) @ w_out


def sq_relu_mlp(h, w1, b1, w2, b2):
    u = jax.nn.relu(h @ w1 + b1)
    return (u * u) @ w2 + b2


def setup_inputs(seed: int = 0) -> dict:
    key = jax.random.key(seed)
    ks = jax.random.split(key, 24)

    def nrm(k, shape, std):
        return jax.random.normal(k, shape, jnp.float32) * std

    d = D_MODEL
    gla_col_scale = jnp.concatenate([
        jnp.ones((2 * GLA_QK_DIM,), jnp.float32),
        jnp.full((GLA_V_DIM,), DEEPNORM_BETA, jnp.float32),
        jnp.ones((GLA_V_DIM + GLA_GATE_RANK,), jnp.float32)])
    kv_col_scale = jnp.concatenate([
        jnp.ones((SB_DIM,), jnp.float32), jnp.full((SB_DIM,), DEEPNORM_BETA, jnp.float32)])
    return {
        "x": nrm(ks[0], (BATCH, SEQ, d), 1.0),
        "c": nrm(ks[1], (BATCH, d), 1.0),
        "ada_w": nrm(ks[2], (DEPTH, d, N_MOD * d), 0.1 * d ** -0.5),
        "ada_b": nrm(ks[3], (DEPTH, N_MOD * d), 0.01),
        "ln_g": 1.0 + nrm(ks[4], (DEPTH, 2, d), 0.02),
        "ln_b": nrm(ks[5], (DEPTH, 2, d), 0.02),
        "gla_w_in": nrm(ks[6], (N_A_LAYERS, d, GLA_IN_DIM), d ** -0.5) * gla_col_scale,
        "gla_w_gate_up": nrm(ks[7], (N_A_LAYERS, GLA_GATE_RANK, GLA_QK_DIM), GLA_GATE_RANK ** -0.5),
        "gla_b_gate": nrm(ks[8], (N_A_LAYERS, GLA_QK_DIM), 0.5),
        "gla_norm_g": 1.0 + nrm(ks[9], (N_A_LAYERS, GLA_DV), 0.02),
        "gla_w_out": nrm(ks[10], (N_A_LAYERS, GLA_V_DIM, d), DEEPNORM_BETA * GLA_V_DIM ** -0.5),
        "kv_ada_w": nrm(ks[11], (d, 2 * d), 0.1 * d ** -0.5),
        "kv_ada_b": nrm(ks[12], (2 * d,), 0.01),
        "w_kv": nrm(ks[13], (d, 2 * SB_DIM), d ** -0.5) * kv_col_scale,
        "sb_w_q": nrm(ks[14], (N_B_LAYERS, d, SB_DIM), d ** -0.5),
        "sb_w_out": nrm(ks[15], (N_B_LAYERS, SB_DIM, d), DEEPNORM_BETA * SB_DIM ** -0.5),
        "mlp_w1": nrm(ks[16], (DEPTH, d, D_FF), d ** -0.5),
        "mlp_b1": nrm(ks[17], (DEPTH, D_FF), 0.02),
        "mlp_w2": nrm(ks[18], (DEPTH, D_FF, d), DEEPNORM_BETA * D_FF ** -0.5),
        "mlp_b2": nrm(ks[19], (DEPTH, d), 0.02),
    }


def reference(x, c, ada_w, ada_b, ln_g, ln_b, gla_w_in, gla_w_gate_up, gla_b_gate, gla_norm_g,
              gla_w_out, kv_ada_w, kv_ada_b, w_kv, sb_w_q, sb_w_out, mlp_w1, mlp_b1, mlp_w2, mlp_b2):
    cs = jax.nn.silu(c)
    k_shared = None
    v_shared = None
    for l in range(DEPTH):
        mod = (cs @ ada_w[l] + ada_b[l])[:, None, :]
        sh_a, sc_a, g_a, sh_m, sc_m, g_m = jnp.split(mod, N_MOD, axis=-1)
        h = x * (1.0 + sc_a) + sh_a
        if l < N_A_LAYERS:
            y = gla_mixer(h, gla_w_in[l], gla_w_gate_up[l], gla_b_gate[l], gla_norm_g[l], gla_w_out[l])
        else:
            if k_shared is None:
                k_shared, v_shared = shared_kv(x, cs, kv_ada_w, kv_ada_b, w_kv)
            j = l - N_A_LAYERS
            y = sb_mixer(h, k_shared, v_shared, sb_w_q[j], sb_w_out[j])
        x = layer_norm(DEEPNORM_ALPHA * x + (1.0 + g_a) * y, ln_g[l, 0], ln_b[l, 0])
        h = x * (1.0 + sc_m) + sh_m
        y = sq_relu_mlp(h, mlp_w1[l], mlp_b1[l], mlp_w2[l], mlp_b2[l])
        x = layer_norm(DEEPNORM_ALPHA * x + (1.0 + g_m) * y, ln_g[l, 1], ln_b[l, 1])
    return x
```

```python
import functools

import jax
import jax.numpy as jnp
from jax import lax
from jax.experimental import pallas as pl
from jax.experimental.pallas import tpu as pltpu

F32 = jnp.float32
BF16 = jnp.bfloat16

GLA_HEADS = 4
GLA_GATE_RANK = 16
GLA_GATE_TAU = 16.0
GLA_CHUNK = 64
SB_HEADS = 16
N_MOD = 6
LN_EPS = 1e-5
RMS_EPS = 1e-5

LANES = 128
MOD_ROWS = 8
VMEM_LIMIT_CAP = 58 << 20
VMEM_HEADROOM = 6 << 20


def _params(semantics, block_bytes):
    limit = min(int(block_bytes) + VMEM_HEADROOM, VMEM_LIMIT_CAP)
    return pltpu.CompilerParams(dimension_semantics=semantics, vmem_limit_bytes=limit)


def _tile(n, want):
    t = min(n, want)
    assert n % t == 0, (n, want)
    return t


def _dot(a, b):
    return jnp.dot(a, b, preferred_element_type=F32)


def _dot_nt(a, b):
    return lax.dot_general(a, b, (((1,), (1,)), ((), ())), preferred_element_type=F32)


def _dot_tn(a, b):
    return lax.dot_general(a, b, (((0,), (0,)), ((), ())), preferred_element_type=F32)


def _split_bf16(x):
    hi = x.astype(BF16)
    lo = (x - hi.astype(F32)).astype(BF16)
    return hi, lo


def _softplus(z):
    return jnp.maximum(z, 0.0) + jnp.log(1.0 + jnp.exp(-jnp.abs(z)))


def _layer_norm(z, g, b):
    mu = jnp.mean(z, axis=-1, keepdims=True)
    zc = z - mu
    var = jnp.mean(zc * zc, axis=-1, keepdims=True)
    return zc * lax.rsqrt(var + LN_EPS) * g + b


def _mod_kernel(c_ref, w_ref, b_ref, o_ref):
    c = c_ref[...]
    cs = c * (1.0 / (1.0 + jnp.exp(-c)))
    o_ref[...] = jnp.dot(cs, w_ref[...], preferred_element_type=F32,
                         precision=lax.Precision.HIGHEST) + b_ref[...]


def _mod_vectors(c_pad, w, b):
    L, D, N = w.shape
    tn = _tile(N, 1024)
    blocks = 2 * (D * tn * 4) + MOD_ROWS * D * 4 + 4 * MOD_ROWS * tn * 4
    return pl.pallas_call(
        _mod_kernel,
        grid=(L, N // tn),
        in_specs=[
            pl.BlockSpec((MOD_ROWS, D), lambda l, j: (0, 0)),
            pl.BlockSpec((None, D, tn), lambda l, j: (l, 0, j)),
            pl.BlockSpec((None, 1, tn), lambda l, j: (l, 0, j)),
        ],
        out_specs=pl.BlockSpec((None, MOD_ROWS, tn), lambda l, j: (l, 0, j)),
        out_shape=jax.ShapeDtypeStruct((L, MOD_ROWS, N), F32),
        compiler_params=_params(("arbitrary", "arbitrary"), blocks),
        name="adaln_vectors",
    )(c_pad, w, b.reshape(L, 1, N))


def _mod_spec(layer, slot, D, n_grid):
    if n_grid == 2:
        return pl.BlockSpec((None, None, 1, D), lambda b, i: (layer, b, 0, slot))
    return pl.BlockSpec((None, None, 1, D), lambda b, i, j: (layer, b, 0, slot))


def _modmm_kernel(x_ref, sh_ref, sc_ref, w_ref, o_ref, h_ref, *, out_scale):
    @pl.when(pl.program_id(2) == 0)
    def _():
        h_ref[...] = (x_ref[...] * (1.0 + sc_ref[...]) + sh_ref[...]).astype(BF16)

    y = _dot(h_ref[...], w_ref[...])
    if out_scale != 1.0:
        y = y * out_scale
    o_ref[...] = y.astype(o_ref.dtype)


def _mod_matmul(x, mod4, layer, sh_slot, sc_slot, w, out_dtype, out_scale=1.0):
    B, T, D = x.shape
    N = w.shape[1]
    tm, tn = _tile(T, 1024), _tile(N, 1024)
    osz = jnp.dtype(out_dtype).itemsize
    blocks = 2 * tm * D * 4 + 2 * D * tn * 2 + 2 * tm * tn * osz + tm * D * 2 + tm * tn * 4
    return pl.pallas_call(
        functools.partial(_modmm_kernel, out_scale=out_scale),
        grid=(B, T // tm, N // tn),
        in_specs=[
            pl.BlockSpec((None, tm, D), lambda b, i, j: (b, i, 0)),
            _mod_spec(layer, sh_slot, D, 3),
            _mod_spec(layer, sc_slot, D, 3),
            pl.BlockSpec((D, tn), lambda b, i, j: (0, j)),
        ],
        out_specs=pl.BlockSpec((None, tm, tn), lambda b, i, j: (b, i, j)),
        out_shape=jax.ShapeDtypeStruct((B, T, N), out_dtype),
        scratch_shapes=[pltpu.VMEM((tm, D), BF16)],
        compiler_params=_params(("arbitrary", "arbitrary", "arbitrary"), blocks),
        name="modulated_matmul",
    )(x, mod4, mod4, w)


def _gla_gate_kernel(x_ref, sh_ref, sc_ref, wg_ref, wup_ref, bg_ref, o_ref):
    h = (x_ref[...] * (1.0 + sc_ref[...]) + sh_ref[...]).astype(BF16)
    gl = _dot(h, wg_ref[...])
    g = _dot(gl.astype(BF16), wup_ref[...]) + bg_ref[...]
    o_ref[...] = (jnp.minimum(g, 0.0) - jnp.log(1.0 + jnp.exp(-jnp.abs(g)))) * (1.0 / GLA_GATE_TAU)


def _gla_log_decay(x, mod4, layer, w_gate, w_up, b_gate):
    B, T, D = x.shape
    R, QK = w_up.shape
    tm = _tile(T, 512)
    blocks = 2 * tm * D * 4 + tm * D * 2 + 2 * D * R * 2 + 2 * R * QK * 2 + 4 * tm * QK * 4
    return pl.pallas_call(
        _gla_gate_kernel,
        grid=(B, T // tm),
        in_specs=[
            pl.BlockSpec((None, tm, D), lambda b, i: (b, i, 0)),
            _mod_spec(layer, 0, D, 2),
            _mod_spec(layer, 1, D, 2),
            pl.BlockSpec((D, R), lambda b, i: (0, 0)),
            pl.BlockSpec((R, QK), lambda b, i: (0, 0)),
            pl.BlockSpec((1, QK), lambda b, i: (0, 0)),
        ],
        out_specs=pl.BlockSpec((None, tm, QK), lambda b, i: (b, i, 0)),
        out_shape=jax.ShapeDtypeStruct((B, T, QK), F32),
        compiler_params=_params(("arbitrary", "arbitrary"), blocks),
        name="gla_log_decay",
    )(x, mod4, mod4, w_gate, w_up, b_gate.reshape(1, QK))


def _gla_rec_kernel(q_ref, k_ref, v_ref, r_ref, la_ref, ng_ref, o_ref, st_ref, *, n_chunks, dk):
    C = GLA_CHUNK
    tb = n_chunks * C

    @pl.when(pl.program_id(2) == 0)
    def _():
        st_ref[...] = jnp.zeros_like(st_ref)

    row = lax.broadcasted_iota(jnp.int32, (tb, tb), 0)
    col = lax.broadcasted_iota(jnp.int32, (tb, tb), 1)
    shift = C.bit_length() - 1
    tri = ((col <= row) & ((row >> shift) == (col >> shift))).astype(BF16)
    la_hi, la_lo = _split_bf16(la_ref[...])
    bc = _dot(tri, la_hi) + _dot(tri, la_lo)

    k = k_ref[...]
    q_dec = (q_ref[...] * (dk ** -0.5) * jnp.exp(bc)).astype(BF16)
    k_intra = (k * jnp.exp(-bc)).astype(BF16)
    crow = lax.broadcasted_iota(jnp.int32, (C, C), 0)
    ccol = lax.broadcasted_iota(jnp.int32, (C, C), 1)
    causal = ccol <= crow
    ng = ng_ref[...]

    for c in range(n_chunks):
        sl = slice(c * C, (c + 1) * C)
        b_last = bc[(c + 1) * C - 1:(c + 1) * C, :]
        k_state = (k[sl] * jnp.exp(b_last - bc[sl])).astype(BF16)
        decay = jnp.exp(b_last)
        v_c = v_ref[sl, :].astype(BF16)
        scores = jnp.where(causal, _dot_nt(q_dec[sl], k_intra[sl]), 0.0)
        st = st_ref[...]
        o = _dot(scores.astype(BF16), v_c) + _dot_nt(q_dec[sl], st.astype(BF16))
        st_ref[...] = st * decay + _dot_tn(v_c, k_state)
        o = o * lax.rsqrt(jnp.mean(o * o, axis=-1, keepdims=True) + RMS_EPS) * ng
        r = r_ref[sl, :]
        o_ref[sl, :] = (o * (r * (1.0 / (1.0 + jnp.exp(-r))))).astype(o_ref.dtype)


def _gla_recurrence(proj, log_a, norm_g):
    B, T, _ = proj.shape
    QK = log_a.shape[-1]
    dk = QK // GLA_HEADS
    dv = norm_g.shape[-1]
    V = dv * GLA_HEADS
    assert dk % LANES == 0 and dv % LANES == 0 and QK % dk == 0 and (2 * QK) % dv == 0
    n_chunks = min(8, T // GLA_CHUNK)
    tb = n_chunks * GLA_CHUNK
    assert T % tb == 0
    kq, kk = 0, QK // dk
    kv, kr = 2 * QK // dv, (2 * QK + V) // dv
    blocks = (2 * (2 * tb * dk * 4 + 2 * tb * dv * 4 + tb * dk * 4 + tb * dv * 2) + dv * dk * 4
              + 3 * tb * tb * 4 + 6 * tb * dk * 4)
    return pl.pallas_call(
        functools.partial(_gla_rec_kernel, n_chunks=n_chunks, dk=dk),
        grid=(B, GLA_HEADS, T // tb),
        in_specs=[
            pl.BlockSpec((None, tb, dk), lambda b, h, t: (b, t, kq + h)),
            pl.BlockSpec((None, tb, dk), lambda b, h, t: (b, t, kk + h)),
            pl.BlockSpec((None, tb, dv), lambda b, h, t: (b, t, kv + h)),
            pl.BlockSpec((None, tb, dv), lambda b, h, t: (b, t, kr + h)),
            pl.BlockSpec((None, tb, dk), lambda b, h, t: (b, t, h)),
            pl.BlockSpec((1, dv), lambda b, h, t: (0, 0)),
        ],
        out_specs=pl.BlockSpec((None, tb, dv), lambda b, h, t: (b, t, h)),
        out_shape=jax.ShapeDtypeStruct((B, T, V), BF16),
        scratch_shapes=[pltpu.VMEM((dv, dk), F32)],
        compiler_params=_params(("arbitrary", "arbitrary", "arbitrary"), blocks),
        name="gla_recurrence",
    )(proj, proj, proj, proj, log_a, norm_g.reshape(1, dv))


def _proj_ln_kernel(a_ref, w_ref, x_ref, g_ref, lg_ref, lb_ref, o_ref, *, alpha):
    y = _dot(a_ref[...], w_ref[...])
    z = alpha * x_ref[...] + (1.0 + g_ref[...]) * y
    o_ref[...] = _layer_norm(z, lg_ref[...], lb_ref[...])


def _proj_residual_ln(a, w, x, mod4, layer, gate_slot, ln_g, ln_b, alpha):
    B, T, D = x.shape
    K = a.shape[-1]
    tm = _tile(T, 512)
    blocks = 2 * tm * K * 2 + 2 * K * D * 2 + 4 * tm * D * 4 + 3 * tm * D * 4
    return pl.pallas_call(
        functools.partial(_proj_ln_kernel, alpha=alpha),
        grid=(B, T // tm),
        in_specs=[
            pl.BlockSpec((None, tm, K), lambda b, i: (b, i, 0)),
            pl.BlockSpec((K, D), lambda b, i: (0, 0)),
            pl.BlockSpec((None, tm, D), lambda b, i: (b, i, 0)),
            _mod_spec(layer, gate_slot, D, 2),
            pl.BlockSpec((1, D), lambda b, i: (0, 0)),
            pl.BlockSpec((1, D), lambda b, i: (0, 0)),
        ],
        out_specs=pl.BlockSpec((None, tm, D), lambda b, i: (b, i, 0)),
        out_shape=jax.ShapeDtypeStruct((B, T, D), F32),
        compiler_params=_params(("arbitrary", "arbitrary"), blocks),
        name="proj_residual_ln",
    )(a, w, x, mod4, ln_g.reshape(1, D), ln_b.reshape(1, D))


def _mlp_kernel(x_ref, sh_ref, sc_ref, g_ref, w1_ref, b1_ref, w2_ref, b2_ref, lg_ref, lb_ref,
                o_ref, h_ref, acc_ref, *, alpha):
    f = pl.program_id(2)

    @pl.when(f == 0)
    def _():
        h_ref[...] = (x_ref[...] * (1.0 + sc_ref[...]) + sh_ref[...]).astype(BF16)
        acc_ref[...] = jnp.zeros_like(acc_ref)

    u = jnp.maximum(_dot(h_ref[...], w1_ref[...]) + b1_ref[...], 0.0)
    acc_ref[...] += _dot((u * u).astype(BF16), w2_ref[...])

    @pl.when(f == pl.num_programs(2) - 1)
    def _():
        y = acc_ref[...] + b2_ref[...]
        z = alpha * x_ref[...] + (1.0 + g_ref[...]) * y
        o_ref[...] = _layer_norm(z, lg_ref[...], lb_ref[...])


def _mlp_block(x, mod4, layer, w1, b1, w2, b2, ln_g, ln_b, alpha):
    B, T, D = x.shape
    F = w1.shape[1]
    tm, tf = _tile(T, 512), _tile(F, 1024)
    blocks = (4 * tm * D * 4 + tm * D * 2 + tm * D * 4 + 4 * D * tf * 2 + tm * tf * 6 + 2 * tm * D * 4)
    return pl.pallas_call(
        functools.partial(_mlp_kernel, alpha=alpha),
        grid=(B, T // tm, F // tf),
        in_specs=[
            pl.BlockSpec((None, tm, D), lambda b, i, f: (b, i, 0)),
            _mod_spec(layer, 3, D, 3),
            _mod_spec(layer, 4, D, 3),
            _mod_spec(layer, 5, D, 3),
            pl.BlockSpec((D, tf), lambda b, i, f: (0, f)),
            pl.BlockSpec((1, tf), lambda b, i, f: (0, f)),
            pl.BlockSpec((tf, D), lambda b, i, f: (f, 0)),
            pl.BlockSpec((1, D), lambda b, i, f: (0, 0)),
            pl.BlockSpec((1, D), lambda b, i, f: (0, 0)),
            pl.BlockSpec((1, D), lambda b, i, f: (0, 0)),
        ],
        out_specs=pl.BlockSpec((None, tm, D), lambda b, i, f: (b, i, 0)),
        out_shape=jax.ShapeDtypeStruct((B, T, D), F32),
        scratch_shapes=[pltpu.VMEM((tm, D), BF16), pltpu.VMEM((tm, D), F32)],
        compiler_params=_params(("arbitrary", "arbitrary", "arbitrary"), blocks),
        name="mlp_block",
    )(x, mod4, mod4, mod4, w1, b1.reshape(1, F), w2, b2.reshape(1, D), ln_g.reshape(1, D), ln_b.reshape(1, D))


SB_TILE = 512
SB_SUB = 256


def _sb_attn_kernel(q_ref, k_ref, v_ref, o_ref, *, tq):
    qi = pl.program_id(2)
    q = q_ref[...]
    n_sub = tq // SB_SUB
    jrow = lax.broadcasted_iota(jnp.int32, (SB_SUB, SB_SUB), 0)
    jcol = lax.broadcasted_iota(jnp.int32, (SB_SUB, SB_SUB), 1)
    later = (jrow > jcol).astype(BF16)

    def sub_block(start, r_neg, acc, mask):
        kb = k_ref[pl.ds(start, SB_SUB), :]
        vb = v_ref[pl.ds(start, SB_SUB), :]
        z = _dot_nt(q, kb)
        sp = _softplus(z)
        spm = sp if mask is None else jnp.where(mask, sp, 0.0)
        hi, lo = _split_bf16(spm)
        sfx = _dot(hi, later) + _dot(lo, later)
        a = jnp.exp(z - sp - (sfx + r_neg))
        if mask is not None:
            a = jnp.where(mask, a, 0.0)
        acc = acc + _dot(a.astype(BF16), vb)
        r_neg = r_neg + sfx[:, 0:1] + spm[:, 0:1]
        return r_neg, acc

    r_neg = jnp.zeros((tq, 1), F32)
    acc = jnp.zeros((tq, q.shape[-1]), F32)

    trow = lax.broadcasted_iota(jnp.int32, (tq, SB_SUB), 0)
    tcol = lax.broadcasted_iota(jnp.int32, (tq, SB_SUB), 1)
    base = pl.multiple_of(qi * tq, tq)
    for s in reversed(range(n_sub)):
        r_neg, acc = sub_block(base + s * SB_SUB, r_neg, acc, (tcol + s * SB_SUB) < trow)

    def tile_body(it, carry):
        r_neg, acc = carry
        start = pl.multiple_of((qi - 1 - it) * tq, tq)
        for s in reversed(range(n_sub)):
            r_neg, acc = sub_block(start + s * SB_SUB, r_neg, acc, None)
        return r_neg, acc

    r_neg, acc = lax.fori_loop(0, qi, tile_body, (r_neg, acc))
    o_ref[...] = acc.astype(o_ref.dtype)


def _sb_attention(q, kv):
    B, T, HD = q.shape
    dh = HD // SB_HEADS
    assert dh % LANES == 0
    tq = _tile(T, SB_TILE)
    assert tq % SB_SUB == 0
    blocks = 4 * T * dh * 2 + 4 * tq * dh * 2 + 16 * tq * SB_SUB * 4
    return pl.pallas_call(
        functools.partial(_sb_attn_kernel, tq=tq),
        grid=(B, SB_HEADS, T // tq),
        in_specs=[
            pl.BlockSpec((None, tq, dh), lambda b, h, i: (b, i, h)),
            pl.BlockSpec((None, T, dh), lambda b, h, i: (b, 0, h)),
            pl.BlockSpec((None, T, dh), lambda b, h, i: (b, 0, SB_HEADS + h)),
        ],
        out_specs=pl.BlockSpec((None, tq, dh), lambda b, h, i: (b, i, h)),
        out_shape=jax.ShapeDtypeStruct((B, T, HD), BF16),
        compiler_params=_params(("arbitrary", "arbitrary", "arbitrary"), blocks),
        name="stick_breaking_attention",
    )(q, kv, kv)


def kernel(x, c, ada_w, ada_b, ln_g, ln_b, gla_w_in, gla_w_gate_up, gla_b_gate, gla_norm_g, gla_w_out, kv_ada_w, kv_ada_b, w_kv, sb_w_q, sb_w_out, mlp_w1, mlp_b1, mlp_w2, mlp_b2):
    B, T, D = x.shape
    depth = ada_w.shape[0]
    n_a = gla_w_in.shape[0]
    alpha = (2.0 * depth) ** 0.25
    QK = gla_w_gate_up.shape[-1]
    V = gla_w_out.shape[1]
    n_main = 2 * QK + 2 * V
    dh = sb_w_q.shape[-1] // SB_HEADS

    c_pad = jnp.pad(c, ((0, MOD_ROWS - B), (0, 0)))
    mod4 = _mod_vectors(c_pad, ada_w, ada_b).reshape(depth, MOD_ROWS, 1, N_MOD * D)
    kv_mod4 = _mod_vectors(c_pad, kv_ada_w[None], kv_ada_b[None]).reshape(1, MOD_ROWS, 1, 2 * D)

    kv = None
    for l in range(depth):
        if l < n_a:
            w_in = gla_w_in[l]
            proj = _mod_matmul(x, mod4, l, 0, 1, w_in[:, :n_main].astype(BF16), F32)
            pad = LANES - GLA_GATE_RANK
            w_gate = jnp.pad(w_in[:, n_main:], ((0, 0), (0, pad))).astype(BF16)
            w_up = jnp.pad(gla_w_gate_up[l], ((0, pad), (0, 0))).astype(BF16)
            log_a = _gla_log_decay(x, mod4, l, w_gate, w_up, gla_b_gate[l])
            mixed = _gla_recurrence(proj, log_a, gla_norm_g[l])
            w_out = gla_w_out[l]
        else:
            if kv is None:
                kv = _mod_matmul(x, kv_mod4, 0, 0, 1, w_kv.astype(BF16), BF16)
            j = l - n_a
            q = _mod_matmul(x, mod4, l, 0, 1, sb_w_q[j].astype(BF16), BF16, out_scale=dh ** -0.5)
            mixed = _sb_attention(q, kv)
            w_out = sb_w_out[j]
        x = _proj_residual_ln(mixed, w_out.astype(BF16), x, mod4, l, 2, ln_g[l, 0], ln_b[l, 0], alpha)
        x = _mlp_block(x, mod4, l, mlp_w1[l].astype(BF16), mlp_b1[l], mlp_w2[l].astype(BF16), mlp_b2[l],
                       ln_g[l, 1], ln_b[l, 1], alpha)
    return x
```

```python
import functools

import jax
import jax.numpy as jnp
from jax import lax
from jax.experimental import pallas as pl
from jax.experimental.pallas import tpu as pltpu

F32 = jnp.float32
BF16 = jnp.bfloat16

GLA_HEADS = 4
GLA_GATE_RANK = 16
GLA_GATE_TAU = 16.0
GLA_CHUNK = 64
SB_HEADS = 16
N_MOD = 6
LN_EPS = 1e-5
RMS_EPS = 1e-5

LANES = 128
MOD_ROWS = 8
VMEM_LIMIT_CAP = 58 << 20
VMEM_HEADROOM = 6 << 20


def _params(semantics, block_bytes):
    limit = min(int(block_bytes) + VMEM_HEADROOM, VMEM_LIMIT_CAP)
    return pltpu.CompilerParams(dimension_semantics=semantics, vmem_limit_bytes=limit)


def _tile(n, want):
    t = min(n, want)
    assert n % t == 0, (n, want)
    return t


def _dot(a, b):
    return jnp.dot(a, b, preferred_element_type=F32)


def _dot_nt(a, b):
    return lax.dot_general(a, b, (((1,), (1,)), ((), ())), preferred_element_type=F32)


def _dot_tn(a, b):
    return lax.dot_general(a, b, (((0,), (0,)), ((), ())), preferred_element_type=F32)


def _split_bf16(x):
    hi = x.astype(BF16)
    lo = (x - hi.astype(F32)).astype(BF16)
    return hi, lo


def _softplus(z):
    return jnp.maximum(z, 0.0) + jnp.log(1.0 + jnp.exp(-jnp.abs(z)))


def _layer_norm(z, g, b):
    mu = jnp.mean(z, axis=-1, keepdims=True)
    zc = z - mu
    var = jnp.mean(zc * zc, axis=-1, keepdims=True)
    return zc * lax.rsqrt(var + LN_EPS) * g + b


def _mod_kernel(c_ref, w_ref, b_ref, o_ref):
    c = c_ref[...]
    cs = c * (1.0 / (1.0 + jnp.exp(-c)))
    o_ref[...] = jnp.dot(cs, w_ref[...], preferred_element_type=F32,
                         precision=lax.Precision.HIGHEST) + b_ref[...]


def _mod_vectors(c_pad, w, b):
    L, D, N = w.shape
    tn = _tile(N, 1024)
    blocks = 2 * (D * tn * 4) + MOD_ROWS * D * 4 + 4 * MOD_ROWS * tn * 4
    return pl.pallas_call(
        _mod_kernel,
        grid=(L, N // tn),
        in_specs=[
            pl.BlockSpec((MOD_ROWS, D), lambda l, j: (0, 0)),
            pl.BlockSpec((None, D, tn), lambda l, j: (l, 0, j)),
            pl.BlockSpec((None, 1, tn), lambda l, j: (l, 0, j)),
        ],
        out_specs=pl.BlockSpec((None, MOD_ROWS, tn), lambda l, j: (l, 0, j)),
        out_shape=jax.ShapeDtypeStruct((L, MOD_ROWS, N), F32),
        compiler_params=_params(("arbitrary", "arbitrary"), blocks),
        name="adaln_vectors",
    )(c_pad, w, b.reshape(L, 1, N))


def _mod_spec(layer, slot, D, n_grid):
    if n_grid == 2:
        return pl.BlockSpec((None, None, 1, D), lambda b, i: (layer, b, 0, slot))
    return pl.BlockSpec((None, None, 1, D), lambda b, i, j: (layer, b, 0, slot))


def _modmm_kernel(x_ref, sh_ref, sc_ref, w_ref, o_ref, h_ref, *, out_scale):
    @pl.when(pl.program_id(2) == 0)
    def _():
        h_ref[...] = (x_ref[...] * (1.0 + sc_ref[...]) + sh_ref[...]).astype(BF16)

    y = _dot(h_ref[...], w_ref[...])
    if out_scale != 1.0:
        y = y * out_scale
    o_ref[...] = y.astype(o_ref.dtype)


def _mod_matmul(x, mod4, layer, sh_slot, sc_slot, w, out_dtype, out_scale=1.0):
    B, T, D = x.shape
    N = w.shape[1]
    tm, tn = _tile(T, 1024), _tile(N, 1024)
    osz = jnp.dtype(out_dtype).itemsize
    blocks = 2 * tm * D * 4 + 2 * D * tn * 2 + 2 * tm * tn * osz + tm * D * 2 + tm * tn * 4
    return pl.pallas_call(
        functools.partial(_modmm_kernel, out_scale=out_scale),
        grid=(B, T // tm, N // tn),
        in_specs=[
            pl.BlockSpec((None, tm, D), lambda b, i, j: (b, i, 0)),
            _mod_spec(layer, sh_slot, D, 3),
            _mod_spec(layer, sc_slot, D, 3),
            pl.BlockSpec((D, tn), lambda b, i, j: (0, j)),
        ],
        out_specs=pl.BlockSpec((None, tm, tn), lambda b, i, j: (b, i, j)),
        out_shape=jax.ShapeDtypeStruct((B, T, N), out_dtype),
        scratch_shapes=[pltpu.VMEM((tm, D), BF16)],
        compiler_params=_params(("arbitrary", "arbitrary", "arbitrary"), blocks),
        name="modulated_matmul",
    )(x, mod4, mod4, w)


def _gla_gate_kernel(x_ref, sh_ref, sc_ref, wg_ref, wup_ref, bg_ref, o_ref):
    h = (x_ref[...] * (1.0 + sc_ref[...]) + sh_ref[...]).astype(BF16)
    gl = _dot(h, wg_ref[...])
    g = _dot(gl.astype(BF16), wup_ref[...]) + bg_ref[...]
    o_ref[...] = (jnp.minimum(g, 0.0) - jnp.log(1.0 + jnp.exp(-jnp.abs(g)))) * (1.0 / GLA_GATE_TAU)


def _gla_log_decay(x, mod4, layer, w_gate, w_up, b_gate):
    B, T, D = x.shape
    R, QK = w_up.shape
    tm = _tile(T, 512)
    blocks = 2 * tm * D * 4 + tm * D * 2 + 2 * D * R * 2 + 2 * R * QK * 2 + 4 * tm * QK * 4
    return pl.pallas_call(
        _gla_gate_kernel,
        grid=(B, T // tm),
        in_specs=[
            pl.BlockSpec((None, tm, D), lambda b, i: (b, i, 0)),
            _mod_spec(layer, 0, D, 2),
            _mod_spec(layer, 1, D, 2),
            pl.BlockSpec((D, R), lambda b, i: (0, 0)),
            pl.BlockSpec((R, QK), lambda b, i: (0, 0)),
            pl.BlockSpec((1, QK), lambda b, i: (0, 0)),
        ],
        out_specs=pl.BlockSpec((None, tm, QK), lambda b, i: (b, i, 0)),
        out_shape=jax.ShapeDtypeStruct((B, T, QK), F32),
        compiler_params=_params(("arbitrary", "arbitrary"), blocks),
        name="gla_log_decay",
    )(x, mod4, mod4, w_gate, w_up, b_gate.reshape(1, QK))


def _gla_rec_kernel(q_ref, k_ref, v_ref, r_ref, la_ref, ng_ref, o_ref, st_ref, *, n_chunks, dk):
    C = GLA_CHUNK
    tb = n_chunks * C

    @pl.when(pl.program_id(2) == 0)
    def _():
        st_ref[...] = jnp.zeros_like(st_ref)

    row = lax.broadcasted_iota(jnp.int32, (tb, tb), 0)
    col = lax.broadcasted_iota(jnp.int32, (tb, tb), 1)
    shift = C.bit_length() - 1
    tri = ((col <= row) & ((row >> shift) == (col >> shift))).astype(BF16)
    la_hi, la_lo = _split_bf16(la_ref[...])
    bc = _dot(tri, la_hi) + _dot(tri, la_lo)

    k = k_ref[...]
    q_dec = (q_ref[...] * (dk ** -0.5) * jnp.exp(bc)).astype(BF16)
    k_intra = (k * jnp.exp(-bc)).astype(BF16)
    crow = lax.broadcasted_iota(jnp.int32, (C, C), 0)
    ccol = lax.broadcasted_iota(jnp.int32, (C, C), 1)
    causal = ccol <= crow
    ng = ng_ref[...]

    for c in range(n_chunks):
        sl = slice(c * C, (c + 1) * C)
        b_last = bc[(c + 1) * C - 1:(c + 1) * C, :]
        k_state = (k[sl] * jnp.exp(b_last - bc[sl])).astype(BF16)
        decay = jnp.exp(b_last)
        v_c = v_ref[sl, :].astype(BF16)
        scores = jnp.where(causal, _dot_nt(q_dec[sl], k_intra[sl]), 0.0)
        st = st_ref[...]
        o = _dot(scores.astype(BF16), v_c) + _dot_nt(q_dec[sl], st.astype(BF16))
        st_ref[...] = st * decay + _dot_tn(v_c, k_state)
        o = o * lax.rsqrt(jnp.mean(o * o, axis=-1, keepdims=True) + RMS_EPS) * ng
        r = r_ref[sl, :]
        o_ref[sl, :] = (o * (r * (1.0 / (1.0 + jnp.exp(-r))))).astype(o_ref.dtype)


def _gla_recurrence(proj, log_a, norm_g):
    B, T, _ = proj.shape
    QK = log_a.shape[-1]
    dk = QK // GLA_HEADS
    dv = norm_g.shape[-1]
    V = dv * GLA_HEADS
    assert dk % LANES == 0 and dv % LANES == 0 and QK % dk == 0 and (2 * QK) % dv == 0
    n_chunks = min(8, T // GLA_CHUNK)
    tb = n_chunks * GLA_CHUNK
    assert T % tb == 0
    kq, kk = 0, QK // dk
    kv, kr = 2 * QK // dv, (2 * QK + V) // dv
    blocks = (2 * (2 * tb * dk * 4 + 2 * tb * dv * 4 + tb * dk * 4 + tb * dv * 2) + dv * dk * 4
              + 3 * tb * tb * 4 + 6 * tb * dk * 4)
    return pl.pallas_call(
        functools.partial(_gla_rec_kernel, n_chunks=n_chunks, dk=dk),
        grid=(B, GLA_HEADS, T // tb),
        in_specs=[
            pl.BlockSpec((None, tb, dk), lambda b, h, t: (b, t, kq + h)),
            pl.BlockSpec((None, tb, dk), lambda b, h, t: (b, t, kk + h)),
            pl.BlockSpec((None, tb, dv), lambda b, h, t: (b, t, kv + h)),
            pl.BlockSpec((None, tb, dv), lambda b, h, t: (b, t, kr + h)),
            pl.BlockSpec((None, tb, dk), lambda b, h, t: (b, t, h)),
            pl.BlockSpec((1, dv), lambda b, h, t: (0, 0)),
        ],
        out_specs=pl.BlockSpec((None, tb, dv), lambda b, h, t: (b, t, h)),
        out_shape=jax.ShapeDtypeStruct((B, T, V), BF16),
        scratch_shapes=[pltpu.VMEM((dv, dk), F32)],
        compiler_params=_params(("arbitrary", "arbitrary", "arbitrary"), blocks),
        name="gla_recurrence",
    )(proj, proj, proj, proj, log_a, norm_g.reshape(1, dv))


def _proj_ln_kernel(a_ref, w_ref, x_ref, g_ref, lg_ref, lb_ref, o_ref, *, alpha):
    y = _dot(a_ref[...], w_ref[...])
    z = alpha * x_ref[...] + (1.0 + g_ref[...]) * y
    o_ref[...] = _layer_norm(z, lg_ref[...], lb_ref[...])


def _proj_residual_ln(a, w, x, mod4, layer, gate_slot, ln_g, ln_b, alpha):
    B, T, D = x.shape
    K = a.shape[-1]
    tm = _tile(T, 512)
    blocks = 2 * tm * K * 2 + 2 * K * D * 2 + 4 * tm * D * 4 + 3 * tm * D * 4
    return pl.pallas_call(
        functools.partial(_proj_ln_kernel, alpha=alpha),
        grid=(B, T // tm),
        in_specs=[
            pl.BlockSpec((None, tm, K), lambda b, i: (b, i, 0)),
            pl.BlockSpec((K, D), lambda b, i: (0, 0)),
            pl.BlockSpec((None, tm, D), lambda b, i: (b, i, 0)),
            _mod_spec(layer, gate_slot, D, 2),
            pl.BlockSpec((1, D), lambda b, i: (0, 0)),
            pl.BlockSpec((1, D), lambda b, i: (0, 0)),
        ],
        out_specs=pl.BlockSpec((None, tm, D), lambda b, i: (b, i, 0)),
        out_shape=jax.ShapeDtypeStruct((B, T, D), F32),
        compiler_params=_params(("arbitrary", "arbitrary"), blocks),
        name="proj_residual_ln",
    )(a, w, x, mod4, ln_g.reshape(1, D), ln_b.reshape(1, D))


def _mlp_kernel(x_ref, sh_ref, sc_ref, g_ref, w1_ref, b1_ref, w2_ref, b2_ref, lg_ref, lb_ref,
                o_ref, h_ref, acc_ref, *, alpha):
    f = pl.program_id(2)

    @pl.when(f == 0)
    def _():
        h_ref[...] = (x_ref[...] * (1.0 + sc_ref[...]) + sh_ref[...]).astype(BF16)
        acc_ref[...] = jnp.zeros_like(acc_ref)

    u = jnp.maximum(_dot(h_ref[...], w1_ref[...]) + b1_ref[...], 0.0)
    acc_ref[...] += _dot((u * u).astype(BF16), w2_ref[...])

    @pl.when(f == pl.num_programs(2) - 1)
    def _():
        y = acc_ref[...] + b2_ref[...]
        z = alpha * x_ref[...] + (1.0 + g_ref[...]) * y
        o_ref[...] = _layer_norm(z, lg_ref[...], lb_ref[...])


def _mlp_block(x, mod4, layer, w1, b1, w2, b2, ln_g, ln_b, alpha):
    B, T, D = x.shape
    F = w1.shape[1]
    tm, tf = _tile(T, 512), _tile(F, 1024)
    blocks = (4 * tm * D * 4 + tm * D * 2 + tm * D * 4 + 4 * D * tf * 2 + tm * tf * 6 + 2 * tm * D * 4)
    return pl.pallas_call(
        functools.partial(_mlp_kernel, alpha=alpha),
        grid=(B, T // tm, F // tf),
        in_specs=[
            pl.BlockSpec((None, tm, D), lambda b, i, f: (b, i, 0)),
            _mod_spec(layer, 3, D, 3),
            _mod_spec(layer, 4, D, 3),
            _mod_spec(layer, 5, D, 3),
            pl.BlockSpec((D, tf), lambda b, i, f: (0, f)),
            pl.BlockSpec((1, tf), lambda b, i, f: (0, f)),
            pl.BlockSpec((tf, D), lambda b, i, f: (f, 0)),
            pl.BlockSpec((1, D), lambda b, i, f: (0, 0)),
            pl.BlockSpec((1, D), lambda b, i, f: (0, 0)),
            pl.BlockSpec((1, D), lambda b, i, f: (0, 0)),
        ],
        out_specs=pl.BlockSpec((None, tm, D), lambda b, i, f: (b, i, 0)),
        out_shape=jax.ShapeDtypeStruct((B, T, D), F32),
        scratch_shapes=[pltpu.VMEM((tm, D), BF16), pltpu.VMEM((tm, D), F32)],
        compiler_params=_params(("arbitrary", "arbitrary", "arbitrary"), blocks),
        name="mlp_block",
    )(x, mod4, mod4, mod4, w1, b1.reshape(1, F), w2, b2.reshape(1, D), ln_g.reshape(1, D), ln_b.reshape(1, D))


SB_BLK = 256
LOG2_E = 1.4426950408889634
LN_2 = 0.6931471805599453
SB_DEAD_LOG2 = 160.0
SB_NO_KEYS = 1e30


def _sb_attn_kernel(q_ref, k_ref, v_ref, o_ref, *, n_pairs):
    n = SB_BLK
    dh = q_ref.shape[-1]
    jrow = lax.broadcasted_iota(jnp.int32, (n, n), 0)
    jcol = lax.broadcasted_iota(jnp.int32, (n, n), 1)
    later = (jrow > jcol).astype(BF16)
    diag_mask = jcol < jrow

    def scores(q, blk):
        return _dot_nt(q, k_ref[pl.ds(pl.multiple_of(blk * n, n), n), :])

    later2 = jnp.concatenate([later, later], axis=0)

    def log_terms(z, mask):
        neg_abs = lax.bitcast_convert_type(
            lax.bitcast_convert_type(z, jnp.uint32) | jnp.uint32(0x80000000), F32)
        sp = jnp.maximum(z, 0.0) + jnp.log(1.0 + jnp.exp2(neg_abs)) * LOG2_E
        lb = z - sp
        if mask is not None:
            sp = jnp.where(mask, sp, 0.0)
        hi, lo = _split_bf16(sp)
        return jnp.concatenate([hi, lo], axis=1), lb, sp[:, 0:1]

    def suffix(hilo):
        return _dot(hilo, later2)

    def weights(lb, sfx, r, sp0, mask):
        a = jnp.exp2(lb - sfx if r is None else lb - (sfx + r))
        if mask is not None:
            a = jnp.where(mask, a, 0.0)
        tot = sfx[:, 0:1] + sp0
        return a.astype(BF16), tot if r is None else r + tot

    def values(a, blk):
        return _dot(a, v_ref[pl.ds(pl.multiple_of(blk * n, n), n), :])

    def step(qa, qb, blk_a, blk_b, ra, rb, acc_a, acc_b, mask):
        za = scores(qa, blk_a)
        zb = scores(qb, blk_b)
        hla, lba, spa0 = log_terms(za, mask)
        sfa = suffix(hla)
        hlb, lbb, spb0 = log_terms(zb, mask)
        aa, ra = weights(lba, sfa, ra, spa0, mask)
        sfb = suffix(hlb)
        acc_a = acc_a + values(aa, blk_a)
        ab, rb = weights(lbb, sfb, rb, spb0, mask)
        acc_b = acc_b + values(ab, blk_b)
        return ra, rb, acc_a, acc_b

    def pair_body(p, _):
        blk_a0 = 2 * p
        blk_b0 = blk_a0 + 1
        row_a = pl.multiple_of(blk_a0 * n, n)
        row_b = pl.multiple_of(blk_b0 * n, n)
        qa = q_ref[pl.ds(row_a, n), :]
        qb = q_ref[pl.ds(row_b, n), :]

        blk_a1 = jnp.maximum(blk_a0 - 1, 0)
        blk_b1 = blk_a0
        za0 = scores(qa, blk_a0)
        zb0 = scores(qb, blk_b0)
        za1 = scores(qa, blk_a1)
        zb1 = scores(qb, blk_b1)
        hla0, lba0, spa0 = log_terms(za0, diag_mask)
        sfa0 = suffix(hla0)
        hlb0, lbb0, spb0 = log_terms(zb0, diag_mask)
        sfb0 = suffix(hlb0)
        hla1, lba1, spa1 = log_terms(za1, None)
        aa0, ra = weights(lba0, sfa0, None, spa0, diag_mask)
        sfa1 = suffix(hla1)
        acc_a = values(aa0, blk_a0)
        hlb1, lbb1, spb1 = log_terms(zb1, None)
        ab0, rb = weights(lbb0, sfb0, None, spb0, diag_mask)
        sfb1 = suffix(hlb1)
        acc_b = values(ab0, blk_b0)
        ra = jnp.where(p > 0, ra, SB_NO_KEYS)
        aa1, ra = weights(lba1, sfa1, ra, spa1, None)
        acc_a = acc_a + values(aa1, blk_a1)
        ab1, rb = weights(lbb1, sfb1, rb, spb1, None)
        acc_b = acc_b + values(ab1, blk_b1)

        def alive(ra, rb):
            return jnp.minimum(jnp.min(ra), jnp.min(rb)) < SB_DEAD_LOG2

        def cond(carry):
            i, go = carry[0], carry[1]
            return jnp.logical_and(i <= blk_b0, go)

        def body(carry):
            i, _, ra, rb, acc_a, acc_b = carry
            has_a = i <= blk_a0
            ra = jnp.where(has_a, ra, SB_NO_KEYS)
            ra, rb, acc_a, acc_b = step(qa, qb, jnp.maximum(blk_a0 - i, 0), blk_b0 - i,
                                        ra, rb, acc_a, acc_b, None)
            return i + 1, alive(ra, rb), ra, rb, acc_a, acc_b

        carry = lax.while_loop(cond, body, (jnp.int32(2), alive(ra, rb), ra, rb, acc_a, acc_b))
        o_ref[pl.ds(row_a, n), :] = carry[4].astype(o_ref.dtype)
        o_ref[pl.ds(row_b, n), :] = carry[5].astype(o_ref.dtype)
        return 0

    lax.fori_loop(0, n_pairs, pair_body, 0)


def _sb_attention(q, kv):
    B, T, HD = q.shape
    dh = HD // SB_HEADS
    assert dh % LANES == 0 and T % (2 * SB_BLK) == 0
    blocks = 8 * T * dh * 2 + 24 * SB_BLK * SB_BLK * 4
    return pl.pallas_call(
        functools.partial(_sb_attn_kernel, n_pairs=T // (2 * SB_BLK)),
        grid=(B, SB_HEADS),
        in_specs=[
            pl.BlockSpec((None, T, dh), lambda b, h: (b, 0, h)),
            pl.BlockSpec((None, T, dh), lambda b, h: (b, 0, h)),
            pl.BlockSpec((None, T, dh), lambda b, h: (b, 0, SB_HEADS + h)),
        ],
        out_specs=pl.BlockSpec((None, T, dh), lambda b, h: (b, 0, h)),
        out_shape=jax.ShapeDtypeStruct((B, T, HD), BF16),
        compiler_params=_params(("arbitrary", "arbitrary"), blocks),
        name="stick_breaking_attention",
    )(q, kv, kv)


def kernel(x, c, ada_w, ada_b, ln_g, ln_b, gla_w_in, gla_w_gate_up, gla_b_gate, gla_norm_g, gla_w_out, kv_ada_w, kv_ada_b, w_kv, sb_w_q, sb_w_out, mlp_w1, mlp_b1, mlp_w2, mlp_b2):
    B, T, D = x.shape
    depth = ada_w.shape[0]
    n_a = gla_w_in.shape[0]
    alpha = (2.0 * depth) ** 0.25
    QK = gla_w_gate_up.shape[-1]
    V = gla_w_out.shape[1]
    n_main = 2 * QK + 2 * V
    dh = sb_w_q.shape[-1] // SB_HEADS

    c_pad = jnp.pad(c, ((0, MOD_ROWS - B), (0, 0)))
    mod4 = _mod_vectors(c_pad, ada_w, ada_b).reshape(depth, MOD_ROWS, 1, N_MOD * D)
    kv_mod4 = _mod_vectors(c_pad, kv_ada_w[None], kv_ada_b[None]).reshape(1, MOD_ROWS, 1, 2 * D)

    kv = None
    for l in range(depth):
        if l < n_a:
            w_in = gla_w_in[l]
            proj = _mod_matmul(x, mod4, l, 0, 1, w_in[:, :n_main].astype(BF16), F32)
            pad = LANES - GLA_GATE_RANK
            w_gate = jnp.pad(w_in[:, n_main:], ((0, 0), (0, pad))).astype(BF16)
            w_up = jnp.pad(gla_w_gate_up[l], ((0, pad), (0, 0))).astype(BF16)
            log_a = _gla_log_decay(x, mod4, l, w_gate, w_up, gla_b_gate[l])
            mixed = _gla_recurrence(proj, log_a, gla_norm_g[l])
            w_out = gla_w_out[l]
        else:
            if kv is None:
                kv = _mod_matmul(x, kv_mod4, 0, 0, 1, w_kv.astype(BF16), BF16)
            j = l - n_a
            q = _mod_matmul(x, mod4, l, 0, 1, sb_w_q[j].astype(BF16), BF16, out_scale=dh ** -0.5 * LOG2_E)
            mixed = _sb_attention(q, kv)
            w_out = sb_w_out[j]
        x = _proj_residual_ln(mixed, w_out.astype(BF16), x, mod4, l, 2, ln_g[l, 0], ln_b[l, 0], alpha)
        x = _mlp_block(x, mod4, l, mlp_w1[l].astype(BF16), mlp_b1[l], mlp_w2[l].astype(BF16), mlp_b2[l],
                       ln_g[l, 1], ln_b[l, 1], alpha)
    return x
```

```python
import functools

import jax
import jax.numpy as jnp
from jax import lax
from jax.experimental import pallas as pl
from jax.experimental.pallas import tpu as pltpu

F32 = jnp.float32
BF16 = jnp.bfloat16

GLA_HEADS = 4
GLA_GATE_RANK = 16
GLA_GATE_TAU = 16.0
GLA_CHUNK = 64
SB_HEADS = 16
N_MOD = 6
LN_EPS = 1e-5
RMS_EPS = 1e-5

LANES = 128
MOD_ROWS = 8
VMEM_LIMIT_CAP = 58 << 20
VMEM_HEADROOM = 6 << 20


def _params(semantics, block_bytes):
    limit = min(int(block_bytes) + VMEM_HEADROOM, VMEM_LIMIT_CAP)
    return pltpu.CompilerParams(dimension_semantics=semantics, vmem_limit_bytes=limit)


def _tile(n, want):
    t = min(n, want)
    assert n % t == 0, (n, want)
    return t


def _dot(a, b):
    return jnp.dot(a, b, preferred_element_type=F32)


def _dot_nt(a, b):
    return lax.dot_general(a, b, (((1,), (1,)), ((), ())), preferred_element_type=F32)


def _dot_tn(a, b):
    return lax.dot_general(a, b, (((0,), (0,)), ((), ())), preferred_element_type=F32)


def _split_bf16(x):
    hi = x.astype(BF16)
    lo = (x - hi.astype(F32)).astype(BF16)
    return hi, lo


def _softplus(z):
    return jnp.maximum(z, 0.0) + jnp.log(1.0 + jnp.exp(-jnp.abs(z)))


def _layer_norm(z, g, b):
    mu = jnp.mean(z, axis=-1, keepdims=True)
    zc = z - mu
    var = jnp.mean(zc * zc, axis=-1, keepdims=True)
    return zc * lax.rsqrt(var + LN_EPS) * g + b


def _mod_kernel(c_ref, w_ref, b_ref, o_ref):
    c = c_ref[...]
    cs = c * (1.0 / (1.0 + jnp.exp(-c)))
    hi, lo = _split_bf16(cs)
    y = _dot(jnp.concatenate([hi, lo], axis=0), w_ref[...].astype(BF16))
    o_ref[...] = y[:MOD_ROWS] + y[MOD_ROWS:] + b_ref[...]


def _mod_vectors(c_pad, w, b):
    L, D, N = w.shape
    tn = _tile(N, 1024)
    blocks = 2 * (D * tn * 4) + MOD_ROWS * D * 4 + 4 * MOD_ROWS * tn * 4
    return pl.pallas_call(
        _mod_kernel,
        grid=(L, N // tn),
        in_specs=[
            pl.BlockSpec((MOD_ROWS, D), lambda l, j: (0, 0)),
            pl.BlockSpec((None, D, tn), lambda l, j: (l, 0, j)),
            pl.BlockSpec((None, 1, tn), lambda l, j: (l, 0, j)),
        ],
        out_specs=pl.BlockSpec((None, MOD_ROWS, tn), lambda l, j: (l, 0, j)),
        out_shape=jax.ShapeDtypeStruct((L, MOD_ROWS, N), F32),
        compiler_params=_params(("arbitrary", "arbitrary"), blocks),
        name="adaln_vectors",
    )(c_pad, w, b.reshape(L, 1, N))


def _mod_spec(layer, slot, D, n_grid):
    if n_grid == 2:
        return pl.BlockSpec((None, None, 1, D), lambda b, i: (layer, b, 0, slot))
    return pl.BlockSpec((None, None, 1, D), lambda b, i, j: (layer, b, 0, slot))


def _modmm_kernel(x_ref, sh_ref, sc_ref, w_ref, o_ref, h_ref, *, out_scale):
    @pl.when(pl.program_id(2) == 0)
    def _():
        h_ref[...] = (x_ref[...] * (1.0 + sc_ref[...]) + sh_ref[...]).astype(BF16)

    y = _dot(h_ref[...], w_ref[...])
    if out_scale != 1.0:
        y = y * out_scale
    o_ref[...] = y.astype(o_ref.dtype)


def _mod_matmul(x, mod4, layer, sh_slot, sc_slot, w, out_dtype, out_scale=1.0):
    B, T, D = x.shape
    N = w.shape[1]
    tm, tn = _tile(T, 1024), _tile(N, 1024)
    osz = jnp.dtype(out_dtype).itemsize
    blocks = 2 * tm * D * 4 + 2 * D * tn * 2 + 2 * tm * tn * osz + tm * D * 2 + tm * tn * 4
    return pl.pallas_call(
        functools.partial(_modmm_kernel, out_scale=out_scale),
        grid=(B, T // tm, N // tn),
        in_specs=[
            pl.BlockSpec((None, tm, D), lambda b, i, j: (b, i, 0)),
            _mod_spec(layer, sh_slot, D, 3),
            _mod_spec(layer, sc_slot, D, 3),
            pl.BlockSpec((D, tn), lambda b, i, j: (0, j)),
        ],
        out_specs=pl.BlockSpec((None, tm, tn), lambda b, i, j: (b, i, j)),
        out_shape=jax.ShapeDtypeStruct((B, T, N), out_dtype),
        scratch_shapes=[pltpu.VMEM((tm, D), BF16)],
        compiler_params=_params(("arbitrary", "arbitrary", "arbitrary"), blocks),
        name="modulated_matmul",
    )(x, mod4, mod4, w)


def _gla_gate_kernel(x_ref, sh_ref, sc_ref, wg_ref, wup_ref, bg_ref, tri_ref, o_ref):
    h = (x_ref[...] * (1.0 + sc_ref[...]) + sh_ref[...]).astype(BF16)
    gl = _dot(h, wg_ref[...])
    g = _dot(gl.astype(BF16), wup_ref[...]) + bg_ref[...]
    log_a = (jnp.minimum(g, 0.0) - jnp.log(1.0 + jnp.exp(-jnp.abs(g)))) * (1.0 / GLA_GATE_TAU)
    hi, lo = _split_bf16(log_a)
    o_ref[...] = _dot(tri_ref[...], jnp.concatenate([hi, lo], axis=0))


def _chunk_cumsum_matrix(tm):
    row = lax.broadcasted_iota(jnp.int32, (tm, tm), 0)
    col = lax.broadcasted_iota(jnp.int32, (tm, tm), 1)
    tri = ((col <= row) & (row // GLA_CHUNK == col // GLA_CHUNK)).astype(BF16)
    return jnp.concatenate([tri, tri], axis=1)


def _gla_cum_log_decay(x, mod4, layer, w_gate, w_up, b_gate):
    B, T, D = x.shape
    R, QK = w_up.shape
    tm = _tile(T, 512)
    assert tm % GLA_CHUNK == 0
    blocks = (2 * tm * D * 4 + tm * D * 2 + 2 * D * R * 2 + 2 * R * QK * 2 + 2 * tm * 2 * tm * 2
              + 6 * tm * QK * 4)
    return pl.pallas_call(
        _gla_gate_kernel,
        grid=(B, T // tm),
        in_specs=[
            pl.BlockSpec((None, tm, D), lambda b, i: (b, i, 0)),
            _mod_spec(layer, 0, D, 2),
            _mod_spec(layer, 1, D, 2),
            pl.BlockSpec((D, R), lambda b, i: (0, 0)),
            pl.BlockSpec((R, QK), lambda b, i: (0, 0)),
            pl.BlockSpec((1, QK), lambda b, i: (0, 0)),
            pl.BlockSpec((tm, 2 * tm), lambda b, i: (0, 0)),
        ],
        out_specs=pl.BlockSpec((None, tm, QK), lambda b, i: (b, i, 0)),
        out_shape=jax.ShapeDtypeStruct((B, T, QK), F32),
        compiler_params=_params(("arbitrary", "arbitrary"), blocks),
        name="gla_cum_log_decay",
    )(x, mod4, mod4, w_gate, w_up, b_gate.reshape(1, QK), _chunk_cumsum_matrix(tm))


GLA_HEADS_PER_STEP = 2


def _gla_rec_kernel(q_ref, k_ref, v_ref, r_ref, bc_ref, ng_ref, o_ref, st_ref, *, n_chunks, dk, dv):
    C = GLA_CHUNK

    @pl.when(pl.program_id(2) == 0)
    def _():
        st_ref[...] = jnp.zeros_like(st_ref)

    crow = lax.broadcasted_iota(jnp.int32, (C, C), 0)
    ccol = lax.broadcasted_iota(jnp.int32, (C, C), 1)
    causal = ccol <= crow
    ng = ng_ref[...]

    bc, k, q_dec, k_intra = [], [], [], []
    for h in range(GLA_HEADS_PER_STEP):
        ks = slice(h * dk, (h + 1) * dk)
        bc.append(bc_ref[:, ks])
        k.append(k_ref[:, ks])
        q_dec.append((q_ref[:, ks] * (dk ** -0.5) * jnp.exp(bc[h])).astype(BF16))
        k_intra.append((k[h] * jnp.exp(-bc[h])).astype(BF16))

    for c in range(n_chunks):
        sl = slice(c * C, (c + 1) * C)
        for h in range(GLA_HEADS_PER_STEP):
            vs = slice(h * dv, (h + 1) * dv)
            b_last = bc[h][(c + 1) * C - 1:(c + 1) * C, :]
            k_state = (k[h][sl] * jnp.exp(b_last - bc[h][sl])).astype(BF16)
            decay = jnp.exp(b_last)
            v_c = v_ref[sl, vs].astype(BF16)
            scores = jnp.where(causal, _dot_nt(q_dec[h][sl], k_intra[h][sl]), 0.0)
            st = st_ref[h]
            o = _dot(scores.astype(BF16), v_c) + _dot_nt(q_dec[h][sl], st.astype(BF16))
            st_ref[h] = st * decay + _dot_tn(v_c, k_state)
            o = o * lax.rsqrt(jnp.mean(o * o, axis=-1, keepdims=True) + RMS_EPS) * ng
            r = r_ref[sl, vs]
            o_ref[sl, vs] = (o * (r * (1.0 / (1.0 + jnp.exp(-r))))).astype(o_ref.dtype)


def _gla_recurrence(proj, bc, norm_g):
    B, T, _ = proj.shape
    QK = bc.shape[-1]
    dk = QK // GLA_HEADS
    dv = norm_g.shape[-1]
    V = dv * GLA_HEADS
    hp = GLA_HEADS_PER_STEP
    assert dk % LANES == 0 and dv % LANES == 0 and GLA_HEADS % hp == 0
    wk, wv = hp * dk, hp * dv
    assert QK % wk == 0 and (2 * QK) % wv == 0 and (2 * QK + V) % wv == 0
    n_chunks = min(8, T // GLA_CHUNK)
    tb = n_chunks * GLA_CHUNK
    assert T % tb == 0
    kk = QK // wk
    kv, kr = 2 * QK // wv, (2 * QK + V) // wv
    blocks = (2 * (3 * tb * wk * 4 + 2 * tb * wv * 4 + tb * wv * 2) + hp * dv * dk * 4 + 8 * tb * wk * 4)
    return pl.pallas_call(
        functools.partial(_gla_rec_kernel, n_chunks=n_chunks, dk=dk, dv=dv),
        grid=(B, GLA_HEADS // hp, T // tb),
        in_specs=[
            pl.BlockSpec((None, tb, wk), lambda b, h, t: (b, t, h)),
            pl.BlockSpec((None, tb, wk), lambda b, h, t: (b, t, kk + h)),
            pl.BlockSpec((None, tb, wv), lambda b, h, t: (b, t, kv + h)),
            pl.BlockSpec((None, tb, wv), lambda b, h, t: (b, t, kr + h)),
            pl.BlockSpec((None, tb, wk), lambda b, h, t: (b, t, h)),
            pl.BlockSpec((1, dv), lambda b, h, t: (0, 0)),
        ],
        out_specs=pl.BlockSpec((None, tb, wv), lambda b, h, t: (b, t, h)),
        out_shape=jax.ShapeDtypeStruct((B, T, V), BF16),
        scratch_shapes=[pltpu.VMEM((hp, dv, dk), F32)],
        compiler_params=_params(("arbitrary", "arbitrary", "arbitrary"), blocks),
        name="gla_recurrence",
    )(proj, proj, proj, proj, bc, norm_g.reshape(1, dv))


def _row_slices(n_rows, n_split):
    step = n_rows // n_split
    return [slice(i * step, (i + 1) * step) for i in range(n_split)]


def _proj_ln_kernel(a_ref, w_ref, x_ref, g_ref, lg_ref, lb_ref, o_ref, *, alpha):
    for sl in _row_slices(a_ref.shape[0], 4):
        y = _dot(a_ref[sl, :], w_ref[...])
        z = alpha * x_ref[sl, :] + (1.0 + g_ref[...]) * y
        o_ref[sl, :] = _layer_norm(z, lg_ref[...], lb_ref[...])


def _proj_residual_ln(a, w, x, mod4, layer, gate_slot, ln_g, ln_b, alpha):
    B, T, D = x.shape
    K = a.shape[-1]
    tm = _tile(T, 512)
    blocks = 2 * tm * K * 2 + 2 * K * D * 2 + 4 * tm * D * 4 + 3 * tm * D * 4
    return pl.pallas_call(
        functools.partial(_proj_ln_kernel, alpha=alpha),
        grid=(B, T // tm),
        in_specs=[
            pl.BlockSpec((None, tm, K), lambda b, i: (b, i, 0)),
            pl.BlockSpec((K, D), lambda b, i: (0, 0)),
            pl.BlockSpec((None, tm, D), lambda b, i: (b, i, 0)),
            _mod_spec(layer, gate_slot, D, 2),
            pl.BlockSpec((1, D), lambda b, i: (0, 0)),
            pl.BlockSpec((1, D), lambda b, i: (0, 0)),
        ],
        out_specs=pl.BlockSpec((None, tm, D), lambda b, i: (b, i, 0)),
        out_shape=jax.ShapeDtypeStruct((B, T, D), F32),
        compiler_params=_params(("arbitrary", "arbitrary"), blocks),
        name="proj_residual_ln",
    )(a, w, x, mod4, ln_g.reshape(1, D), ln_b.reshape(1, D))


def _mlp_kernel(x_ref, sh_ref, sc_ref, g_ref, w1_ref, b1_ref, w2_ref, b2_ref, lg_ref, lb_ref,
                o_ref, h_ref, acc_ref, *, alpha):
    f = pl.program_id(2)
    last = pl.num_programs(2) - 1
    halves = _row_slices(x_ref.shape[0], 2)

    def contribution(sl):
        u = jnp.maximum(_dot(h_ref[sl, :], w1_ref[...]) + b1_ref[...], 0.0)
        return _dot((u * u).astype(BF16), w2_ref[...])

    @pl.when(f == 0)
    def _():
        for sl in halves:
            h_ref[sl, :] = (x_ref[sl, :] * (1.0 + sc_ref[...]) + sh_ref[...]).astype(BF16)
            acc_ref[sl, :] = contribution(sl)

    @pl.when(jnp.logical_and(f > 0, f < last))
    def _():
        acc_ref[...] += contribution(slice(None))

    @pl.when(f == last)
    def _():
        for sl in halves:
            y = acc_ref[sl, :] + contribution(sl) + b2_ref[...]
            z = alpha * x_ref[sl, :] + (1.0 + g_ref[...]) * y
            o_ref[sl, :] = _layer_norm(z, lg_ref[...], lb_ref[...])


def _mlp_block(x, mod4, layer, w1, b1, w2, b2, ln_g, ln_b, alpha):
    B, T, D = x.shape
    F = w1.shape[1]
    tm, tf = _tile(T, 512), _tile(F, 1024)
    assert F // tf >= 2, "the first and the last d_ff step must be distinct grid steps"
    blocks = (4 * tm * D * 4 + tm * D * 2 + tm * D * 4 + 4 * D * tf * 2 + tm * tf * 6 + 2 * tm * D * 4)
    return pl.pallas_call(
        functools.partial(_mlp_kernel, alpha=alpha),
        grid=(B, T // tm, F // tf),
        in_specs=[
            pl.BlockSpec((None, tm, D), lambda b, i, f: (b, i, 0)),
            _mod_spec(layer, 3, D, 3),
            _mod_spec(layer, 4, D, 3),
            _mod_spec(layer, 5, D, 3),
            pl.BlockSpec((D, tf), lambda b, i, f: (0, f)),
            pl.BlockSpec((1, tf), lambda b, i, f: (0, f)),
            pl.BlockSpec((tf, D), lambda b, i, f: (f, 0)),
            pl.BlockSpec((1, D), lambda b, i, f: (0, 0)),
            pl.BlockSpec((1, D), lambda b, i, f: (0, 0)),
            pl.BlockSpec((1, D), lambda b, i, f: (0, 0)),
        ],
        out_specs=pl.BlockSpec((None, tm, D), lambda b, i, f: (b, i, 0)),
        out_shape=jax.ShapeDtypeStruct((B, T, D), F32),
        scratch_shapes=[pltpu.VMEM((tm, D), BF16), pltpu.VMEM((tm, D), F32)],
        compiler_params=_params(("arbitrary", "arbitrary", "arbitrary"), blocks),
        name="mlp_block",
    )(x, mod4, mod4, mod4, w1, b1.reshape(1, F), w2, b2.reshape(1, D), ln_g.reshape(1, D), ln_b.reshape(1, D))


SB_BLK = 256
LOG2_E = 1.4426950408889634
LN_2 = 0.6931471805599453
SB_DEAD_LOG2 = 160.0
SB_NO_KEYS = 1e30


def _sb_attn_kernel(q_ref, k_ref, v_ref, o_ref, *, n_pairs):
    n = SB_BLK
    dh = q_ref.shape[-1]
    jrow = lax.broadcasted_iota(jnp.int32, (n, n), 0)
    jcol = lax.broadcasted_iota(jnp.int32, (n, n), 1)
    later = (jrow > jcol).astype(BF16)
    diag_mask = jcol < jrow

    def scores(q, blk):
        return _dot_nt(q, k_ref[pl.ds(pl.multiple_of(blk * n, n), n), :])

    later2 = jnp.concatenate([later, later], axis=0)

    def log_terms(z, mask):
        neg_abs = lax.bitcast_convert_type(
            lax.bitcast_convert_type(z, jnp.uint32) | jnp.uint32(0x80000000), F32)
        sp = jnp.maximum(z, 0.0) + jnp.log(1.0 + jnp.exp2(neg_abs)) * LOG2_E
        lb = z - sp
        if mask is not None:
            sp = jnp.where(mask, sp, 0.0)
        hi, lo = _split_bf16(sp)
        return jnp.concatenate([hi, lo], axis=1), lb, sp[:, 0:1]

    def suffix(hilo):
        return _dot(hilo, later2)

    def weights(lb, sfx, r, sp0, mask):
        a = jnp.exp2(lb - sfx if r is None else lb - (sfx + r))
        if mask is not None:
            a = jnp.where(mask, a, 0.0)
        tot = sfx[:, 0:1] + sp0
        return a.astype(BF16), tot if r is None else r + tot

    def values(a, blk):
        return _dot(a, v_ref[pl.ds(pl.multiple_of(blk * n, n), n), :])

    def step(qa, qb, blk_a, blk_b, ra, rb, acc_a, acc_b, mask):
        za = scores(qa, blk_a)
        zb = scores(qb, blk_b)
        hla, lba, spa0 = log_terms(za, mask)
        sfa = suffix(hla)
        hlb, lbb, spb0 = log_terms(zb, mask)
        aa, ra = weights(lba, sfa, ra, spa0, mask)
        sfb = suffix(hlb)
        acc_a = acc_a + values(aa, blk_a)
        ab, rb = weights(lbb, sfb, rb, spb0, mask)
        acc_b = acc_b + values(ab, blk_b)
        return ra, rb, acc_a, acc_b

    def pair_body(p, _):
        blk_a0 = 2 * p
        blk_b0 = blk_a0 + 1
        row_a = pl.multiple_of(blk_a0 * n, n)
        row_b = pl.multiple_of(blk_b0 * n, n)
        qa = q_ref[pl.ds(row_a, n), :]
        qb = q_ref[pl.ds(row_b, n), :]

        blk_a1 = jnp.maximum(blk_a0 - 1, 0)
        blk_b1 = blk_a0
        za0 = scores(qa, blk_a0)
        zb0 = scores(qb, blk_b0)
        za1 = scores(qa, blk_a1)
        zb1 = scores(qb, blk_b1)
        hla0, lba0, spa0 = log_terms(za0, diag_mask)
        sfa0 = suffix(hla0)
        hlb0, lbb0, spb0 = log_terms(zb0, diag_mask)
        sfb0 = suffix(hlb0)
        hla1, lba1, spa1 = log_terms(za1, None)
        aa0, ra = weights(lba0, sfa0, None, spa0, diag_mask)
        sfa1 = suffix(hla1)
        acc_a = values(aa0, blk_a0)
        hlb1, lbb1, spb1 = log_terms(zb1, None)
        ab0, rb = weights(lbb0, sfb0, None, spb0, diag_mask)
        sfb1 = suffix(hlb1)
        acc_b = values(ab0, blk_b0)
        ra = jnp.where(p > 0, ra, SB_NO_KEYS)
        aa1, ra = weights(lba1, sfa1, ra, spa1, None)
        acc_a = acc_a + values(aa1, blk_a1)
        ab1, rb = weights(lbb1, sfb1, rb, spb1, None)
        acc_b = acc_b + values(ab1, blk_b1)

        def alive(ra, rb):
            return jnp.minimum(jnp.min(ra), jnp.min(rb)) < SB_DEAD_LOG2

        def cond(carry):
            i, go = carry[0], carry[1]
            return jnp.logical_and(i <= blk_b0, go)

        def body(carry):
            i, _, ra, rb, acc_a, acc_b = carry
            has_a = i <= blk_a0
            ra = jnp.where(has_a, ra, SB_NO_KEYS)
            ra, rb, acc_a, acc_b = step(qa, qb, jnp.maximum(blk_a0 - i, 0), blk_b0 - i,
                                        ra, rb, acc_a, acc_b, None)
            return i + 1, alive(ra, rb), ra, rb, acc_a, acc_b

        carry = lax.while_loop(cond, body, (jnp.int32(2), alive(ra, rb), ra, rb, acc_a, acc_b))
        o_ref[pl.ds(row_a, n), :] = carry[4].astype(o_ref.dtype)
        o_ref[pl.ds(row_b, n), :] = carry[5].astype(o_ref.dtype)
        return 0

    lax.fori_loop(0, n_pairs, pair_body, 0)


def _sb_attention(q, kv):
    B, T, HD = q.shape
    dh = HD // SB_HEADS
    assert dh % LANES == 0 and T % (2 * SB_BLK) == 0
    blocks = 8 * T * dh * 2 + 24 * SB_BLK * SB_BLK * 4
    return pl.pallas_call(
        functools.partial(_sb_attn_kernel, n_pairs=T // (2 * SB_BLK)),
        grid=(B, SB_HEADS),
        in_specs=[
            pl.BlockSpec((None, T, dh), lambda b, h: (b, 0, h)),
            pl.BlockSpec((None, T, dh), lambda b, h: (b, 0, h)),
            pl.BlockSpec((None, T, dh), lambda b, h: (b, 0, SB_HEADS + h)),
        ],
        out_specs=pl.BlockSpec((None, T, dh), lambda b, h: (b, 0, h)),
        out_shape=jax.ShapeDtypeStruct((B, T, HD), BF16),
        compiler_params=_params(("arbitrary", "arbitrary"), blocks),
        name="stick_breaking_attention",
    )(q, kv, kv)


def kernel(x, c, ada_w, ada_b, ln_g, ln_b, gla_w_in, gla_w_gate_up, gla_b_gate, gla_norm_g, gla_w_out, kv_ada_w, kv_ada_b, w_kv, sb_w_q, sb_w_out, mlp_w1, mlp_b1, mlp_w2, mlp_b2):
    B, T, D = x.shape
    depth = ada_w.shape[0]
    n_a = gla_w_in.shape[0]
    alpha = (2.0 * depth) ** 0.25
    QK = gla_w_gate_up.shape[-1]
    V = gla_w_out.shape[1]
    n_main = 2 * QK + 2 * V
    dh = sb_w_q.shape[-1] // SB_HEADS

    c_pad = jnp.pad(c, ((0, MOD_ROWS - B), (0, 0)))
    mod4 = _mod_vectors(c_pad, ada_w, ada_b).reshape(depth, MOD_ROWS, 1, N_MOD * D)
    kv_mod4 = _mod_vectors(c_pad, kv_ada_w[None], kv_ada_b[None]).reshape(1, MOD_ROWS, 1, 2 * D)

    kv = None
    for l in range(depth):
        if l < n_a:
            w_in = gla_w_in[l]
            proj = _mod_matmul(x, mod4, l, 0, 1, w_in[:, :n_main].astype(BF16), F32)
            pad = LANES - GLA_GATE_RANK
            w_gate = jnp.pad(w_in[:, n_main:], ((0, 0), (0, pad))).astype(BF16)
            w_up = jnp.pad(gla_w_gate_up[l], ((0, pad), (0, 0))).astype(BF16)
            bc = _gla_cum_log_decay(x, mod4, l, w_gate, w_up, gla_b_gate[l])
            mixed = _gla_recurrence(proj, bc, gla_norm_g[l])
            w_out = gla_w_out[l]
        else:
            if kv is None:
                kv = _mod_matmul(x, kv_mod4, 0, 0, 1, w_kv.astype(BF16), BF16)
            j = l - n_a
            q = _mod_matmul(x, mod4, l, 0, 1, sb_w_q[j].astype(BF16), BF16, out_scale=dh ** -0.5 * LOG2_E)
            mixed = _sb_attention(q, kv)
            w_out = sb_w_out[j]
        x = _proj_residual_ln(mixed, w_out.astype(BF16), x, mod4, l, 2, ln_g[l, 0], ln_b[l, 0], alpha)
        x = _mlp_block(x, mod4, l, mlp_w1[l].astype(BF16), mlp_b1[l], mlp_w2[l].astype(BF16), mlp_b2[l],
                       ln_g[l, 1], ln_b[l, 1], alpha)
    return x
```

```python
import functools

import jax
import jax.numpy as jnp
from jax import lax
from jax.experimental import pallas as pl
from jax.experimental.pallas import tpu as pltpu

F32 = jnp.float32
BF16 = jnp.bfloat16

GLA_HEADS = 4
GLA_GATE_RANK = 16
GLA_GATE_TAU = 16.0
GLA_CHUNK = 64
SB_HEADS = 16
N_MOD = 6
LN_EPS = 1e-5
RMS_EPS = 1e-5

LANES = 128
MOD_ROWS = 8
VMEM_LIMIT_CAP = 58 << 20
VMEM_HEADROOM = 6 << 20


def _params(semantics, block_bytes):
    limit = min(int(block_bytes) + VMEM_HEADROOM, VMEM_LIMIT_CAP)
    return pltpu.CompilerParams(dimension_semantics=semantics, vmem_limit_bytes=limit)


def _tile(n, want):
    t = min(n, want)
    assert n % t == 0, (n, want)
    return t


def _row_slices(n_rows, n_split):
    assert n_split >= 1 and n_rows % n_split == 0
    step = n_rows // n_split
    return [slice(i * step, (i + 1) * step) for i in range(n_split)]


def _dot(a, b):
    return jnp.dot(a, b, preferred_element_type=F32)


def _dot_nt(a, b):
    return lax.dot_general(a, b, (((1,), (1,)), ((), ())), preferred_element_type=F32)


def _dot_tn(a, b):
    return lax.dot_general(a, b, (((0,), (0,)), ((), ())), preferred_element_type=F32)


def _split_bf16(x):
    hi = x.astype(BF16)
    lo = (x - hi.astype(F32)).astype(BF16)
    return hi, lo


def _softplus(z):
    return jnp.maximum(z, 0.0) + jnp.log(1.0 + jnp.exp(-jnp.abs(z)))


def _layer_norm(z, g, b):
    mu = jnp.mean(z, axis=-1, keepdims=True)
    zc = z - mu
    var = jnp.mean(zc * zc, axis=-1, keepdims=True)
    return zc * lax.rsqrt(var + LN_EPS) * g + b


def _mod_kernel(c_ref, w_ref, b_ref, o_ref):
    c = c_ref[...]
    cs = c * (1.0 / (1.0 + jnp.exp(-c)))
    hi, lo = _split_bf16(cs)
    y = _dot(jnp.concatenate([hi, lo], axis=0), w_ref[...].astype(BF16))
    o_ref[...] = y[:MOD_ROWS] + y[MOD_ROWS:] + b_ref[...]


def _mod_vectors(c_pad, w, b):
    L, D, N = w.shape
    tn = _tile(N, 1024)
    blocks = 2 * (D * tn * 4) + MOD_ROWS * D * 4 + 4 * MOD_ROWS * tn * 4
    return pl.pallas_call(
        _mod_kernel,
        grid=(L, N // tn),
        in_specs=[
            pl.BlockSpec((MOD_ROWS, D), lambda l, j: (0, 0)),
            pl.BlockSpec((None, D, tn), lambda l, j: (l, 0, j)),
            pl.BlockSpec((None, 1, tn), lambda l, j: (l, 0, j)),
        ],
        out_specs=pl.BlockSpec((None, MOD_ROWS, tn), lambda l, j: (l, 0, j)),
        out_shape=jax.ShapeDtypeStruct((L, MOD_ROWS, N), F32),
        compiler_params=_params(("arbitrary", "arbitrary"), blocks),
        name="adaln_vectors",
    )(c_pad, w, b.reshape(L, 1, N))


def _mod_spec(layer, slot, D, n_grid):
    if n_grid == 2:
        return pl.BlockSpec((None, None, 1, D), lambda b, i: (layer, b, 0, slot))
    return pl.BlockSpec((None, None, 1, D), lambda b, i, j: (layer, b, 0, slot))


MODMM_ROW_GROUP = 256


def _modmm_kernel(x_ref, sh_ref, sc_ref, w_ref, o_ref, h_ref, *, out_scale):
    groups = _row_slices(x_ref.shape[0], x_ref.shape[0] // MODMM_ROW_GROUP)

    def emit(sl):
        y = _dot(h_ref[sl, :], w_ref[...])
        if out_scale != 1.0:
            y = y * out_scale
        o_ref[sl, :] = y.astype(o_ref.dtype)

    @pl.when(pl.program_id(2) == 0)
    def _():
        for sl in groups:
            h_ref[sl, :] = (x_ref[sl, :] * (1.0 + sc_ref[...]) + sh_ref[...]).astype(BF16)
            emit(sl)

    @pl.when(pl.program_id(2) > 0)
    def _():
        for sl in groups:
            emit(sl)


def _mod_matmul(x, mod4, layer, sh_slot, sc_slot, w, out_dtype, out_scale=1.0):
    B, T, D = x.shape
    N = w.shape[1]
    tm = _tile(T, 1024)
    osz = jnp.dtype(out_dtype).itemsize

    def block_bytes(tn):
        return (2 * tm * D * 4 + 2 * D * tn * 2 + 2 * tm * tn * osz + tm * D * 2
                + 2 * MODMM_ROW_GROUP * tn * 4)

    tn = next(t for t in (2048, 1536, 1024, 512, 256, 128)
              if N % t == 0 and block_bytes(t) + VMEM_HEADROOM <= VMEM_LIMIT_CAP)
    blocks = block_bytes(tn)
    return pl.pallas_call(
        functools.partial(_modmm_kernel, out_scale=out_scale),
        grid=(B, T // tm, N // tn),
        in_specs=[
            pl.BlockSpec((None, tm, D), lambda b, i, j: (b, i, 0)),
            _mod_spec(layer, sh_slot, D, 3),
            _mod_spec(layer, sc_slot, D, 3),
            pl.BlockSpec((D, tn), lambda b, i, j: (0, j)),
        ],
        out_specs=pl.BlockSpec((None, tm, tn), lambda b, i, j: (b, i, j)),
        out_shape=jax.ShapeDtypeStruct((B, T, N), out_dtype),
        scratch_shapes=[pltpu.VMEM((tm, D), BF16)],
        compiler_params=_params(("arbitrary", "arbitrary", "arbitrary"), blocks),
        name="modulated_matmul",
    )(x, mod4, mod4, w)


def _gla_gate_kernel(x_ref, sh_ref, sc_ref, wg_ref, wup_ref, bg_ref, tri_ref, o_ref):
    h = (x_ref[...] * (1.0 + sc_ref[...]) + sh_ref[...]).astype(BF16)
    gl = _dot(h, wg_ref[...])
    g = _dot(gl.astype(BF16), wup_ref[...]) + bg_ref[...]
    log_a = (jnp.minimum(g, 0.0) - jnp.log(1.0 + jnp.exp(-jnp.abs(g)))) * (1.0 / GLA_GATE_TAU)
    hi, lo = _split_bf16(log_a)
    o_ref[...] = _dot(tri_ref[...], jnp.concatenate([hi, lo], axis=0))


def _chunk_cumsum_matrix(tm):
    row = lax.broadcasted_iota(jnp.int32, (tm, tm), 0)
    col = lax.broadcasted_iota(jnp.int32, (tm, tm), 1)
    tri = ((col <= row) & (row // GLA_CHUNK == col // GLA_CHUNK)).astype(BF16)
    return jnp.concatenate([tri, tri], axis=1)


def _gla_cum_log_decay(x, mod4, layer, w_gate, w_up, b_gate):
    B, T, D = x.shape
    R, QK = w_up.shape
    tm = _tile(T, 512)
    assert tm % GLA_CHUNK == 0
    blocks = (2 * tm * D * 4 + tm * D * 2 + 2 * D * R * 2 + 2 * R * QK * 2 + 2 * tm * 2 * tm * 2
              + 6 * tm * QK * 4)
    return pl.pallas_call(
        _gla_gate_kernel,
        grid=(B, T // tm),
        in_specs=[
            pl.BlockSpec((None, tm, D), lambda b, i: (b, i, 0)),
            _mod_spec(layer, 0, D, 2),
            _mod_spec(layer, 1, D, 2),
            pl.BlockSpec((D, R), lambda b, i: (0, 0)),
            pl.BlockSpec((R, QK), lambda b, i: (0, 0)),
            pl.BlockSpec((1, QK), lambda b, i: (0, 0)),
            pl.BlockSpec((tm, 2 * tm), lambda b, i: (0, 0)),
        ],
        out_specs=pl.BlockSpec((None, tm, QK), lambda b, i: (b, i, 0)),
        out_shape=jax.ShapeDtypeStruct((B, T, QK), F32),
        compiler_params=_params(("arbitrary", "arbitrary"), blocks),
        name="gla_cum_log_decay",
    )(x, mod4, mod4, w_gate, w_up, b_gate.reshape(1, QK), _chunk_cumsum_matrix(tm))


GLA_HEADS_PER_STEP = 2


def _gla_rec_kernel(q_ref, k_ref, v_ref, r_ref, bc_ref, ng_ref, o_ref, st_ref, *, n_chunks, dk, dv):
    C = GLA_CHUNK

    @pl.when(pl.program_id(2) == 0)
    def _():
        st_ref[...] = jnp.zeros_like(st_ref)

    crow = lax.broadcasted_iota(jnp.int32, (C, C), 0)
    ccol = lax.broadcasted_iota(jnp.int32, (C, C), 1)
    causal = ccol <= crow
    ng = ng_ref[...]

    bc, k, q_dec, k_intra = [], [], [], []
    for h in range(GLA_HEADS_PER_STEP):
        ks = slice(h * dk, (h + 1) * dk)
        bc.append(bc_ref[:, ks])
        k.append(k_ref[:, ks])
        q_dec.append((q_ref[:, ks] * (dk ** -0.5) * jnp.exp(bc[h])).astype(BF16))
        k_intra.append((k[h] * jnp.exp(-bc[h])).astype(BF16))

    for c in range(n_chunks):
        sl = slice(c * C, (c + 1) * C)
        for h in range(GLA_HEADS_PER_STEP):
            vs = slice(h * dv, (h + 1) * dv)
            b_last = bc[h][(c + 1) * C - 1:(c + 1) * C, :]
            k_state = (k[h][sl] * jnp.exp(b_last - bc[h][sl])).astype(BF16)
            decay = jnp.exp(b_last)
            v_c = v_ref[sl, vs].astype(BF16)
            scores = jnp.where(causal, _dot_nt(q_dec[h][sl], k_intra[h][sl]), 0.0)
            st = st_ref[h]
            o = _dot(scores.astype(BF16), v_c) + _dot_nt(q_dec[h][sl], st.astype(BF16))
            st_ref[h] = st * decay + _dot_tn(v_c, k_state)
            o = o * lax.rsqrt(jnp.mean(o * o, axis=-1, keepdims=True) + RMS_EPS) * ng
            r = r_ref[sl, vs]
            o_ref[sl, vs] = (o * (r * (1.0 / (1.0 + jnp.exp(-r))))).astype(o_ref.dtype)


def _gla_recurrence(proj, bc, norm_g):
    B, T, _ = proj.shape
    QK = bc.shape[-1]
    dk = QK // GLA_HEADS
    dv = norm_g.shape[-1]
    V = dv * GLA_HEADS
    hp = GLA_HEADS_PER_STEP
    assert dk % LANES == 0 and dv % LANES == 0 and GLA_HEADS % hp == 0
    wk, wv = hp * dk, hp * dv
    assert QK % wk == 0 and (2 * QK) % wv == 0 and (2 * QK + V) % wv == 0
    n_chunks = min(8, T // GLA_CHUNK)
    tb = n_chunks * GLA_CHUNK
    assert T % tb == 0
    kk = QK // wk
    kv, kr = 2 * QK // wv, (2 * QK + V) // wv
    blocks = (2 * (3 * tb * wk * 4 + 2 * tb * wv * 4 + tb * wv * 2) + hp * dv * dk * 4 + 8 * tb * wk * 4)
    return pl.pallas_call(
        functools.partial(_gla_rec_kernel, n_chunks=n_chunks, dk=dk, dv=dv),
        grid=(B, GLA_HEADS // hp, T // tb),
        in_specs=[
            pl.BlockSpec((None, tb, wk), lambda b, h, t: (b, t, h)),
            pl.BlockSpec((None, tb, wk), lambda b, h, t: (b, t, kk + h)),
            pl.BlockSpec((None, tb, wv), lambda b, h, t: (b, t, kv + h)),
            pl.BlockSpec((None, tb, wv), lambda b, h, t: (b, t, kr + h)),
            pl.BlockSpec((None, tb, wk), lambda b, h, t: (b, t, h)),
            pl.BlockSpec((1, dv), lambda b, h, t: (0, 0)),
        ],
        out_specs=pl.BlockSpec((None, tb, wv), lambda b, h, t: (b, t, h)),
        out_shape=jax.ShapeDtypeStruct((B, T, V), BF16),
        scratch_shapes=[pltpu.VMEM((hp, dv, dk), F32)],
        compiler_params=_params(("arbitrary", "arbitrary", "arbitrary"), blocks),
        name="gla_recurrence",
    )(proj, proj, proj, proj, bc, norm_g.reshape(1, dv))


def _proj_ln_kernel(a_ref, w_ref, x_ref, g_ref, lg_ref, lb_ref, o_ref, *, alpha):
    for sl in _row_slices(a_ref.shape[0], 4):
        y = _dot(a_ref[sl, :], w_ref[...])
        z = alpha * x_ref[sl, :] + (1.0 + g_ref[...]) * y
        o_ref[sl, :] = _layer_norm(z, lg_ref[...], lb_ref[...])


def _proj_residual_ln(a, w, x, mod4, layer, gate_slot, ln_g, ln_b, alpha):
    B, T, D = x.shape
    K = a.shape[-1]
    tm = _tile(T, 512)
    blocks = 2 * tm * K * 2 + 2 * K * D * 2 + 4 * tm * D * 4 + 3 * tm * D * 4
    return pl.pallas_call(
        functools.partial(_proj_ln_kernel, alpha=alpha),
        grid=(B, T // tm),
        in_specs=[
            pl.BlockSpec((None, tm, K), lambda b, i: (b, i, 0)),
            pl.BlockSpec((K, D), lambda b, i: (0, 0)),
            pl.BlockSpec((None, tm, D), lambda b, i: (b, i, 0)),
            _mod_spec(layer, gate_slot, D, 2),
            pl.BlockSpec((1, D), lambda b, i: (0, 0)),
            pl.BlockSpec((1, D), lambda b, i: (0, 0)),
        ],
        out_specs=pl.BlockSpec((None, tm, D), lambda b, i: (b, i, 0)),
        out_shape=jax.ShapeDtypeStruct((B, T, D), F32),
        compiler_params=_params(("arbitrary", "arbitrary"), blocks),
        name="proj_residual_ln",
    )(a, w, x, mod4, ln_g.reshape(1, D), ln_b.reshape(1, D))


MLP_TM = 1024
MLP_TF = 1024
MLP_ROW_GROUP = 256

def _mlp_kernel(x_ref, sh_ref, sc_ref, g_ref, w1_ref, b1_ref, w2_ref, b2_ref, lg_ref, lb_ref,
                o_ref, h_ref, *, alpha):
    f = pl.program_id(2)
    last = pl.num_programs(2) - 1
    groups = _row_slices(x_ref.shape[0], x_ref.shape[0] // MLP_ROW_GROUP)

    def contribution(sl):
        u = jnp.maximum(_dot(h_ref[sl, :], w1_ref[...]) + b1_ref[...], 0.0)
        return _dot((u * u).astype(BF16), w2_ref[...])

    @pl.when(f == 0)
    def _():
        for sl in groups:
            h_ref[sl, :] = (x_ref[sl, :] * (1.0 + sc_ref[...]) + sh_ref[...]).astype(BF16)
            o_ref[sl, :] = contribution(sl)

    @pl.when(jnp.logical_and(f > 0, f < last))
    def _():
        for sl in groups:
            o_ref[sl, :] += contribution(sl)

    @pl.when(f == last)
    def _():
        for sl in groups:
            y = o_ref[sl, :] + contribution(sl) + b2_ref[...]
            z = alpha * x_ref[sl, :] + (1.0 + g_ref[...]) * y
            o_ref[sl, :] = _layer_norm(z, lg_ref[...], lb_ref[...])


def _mlp_block(x, mod4, layer, w1, b1, w2, b2, ln_g, ln_b, alpha):
    B, T, D = x.shape
    F = w1.shape[1]
    tm, tf = _tile(T, MLP_TM), _tile(F, MLP_TF)
    assert F // tf >= 2, "the first and the last d_ff step must be distinct grid steps"
    assert tm % MLP_ROW_GROUP == 0
    blocks = (4 * tm * D * 4 + tm * D * 2 + 4 * D * tf * 2 + tm * tf * 6 + 3 * MLP_ROW_GROUP * D * 4)
    return pl.pallas_call(
        functools.partial(_mlp_kernel, alpha=alpha),
        grid=(B, T // tm, F // tf),
        in_specs=[
            pl.BlockSpec((None, tm, D), lambda b, i, f: (b, i, 0)),
            _mod_spec(layer, 3, D, 3),
            _mod_spec(layer, 4, D, 3),
            _mod_spec(layer, 5, D, 3),
            pl.BlockSpec((D, tf), lambda b, i, f: (0, f)),
            pl.BlockSpec((1, tf), lambda b, i, f: (0, f)),
            pl.BlockSpec((tf, D), lambda b, i, f: (f, 0)),
            pl.BlockSpec((1, D), lambda b, i, f: (0, 0)),
            pl.BlockSpec((1, D), lambda b, i, f: (0, 0)),
            pl.BlockSpec((1, D), lambda b, i, f: (0, 0)),
        ],
        out_specs=pl.BlockSpec((None, tm, D), lambda b, i, f: (b, i, 0)),
        out_shape=jax.ShapeDtypeStruct((B, T, D), F32),
        scratch_shapes=[pltpu.VMEM((tm, D), BF16)],
        compiler_params=_params(("arbitrary", "arbitrary", "arbitrary"), blocks),
        name="mlp_block",
    )(x, mod4, mod4, mod4, w1, b1.reshape(1, F), w2, b2.reshape(1, D), ln_g.reshape(1, D), ln_b.reshape(1, D))


SB_BLK = 256
LOG2_E = 1.4426950408889634
LN_2 = 0.6931471805599453
SB_DEAD_LOG2 = 160.0
SB_NO_KEYS = 1e30


def _sb_attn_kernel(q_ref, k_ref, v_ref, o_ref, *, n_pairs):
    n = SB_BLK
    dh = q_ref.shape[-1]
    jrow = lax.broadcasted_iota(jnp.int32, (n, n), 0)
    jcol = lax.broadcasted_iota(jnp.int32, (n, n), 1)
    later = (jrow > jcol).astype(BF16)
    diag_mask = jcol < jrow

    def scores(q, blk):
        return _dot_nt(q, k_ref[pl.ds(pl.multiple_of(blk * n, n), n), :])

    def log_terms(z, mask):
        neg_abs = lax.bitcast_convert_type(
            lax.bitcast_convert_type(z, jnp.uint32) | jnp.uint32(0x80000000), F32)
        sp = jnp.maximum(z, 0.0) + jnp.log(1.0 + jnp.exp2(neg_abs)) * LOG2_E
        lb = z - sp
        if mask is not None:
            sp = jnp.where(mask, sp, 0.0)
        return sp.astype(BF16), lb, sp[:, 0:1]

    def suffix(hilo):
        return _dot(hilo, later)

    def weights(lb, sfx, r, sp0, mask):
        a = jnp.exp2(lb - sfx if r is None else lb - (sfx + r))
        if mask is not None:
            a = jnp.where(mask, a, 0.0)
        tot = sfx[:, 0:1] + sp0
        return a.astype(BF16), tot if r is None else r + tot

    def values(a, blk):
        return _dot(a, v_ref[pl.ds(pl.multiple_of(blk * n, n), n), :])

    def step(qa, qb, blk_a, blk_b, ra, rb, acc_a, acc_b, mask):
        za = scores(qa, blk_a)
        zb = scores(qb, blk_b)
        hla, lba, spa0 = log_terms(za, mask)
        sfa = suffix(hla)
        hlb, lbb, spb0 = log_terms(zb, mask)
        aa, ra = weights(lba, sfa, ra, spa0, mask)
        sfb = suffix(hlb)
        acc_a = acc_a + values(aa, blk_a)
        ab, rb = weights(lbb, sfb, rb, spb0, mask)
        acc_b = acc_b + values(ab, blk_b)
        return ra, rb, acc_a, acc_b

    def pair_body(p, _):
        blk_a0 = 2 * p
        blk_b0 = blk_a0 + 1
        row_a = pl.multiple_of(blk_a0 * n, n)
        row_b = pl.multiple_of(blk_b0 * n, n)
        qa = q_ref[pl.ds(row_a, n), :]
        qb = q_ref[pl.ds(row_b, n), :]

        blk_a1 = jnp.maximum(blk_a0 - 1, 0)
        blk_b1 = blk_a0
        za0 = scores(qa, blk_a0)
        zb0 = scores(qb, blk_b0)
        za1 = scores(qa, blk_a1)
        zb1 = scores(qb, blk_b1)
        hla0, lba0, spa0 = log_terms(za0, diag_mask)
        sfa0 = suffix(hla0)
        hlb0, lbb0, spb0 = log_terms(zb0, diag_mask)
        sfb0 = suffix(hlb0)
        hla1, lba1, spa1 = log_terms(za1, None)
        aa0, ra = weights(lba0, sfa0, None, spa0, diag_mask)
        sfa1 = suffix(hla1)
        acc_a = values(aa0, blk_a0)
        hlb1, lbb1, spb1 = log_terms(zb1, None)
        ab0, rb = weights(lbb0, sfb0, None, spb0, diag_mask)
        sfb1 = suffix(hlb1)
        acc_b = values(ab0, blk_b0)
        ra = jnp.where(p > 0, ra, SB_NO_KEYS)
        aa1, ra = weights(lba1, sfa1, ra, spa1, None)
        acc_a = acc_a + values(aa1, blk_a1)
        ab1, rb = weights(lbb1, sfb1, rb, spb1, None)
        acc_b = acc_b + values(ab1, blk_b1)

        def alive(ra, rb):
            return jnp.minimum(jnp.min(ra), jnp.min(rb)) < SB_DEAD_LOG2

        def cond(carry):
            i, go = carry[0], carry[1]
            return jnp.logical_and(i <= blk_b0, go)

        def body(carry):
            i, _, ra, rb, acc_a, acc_b = carry
            has_a = i <= blk_a0
            ra = jnp.where(has_a, ra, SB_NO_KEYS)
            ra, rb, acc_a, acc_b = step(qa, qb, jnp.maximum(blk_a0 - i, 0), blk_b0 - i,
                                        ra, rb, acc_a, acc_b, None)
            return i + 1, alive(ra, rb), ra, rb, acc_a, acc_b

        carry = lax.while_loop(cond, body, (jnp.int32(2), alive(ra, rb), ra, rb, acc_a, acc_b))
        o_ref[pl.ds(row_a, n), :] = carry[4].astype(o_ref.dtype)
        o_ref[pl.ds(row_b, n), :] = carry[5].astype(o_ref.dtype)
        return 0

    lax.fori_loop(0, n_pairs, pair_body, 0)


def _sb_attention(q, kv):
    B, T, HD = q.shape
    dh = HD // SB_HEADS
    assert dh % LANES == 0 and T % (2 * SB_BLK) == 0
    blocks = 8 * T * dh * 2 + 24 * SB_BLK * SB_BLK * 4
    return pl.pallas_call(
        functools.partial(_sb_attn_kernel, n_pairs=T // (2 * SB_BLK)),
        grid=(B, SB_HEADS),
        in_specs=[
            pl.BlockSpec((None, T, dh), lambda b, h: (b, 0, h)),
            pl.BlockSpec((None, T, dh), lambda b, h: (b, 0, h)),
            pl.BlockSpec((None, T, dh), lambda b, h: (b, 0, SB_HEADS + h)),
        ],
        out_specs=pl.BlockSpec((None, T, dh), lambda b, h: (b, 0, h)),
        out_shape=jax.ShapeDtypeStruct((B, T, HD), BF16),
        compiler_params=_params(("arbitrary", "arbitrary"), blocks),
        name="stick_breaking_attention",
    )(q, kv, kv)


def kernel(x, c, ada_w, ada_b, ln_g, ln_b, gla_w_in, gla_w_gate_up, gla_b_gate, gla_norm_g, gla_w_out, kv_ada_w, kv_ada_b, w_kv, sb_w_q, sb_w_out, mlp_w1, mlp_b1, mlp_w2, mlp_b2):
    B, T, D = x.shape
    depth = ada_w.shape[0]
    n_a = gla_w_in.shape[0]
    alpha = (2.0 * depth) ** 0.25
    QK = gla_w_gate_up.shape[-1]
    V = gla_w_out.shape[1]
    n_main = 2 * QK + 2 * V
    dh = sb_w_q.shape[-1] // SB_HEADS

    c_pad = jnp.pad(c, ((0, MOD_ROWS - B), (0, 0)))
    mod4 = _mod_vectors(c_pad, ada_w, ada_b).reshape(depth, MOD_ROWS, 1, N_MOD * D)
    kv_mod4 = _mod_vectors(c_pad, kv_ada_w[None], kv_ada_b[None]).reshape(1, MOD_ROWS, 1, 2 * D)

    kv = None
    for l in range(depth):
        if l < n_a:
            w_in = gla_w_in[l]
            proj = _mod_matmul(x, mod4, l, 0, 1, w_in[:, :n_main].astype(BF16), F32)
            pad = LANES - GLA_GATE_RANK
            w_gate = jnp.pad(w_in[:, n_main:], ((0, 0), (0, pad))).astype(BF16)
            w_up = jnp.pad(gla_w_gate_up[l], ((0, pad), (0, 0))).astype(BF16)
            bc = _gla_cum_log_decay(x, mod4, l, w_gate, w_up, gla_b_gate[l])
            mixed = _gla_recurrence(proj, bc, gla_norm_g[l])
            w_out = gla_w_out[l]
        else:
            if kv is None:
                kv = _mod_matmul(x, kv_mod4, 0, 0, 1, w_kv.astype(BF16), BF16)
            j = l - n_a
            q = _mod_matmul(x, mod4, l, 0, 1, sb_w_q[j].astype(BF16), BF16, out_scale=dh ** -0.5 * LOG2_E)
            mixed = _sb_attention(q, kv)
            w_out = sb_w_out[j]
        x = _proj_residual_ln(mixed, w_out.astype(BF16), x, mod4, l, 2, ln_g[l, 0], ln_b[l, 0], alpha)
        x = _mlp_block(x, mod4, l, mlp_w1[l].astype(BF16), mlp_b1[l], mlp_w2[l].astype(BF16), mlp_b2[l],
                       ln_g[l, 1], ln_b[l, 1], alpha)
    return x
```

```python
import functools

import jax
import jax.numpy as jnp
from jax import lax
from jax.experimental import pallas as pl
from jax.experimental.pallas import tpu as pltpu

F32 = jnp.float32
BF16 = jnp.bfloat16

GLA_HEADS = 4
GLA_GATE_RANK = 16
GLA_GATE_TAU = 16.0
GLA_CHUNK = 64
SB_HEADS = 16
N_MOD = 6
LN_EPS = 1e-5
RMS_EPS = 1e-5

LANES = 128
MOD_ROWS = 8
VMEM_LIMIT_CAP = 58 << 20
VMEM_HEADROOM = 6 << 20


def _params(semantics, block_bytes):
    limit = min(int(block_bytes) + VMEM_HEADROOM, VMEM_LIMIT_CAP)
    return pltpu.CompilerParams(dimension_semantics=semantics, vmem_limit_bytes=limit)


def _tile(n, want):
    t = min(n, want)
    assert n % t == 0, (n, want)
    return t


def _row_slices(n_rows, n_split):
    assert n_split >= 1 and n_rows % n_split == 0
    step = n_rows // n_split
    return [slice(i * step, (i + 1) * step) for i in range(n_split)]


def _dot(a, b):
    return jnp.dot(a, b, preferred_element_type=F32)


def _dot_nt(a, b):
    return lax.dot_general(a, b, (((1,), (1,)), ((), ())), preferred_element_type=F32)


def _dot_tn(a, b):
    return lax.dot_general(a, b, (((0,), (0,)), ((), ())), preferred_element_type=F32)


def _split_bf16(x):
    hi = x.astype(BF16)
    lo = (x - hi.astype(F32)).astype(BF16)
    return hi, lo


BF16_SUBLANES = 16


def _cast_specs(weights, n_steps, step_index):
    in_specs, out_specs, out_shapes, nbytes = [], [], [], 0
    for w in weights:
        R, C = w.shape
        rows = R // n_steps
        assert R % n_steps == 0 and rows % BF16_SUBLANES == 0, (w.shape, n_steps)
        index_map = lambda *g: (step_index(*g), 0)
        in_specs.append(pl.BlockSpec((rows, C), index_map))
        out_specs.append(pl.BlockSpec((rows, C), index_map))
        out_shapes.append(jax.ShapeDtypeStruct((R, C), BF16))
        nbytes += 2 * rows * C * (4 + 2)
    return in_specs, out_specs, out_shapes, nbytes


def _cast_slabs(src_refs, dst_refs):
    for src, dst in zip(src_refs, dst_refs):
        dst[...] = src[...].astype(dst.dtype)


def _layer_norm(z, g, b):
    mu = jnp.mean(z, axis=-1, keepdims=True)
    zc = z - mu
    var = jnp.mean(zc * zc, axis=-1, keepdims=True)
    return zc * lax.rsqrt(var + LN_EPS) * g + b


def _mod_kernel(c_ref, w_ref, b_ref, o_ref):
    c = c_ref[...]
    cs = c * (1.0 / (1.0 + jnp.exp(-c)))
    hi, lo = _split_bf16(cs)
    y = _dot(jnp.concatenate([hi, lo], axis=0), w_ref[...].astype(BF16))
    o_ref[...] = y[:MOD_ROWS] + y[MOD_ROWS:] + b_ref[...]


def _mod_vectors(c_pad, w, b):
    L, D, N = w.shape
    tn = _tile(N, 1024)
    blocks = 2 * (D * tn * 4) + MOD_ROWS * D * 4 + 4 * MOD_ROWS * tn * 4
    return pl.pallas_call(
        _mod_kernel,
        grid=(L, N // tn),
        in_specs=[
            pl.BlockSpec((MOD_ROWS, D), lambda l, j: (0, 0)),
            pl.BlockSpec((None, D, tn), lambda l, j: (l, 0, j)),
            pl.BlockSpec((None, 1, tn), lambda l, j: (l, 0, j)),
        ],
        out_specs=pl.BlockSpec((None, MOD_ROWS, tn), lambda l, j: (l, 0, j)),
        out_shape=jax.ShapeDtypeStruct((L, MOD_ROWS, N), F32),
        compiler_params=_params(("arbitrary", "arbitrary"), blocks),
        name="adaln_vectors",
    )(c_pad, w, b.reshape(L, 1, N))


def _mod_spec(layer, slot, D, n_grid):
    if n_grid == 2:
        return pl.BlockSpec((None, None, 1, D), lambda b, i: (layer, b, 0, slot))
    return pl.BlockSpec((None, None, 1, D), lambda b, i, j: (layer, b, 0, slot))


MODMM_ROW_GROUP = 256


def _modmm_kernel(x_ref, sh_ref, sc_ref, w_ref, *rest, out_scale, with_side):
    if with_side:
        ws_ref, o_ref, side_ref, h_ref = rest
    else:
        o_ref, h_ref = rest
    groups = _row_slices(x_ref.shape[0], x_ref.shape[0] // MODMM_ROW_GROUP)

    def emit(sl):
        y = _dot(h_ref[sl, :], w_ref[...])
        if out_scale != 1.0:
            y = y * out_scale
        o_ref[sl, :] = y.astype(o_ref.dtype)

    @pl.when(pl.program_id(2) == 0)
    def _():
        for sl in groups:
            h_ref[sl, :] = (x_ref[sl, :] * (1.0 + sc_ref[...]) + sh_ref[...]).astype(BF16)
            emit(sl)
            if with_side:
                side_ref[sl, :] = _dot(h_ref[sl, :], ws_ref[...])

    @pl.when(pl.program_id(2) > 0)
    def _():
        for sl in groups:
            emit(sl)


def _mod_matmul(x, mod4, layer, sh_slot, sc_slot, w, out_dtype, out_scale=1.0, n_cols=None, w_side=None):
    B, T, D = x.shape
    N = w.shape[1] if n_cols is None else n_cols
    tm = _tile(T, 1024)
    osz = jnp.dtype(out_dtype).itemsize
    S = 0 if w_side is None else w_side.shape[1]

    def block_bytes(tn):
        return (2 * tm * D * 4 + 2 * D * tn * 2 + 2 * tm * tn * osz + tm * D * 2
                + 2 * MODMM_ROW_GROUP * tn * 4 + 2 * D * S * 2 + 2 * tm * S * 4)

    tn = next(t for t in (2048, 1536, 1024, 512, 256, 128)
              if N % t == 0 and block_bytes(t) + VMEM_HEADROOM <= VMEM_LIMIT_CAP)
    blocks = block_bytes(tn)
    in_specs = [
        pl.BlockSpec((None, tm, D), lambda b, i, j: (b, i, 0)),
        _mod_spec(layer, sh_slot, D, 3),
        _mod_spec(layer, sc_slot, D, 3),
        pl.BlockSpec((D, tn), lambda b, i, j: (0, j)),
    ]
    out_specs = pl.BlockSpec((None, tm, tn), lambda b, i, j: (b, i, j))
    out_shape = jax.ShapeDtypeStruct((B, T, N), out_dtype)
    args = (x, mod4, mod4, w)
    if w_side is not None:
        in_specs.append(pl.BlockSpec((D, S), lambda b, i, j: (0, 0)))
        out_specs = (out_specs, pl.BlockSpec((None, tm, S), lambda b, i, j: (b, i, 0)))
        out_shape = (out_shape, jax.ShapeDtypeStruct((B, T, S), F32))
        args = args + (w_side,)
    return pl.pallas_call(
        functools.partial(_modmm_kernel, out_scale=out_scale, with_side=w_side is not None),
        grid=(B, T // tm, N // tn),
        in_specs=in_specs,
        out_specs=out_specs,
        out_shape=out_shape,
        scratch_shapes=[pltpu.VMEM((tm, D), BF16)],
        compiler_params=_params(("arbitrary", "arbitrary", "arbitrary"), blocks),
        name="modulated_matmul",
    )(*args)


def _gla_gate_kernel(gl_ref, wup_ref, bg_ref, tri_ref, o_ref):
    g = _dot(gl_ref[...].astype(BF16), wup_ref[...]) + bg_ref[...]
    log_a = (jnp.minimum(g, 0.0) - jnp.log(1.0 + jnp.exp(-jnp.abs(g)))) * (1.0 / GLA_GATE_TAU)
    hi, lo = _split_bf16(log_a)
    o_ref[...] = _dot(tri_ref[...], jnp.concatenate([hi, lo], axis=0))


def _chunk_cumsum_matrix(tm):
    row = lax.broadcasted_iota(jnp.int32, (tm, tm), 0)
    col = lax.broadcasted_iota(jnp.int32, (tm, tm), 1)
    tri = ((col <= row) & (row // GLA_CHUNK == col // GLA_CHUNK)).astype(BF16)
    return jnp.concatenate([tri, tri], axis=1)


def _gla_cum_log_decay(gl, w_up, b_gate):
    B, T, R = gl.shape
    QK = w_up.shape[1]
    tm = _tile(T, 512)
    assert tm % GLA_CHUNK == 0
    blocks = 2 * tm * R * 4 + 2 * R * QK * 2 + 2 * tm * 2 * tm * 2 + 8 * tm * QK * 4
    return pl.pallas_call(
        _gla_gate_kernel,
        grid=(B, T // tm),
        in_specs=[
            pl.BlockSpec((None, tm, R), lambda b, i: (b, i, 0)),
            pl.BlockSpec((R, QK), lambda b, i: (0, 0)),
            pl.BlockSpec((1, QK), lambda b, i: (0, 0)),
            pl.BlockSpec((tm, 2 * tm), lambda b, i: (0, 0)),
        ],
        out_specs=pl.BlockSpec((None, tm, QK), lambda b, i: (b, i, 0)),
        out_shape=jax.ShapeDtypeStruct((B, T, QK), F32),
        compiler_params=_params(("arbitrary", "arbitrary"), blocks),
        name="gla_cum_log_decay",
    )(gl, w_up, b_gate.reshape(1, QK), _chunk_cumsum_matrix(tm))


GLA_HEADS_PER_STEP = 2


def _gla_rec_kernel(q_ref, k_ref, v_ref, r_ref, bc_ref, ng_ref, *rest, n_chunks, dk, dv, n_cast):
    C = GLA_CHUNK
    o_ref, st_ref = rest[n_cast], rest[-1]
    _cast_slabs(rest[:n_cast], rest[n_cast + 1:-1])

    @pl.when(pl.program_id(2) == 0)
    def _():
        st_ref[...] = jnp.zeros_like(st_ref)

    crow = lax.broadcasted_iota(jnp.int32, (C, C), 0)
    ccol = lax.broadcasted_iota(jnp.int32, (C, C), 1)
    causal = ccol <= crow
    ng = ng_ref[...]

    bc, k, q_dec, k_intra = [], [], [], []
    for h in range(GLA_HEADS_PER_STEP):
        ks = slice(h * dk, (h + 1) * dk)
        bc.append(bc_ref[:, ks])
        k.append(k_ref[:, ks])
        q_dec.append((q_ref[:, ks] * (dk ** -0.5) * jnp.exp(bc[h])).astype(BF16))
        k_intra.append((k[h] * jnp.exp(-bc[h])).astype(BF16))

    for c in range(n_chunks):
        sl = slice(c * C, (c + 1) * C)
        for h in range(GLA_HEADS_PER_STEP):
            vs = slice(h * dv, (h + 1) * dv)
            b_last = bc[h][(c + 1) * C - 1:(c + 1) * C, :]
            k_state = (k[h][sl] * jnp.exp(b_last - bc[h][sl])).astype(BF16)
            decay = jnp.exp(b_last)
            v_c = v_ref[sl, vs].astype(BF16)
            scores = jnp.where(causal, _dot_nt(q_dec[h][sl], k_intra[h][sl]), 0.0)
            st = st_ref[h]
            o = _dot(scores.astype(BF16), v_c) + _dot_nt(q_dec[h][sl], st.astype(BF16))
            st_ref[h] = st * decay + _dot_tn(v_c, k_state)
            o = o * lax.rsqrt(jnp.mean(o * o, axis=-1, keepdims=True) + RMS_EPS) * ng
            r = r_ref[sl, vs]
            o_ref[sl, vs] = (o * (r * (1.0 / (1.0 + jnp.exp(-r))))).astype(o_ref.dtype)


def _gla_recurrence(proj, bc, norm_g, cast_weights=()):
    B, T, _ = proj.shape
    QK = bc.shape[-1]
    dk = QK // GLA_HEADS
    dv = norm_g.shape[-1]
    V = dv * GLA_HEADS
    hp = GLA_HEADS_PER_STEP
    assert dk % LANES == 0 and dv % LANES == 0 and GLA_HEADS % hp == 0
    wk, wv = hp * dk, hp * dv
    assert QK % wk == 0 and (2 * QK) % wv == 0 and (2 * QK + V) % wv == 0
    n_chunks = min(8, T // GLA_CHUNK)
    tb = n_chunks * GLA_CHUNK
    assert T % tb == 0
    kk = QK // wk
    kv, kr = 2 * QK // wv, (2 * QK + V) // wv
    n_hg, n_t = GLA_HEADS // hp, T // tb
    c_in, c_out, c_shapes, c_bytes = _cast_specs(cast_weights, B * n_hg * n_t,
                                                 lambda b, h, t: (b * n_hg + h) * n_t + t)
    blocks = (2 * (3 * tb * wk * 4 + 2 * tb * wv * 4 + tb * wv * 2) + hp * dv * dk * 4 + 8 * tb * wk * 4
              + c_bytes)
    outs = pl.pallas_call(
        functools.partial(_gla_rec_kernel, n_chunks=n_chunks, dk=dk, dv=dv, n_cast=len(cast_weights)),
        grid=(B, n_hg, n_t),
        in_specs=[
            pl.BlockSpec((None, tb, wk), lambda b, h, t: (b, t, h)),
            pl.BlockSpec((None, tb, wk), lambda b, h, t: (b, t, kk + h)),
            pl.BlockSpec((None, tb, wv), lambda b, h, t: (b, t, kv + h)),
            pl.BlockSpec((None, tb, wv), lambda b, h, t: (b, t, kr + h)),
            pl.BlockSpec((None, tb, wk), lambda b, h, t: (b, t, h)),
            pl.BlockSpec((1, dv), lambda b, h, t: (0, 0)),
        ] + c_in,
        out_specs=[pl.BlockSpec((None, tb, wv), lambda b, h, t: (b, t, h))] + c_out,
        out_shape=[jax.ShapeDtypeStruct((B, T, V), BF16)] + c_shapes,
        scratch_shapes=[pltpu.VMEM((hp, dv, dk), F32)],
        compiler_params=_params(("arbitrary", "arbitrary", "arbitrary"), blocks),
        name="gla_recurrence",
    )(proj, proj, proj, proj, bc, norm_g.reshape(1, dv), *cast_weights)
    return outs[0], tuple(outs[1:])


def _proj_ln_kernel(a_ref, w_ref, x_ref, g_ref, lg_ref, lb_ref, o_ref, *, alpha):
    for sl in _row_slices(a_ref.shape[0], 4):
        y = _dot(a_ref[sl, :], w_ref[...])
        z = alpha * x_ref[sl, :] + (1.0 + g_ref[...]) * y
        o_ref[sl, :] = _layer_norm(z, lg_ref[...], lb_ref[...])


def _proj_residual_ln(a, w, x, mod4, layer, gate_slot, ln_g, ln_b, alpha):
    B, T, D = x.shape
    K = a.shape[-1]
    tm = _tile(T, 512)
    blocks = 2 * tm * K * 2 + 2 * K * D * 2 + 4 * tm * D * 4 + 3 * tm * D * 4
    return pl.pallas_call(
        functools.partial(_proj_ln_kernel, alpha=alpha),
        grid=(B, T // tm),
        in_specs=[
            pl.BlockSpec((None, tm, K), lambda b, i: (b, i, 0)),
            pl.BlockSpec((K, D), lambda b, i: (0, 0)),
            pl.BlockSpec((None, tm, D), lambda b, i: (b, i, 0)),
            _mod_spec(layer, gate_slot, D, 2),
            pl.BlockSpec((1, D), lambda b, i: (0, 0)),
            pl.BlockSpec((1, D), lambda b, i: (0, 0)),
        ],
        out_specs=pl.BlockSpec((None, tm, D), lambda b, i: (b, i, 0)),
        out_shape=jax.ShapeDtypeStruct((B, T, D), F32),
        compiler_params=_params(("arbitrary", "arbitrary"), blocks),
        name="proj_residual_ln",
    )(a, w, x, mod4, ln_g.reshape(1, D), ln_b.reshape(1, D))


MLP_TM = 1024
MLP_TF = 1024
MLP_ROW_GROUP = 256

def _mlp_kernel(x_ref, sh_ref, sc_ref, g_ref, w1_ref, b1_ref, w2_ref, b2_ref, lg_ref, lb_ref,
                o_ref, h_ref, *, alpha):
    f = pl.program_id(2)
    last = pl.num_programs(2) - 1
    groups = _row_slices(x_ref.shape[0], x_ref.shape[0] // MLP_ROW_GROUP)

    def contribution(sl):
        u = jnp.maximum(_dot(h_ref[sl, :], w1_ref[...]) + b1_ref[...], 0.0)
        return _dot((u * u).astype(BF16), w2_ref[...])

    @pl.when(f == 0)
    def _():
        for sl in groups:
            h_ref[sl, :] = (x_ref[sl, :] * (1.0 + sc_ref[...]) + sh_ref[...]).astype(BF16)
            o_ref[sl, :] = contribution(sl)

    @pl.when(jnp.logical_and(f > 0, f < last))
    def _():
        for sl in groups:
            o_ref[sl, :] += contribution(sl)

    @pl.when(f == last)
    def _():
        for sl in groups:
            y = o_ref[sl, :] + contribution(sl) + b2_ref[...]
            z = alpha * x_ref[sl, :] + (1.0 + g_ref[...]) * y
            o_ref[sl, :] = _layer_norm(z, lg_ref[...], lb_ref[...])


def _mlp_block(x, mod4, layer, w1, b1, w2, b2, ln_g, ln_b, alpha):
    B, T, D = x.shape
    F = w1.shape[1]
    tm, tf = _tile(T, MLP_TM), _tile(F, MLP_TF)
    assert F // tf >= 2, "the first and the last d_ff step must be distinct grid steps"
    assert tm % MLP_ROW_GROUP == 0
    blocks = (4 * tm * D * 4 + tm * D * 2 + 4 * D * tf * 2 + tm * tf * 6 + 3 * MLP_ROW_GROUP * D * 4)
    return pl.pallas_call(
        functools.partial(_mlp_kernel, alpha=alpha),
        grid=(B, T // tm, F // tf),
        in_specs=[
            pl.BlockSpec((None, tm, D), lambda b, i, f: (b, i, 0)),
            _mod_spec(layer, 3, D, 3),
            _mod_spec(layer, 4, D, 3),
            _mod_spec(layer, 5, D, 3),
            pl.BlockSpec((D, tf), lambda b, i, f: (0, f)),
            pl.BlockSpec((1, tf), lambda b, i, f: (0, f)),
            pl.BlockSpec((tf, D), lambda b, i, f: (f, 0)),
            pl.BlockSpec((1, D), lambda b, i, f: (0, 0)),
            pl.BlockSpec((1, D), lambda b, i, f: (0, 0)),
            pl.BlockSpec((1, D), lambda b, i, f: (0, 0)),
        ],
        out_specs=pl.BlockSpec((None, tm, D), lambda b, i, f: (b, i, 0)),
        out_shape=jax.ShapeDtypeStruct((B, T, D), F32),
        scratch_shapes=[pltpu.VMEM((tm, D), BF16)],
        compiler_params=_params(("arbitrary", "arbitrary", "arbitrary"), blocks),
        name="mlp_block",
    )(x, mod4, mod4, mod4, w1, b1.reshape(1, F), w2, b2.reshape(1, D), ln_g.reshape(1, D), ln_b.reshape(1, D))


SB_BLK = 256
LOG2_E = 1.4426950408889634
LN_2 = 0.6931471805599453
SB_DEAD_LOG2 = 160.0
SB_NO_KEYS = 1e30


def _sb_attn_kernel(q_ref, k_ref, v_ref, *rest, n_pairs, n_cast):
    o_ref = rest[n_cast]
    _cast_slabs(rest[:n_cast], rest[n_cast + 1:])
    _sb_attn_body(q_ref, k_ref, v_ref, o_ref, n_pairs)


def _sb_attn_body(q_ref, k_ref, v_ref, o_ref, n_pairs):
    n = SB_BLK
    dh = q_ref.shape[-1]
    jrow = lax.broadcasted_iota(jnp.int32, (n, n), 0)
    jcol = lax.broadcasted_iota(jnp.int32, (n, n), 1)
    later = (jrow > jcol).astype(BF16)
    diag_mask = jcol < jrow

    def scores(q, blk):
        return _dot_nt(q, k_ref[pl.ds(pl.multiple_of(blk * n, n), n), :])

    def log_terms(z, mask):
        neg_abs = lax.bitcast_convert_type(
            lax.bitcast_convert_type(z, jnp.uint32) | jnp.uint32(0x80000000), F32)
        sp = jnp.maximum(z, 0.0) + jnp.log(1.0 + jnp.exp2(neg_abs)) * LOG2_E
        lb = z - sp
        if mask is not None:
            sp = jnp.where(mask, sp, 0.0)
        return sp.astype(BF16), lb, sp[:, 0:1]

    def suffix(hilo):
        return _dot(hilo, later)

    def weights(lb, sfx, r, sp0, mask):
        a = jnp.exp2(lb - sfx if r is None else lb - (sfx + r))
        if mask is not None:
            a = jnp.where(mask, a, 0.0)
        tot = sfx[:, 0:1] + sp0
        return a.astype(BF16), tot if r is None else r + tot

    def values(a, blk):
        return _dot(a, v_ref[pl.ds(pl.multiple_of(blk * n, n), n), :])

    def step(qa, qb, blk_a, blk_b, ra, rb, acc_a, acc_b, mask):
        za = scores(qa, blk_a)
        zb = scores(qb, blk_b)
        hla, lba, spa0 = log_terms(za, mask)
        sfa = suffix(hla)
        hlb, lbb, spb0 = log_terms(zb, mask)
        aa, ra = weights(lba, sfa, ra, spa0, mask)
        sfb = suffix(hlb)
        acc_a = acc_a + values(aa, blk_a)
        ab, rb = weights(lbb, sfb, rb, spb0, mask)
        acc_b = acc_b + values(ab, blk_b)
        return ra, rb, acc_a, acc_b

    def pair_body(p, _):
        blk_a0 = 2 * p
        blk_b0 = blk_a0 + 1
        row_a = pl.multiple_of(blk_a0 * n, n)
        row_b = pl.multiple_of(blk_b0 * n, n)
        qa = q_ref[pl.ds(row_a, n), :]
        qb = q_ref[pl.ds(row_b, n), :]

        blk_a1 = jnp.maximum(blk_a0 - 1, 0)
        blk_b1 = blk_a0
        za0 = scores(qa, blk_a0)
        zb0 = scores(qb, blk_b0)
        za1 = scores(qa, blk_a1)
        zb1 = scores(qb, blk_b1)
        hla0, lba0, spa0 = log_terms(za0, diag_mask)
        sfa0 = suffix(hla0)
        hlb0, lbb0, spb0 = log_terms(zb0, diag_mask)
        sfb0 = suffix(hlb0)
        hla1, lba1, spa1 = log_terms(za1, None)
        aa0, ra = weights(lba0, sfa0, None, spa0, diag_mask)
        sfa1 = suffix(hla1)
        acc_a = values(aa0, blk_a0)
        hlb1, lbb1, spb1 = log_terms(zb1, None)
        ab0, rb = weights(lbb0, sfb0, None, spb0, diag_mask)
        sfb1 = suffix(hlb1)
        acc_b = values(ab0, blk_b0)
        ra = jnp.where(p > 0, ra, SB_NO_KEYS)
        aa1, ra = weights(lba1, sfa1, ra, spa1, None)
        acc_a = acc_a + values(aa1, blk_a1)
        ab1, rb = weights(lbb1, sfb1, rb, spb1, None)
        acc_b = acc_b + values(ab1, blk_b1)

        def alive(ra, rb):
            return jnp.minimum(jnp.min(ra), jnp.min(rb)) < SB_DEAD_LOG2

        def cond(carry):
            i, go = carry[0], carry[1]
            return jnp.logical_and(i <= blk_b0, go)

        def body(carry):
            i, _, ra, rb, acc_a, acc_b = carry
            has_a = i <= blk_a0
            ra = jnp.where(has_a, ra, SB_NO_KEYS)
            ra, rb, acc_a, acc_b = step(qa, qb, jnp.maximum(blk_a0 - i, 0), blk_b0 - i,
                                        ra, rb, acc_a, acc_b, None)
            return i + 1, alive(ra, rb), ra, rb, acc_a, acc_b

        carry = lax.while_loop(cond, body, (jnp.int32(2), alive(ra, rb), ra, rb, acc_a, acc_b))
        o_ref[pl.ds(row_a, n), :] = carry[4].astype(o_ref.dtype)
        o_ref[pl.ds(row_b, n), :] = carry[5].astype(o_ref.dtype)
        return 0

    lax.fori_loop(0, n_pairs, pair_body, 0)


def _sb_attention(q, kv, cast_weights=()):
    B, T, HD = q.shape
    dh = HD // SB_HEADS
    assert dh % LANES == 0 and T % (2 * SB_BLK) == 0
    c_in, c_out, c_shapes, c_bytes = _cast_specs(cast_weights, B * SB_HEADS, lambda b, h: b * SB_HEADS + h)
    blocks = 8 * T * dh * 2 + 24 * SB_BLK * SB_BLK * 4 + c_bytes
    outs = pl.pallas_call(
        functools.partial(_sb_attn_kernel, n_pairs=T // (2 * SB_BLK), n_cast=len(cast_weights)),
        grid=(B, SB_HEADS),
        in_specs=[
            pl.BlockSpec((None, T, dh), lambda b, h: (b, 0, h)),
            pl.BlockSpec((None, T, dh), lambda b, h: (b, 0, h)),
            pl.BlockSpec((None, T, dh), lambda b, h: (b, 0, SB_HEADS + h)),
        ] + c_in,
        out_specs=[pl.BlockSpec((None, T, dh), lambda b, h: (b, 0, h))] + c_out,
        out_shape=[jax.ShapeDtypeStruct((B, T, HD), BF16)] + c_shapes,
        compiler_params=_params(("arbitrary", "arbitrary"), blocks),
        name="stick_breaking_attention",
    )(q, kv, kv, *cast_weights)
    return outs[0], tuple(outs[1:])


def kernel(x, c, ada_w, ada_b, ln_g, ln_b, gla_w_in, gla_w_gate_up, gla_b_gate, gla_norm_g, gla_w_out, kv_ada_w, kv_ada_b, w_kv, sb_w_q, sb_w_out, mlp_w1, mlp_b1, mlp_w2, mlp_b2):
    B, T, D = x.shape
    depth = ada_w.shape[0]
    n_a = gla_w_in.shape[0]
    alpha = (2.0 * depth) ** 0.25
    QK = gla_w_gate_up.shape[-1]
    V = gla_w_out.shape[1]
    n_main = 2 * QK + 2 * V
    dh = sb_w_q.shape[-1] // SB_HEADS

    c_pad = jnp.pad(c, ((0, MOD_ROWS - B), (0, 0)))
    mod4 = _mod_vectors(c_pad, ada_w, ada_b).reshape(depth, MOD_ROWS, 1, N_MOD * D)
    kv_mod4 = _mod_vectors(c_pad, kv_ada_w[None], kv_ada_b[None]).reshape(1, MOD_ROWS, 1, 2 * D)

    def projection_weights(l):
        if l >= depth:
            return {}
        if l < n_a:
            return {"w_in": gla_w_in[l]}
        j = l - n_a
        return {"w_q": sb_w_q[j], **({"w_kv": w_kv} if j == 0 else {})}

    bf = {name: w.astype(BF16) for name, w in projection_weights(0).items()}
    kv = None
    for l in range(depth):
        upcoming = projection_weights(l + 1)
        w_out_f32 = gla_w_out[l] if l < n_a else sb_w_out[l - n_a]
        names = ["w_out", "w1", "w2"] + list(upcoming)
        to_cast = (w_out_f32, mlp_w1[l], mlp_w2[l]) + tuple(upcoming.values())
        if l < n_a:
            w_in = bf["w_in"]
            pad = LANES - GLA_GATE_RANK
            w_gate = jnp.pad(w_in[:, n_main:], ((0, 0), (0, pad)))
            w_up = jnp.pad(gla_w_gate_up[l], ((0, pad), (0, 0))).astype(BF16)
            proj, gl = _mod_matmul(x, mod4, l, 0, 1, w_in, F32, n_cols=n_main, w_side=w_gate)
            bc = _gla_cum_log_decay(gl, w_up, gla_b_gate[l])
            mixed, cast = _gla_recurrence(proj, bc, gla_norm_g[l], to_cast)
        else:
            if kv is None:
                kv = _mod_matmul(x, kv_mod4, 0, 0, 1, bf["w_kv"], BF16)
            q = _mod_matmul(x, mod4, l, 0, 1, bf["w_q"], BF16, out_scale=dh ** -0.5 * LOG2_E)
            mixed, cast = _sb_attention(q, kv, to_cast)
        bf = dict(zip(names, cast))
        x = _proj_residual_ln(mixed, bf["w_out"], x, mod4, l, 2, ln_g[l, 0], ln_b[l, 0], alpha)
        x = _mlp_block(x, mod4, l, bf["w1"], mlp_b1[l], bf["w2"], mlp_b2[l], ln_g[l, 1], ln_b[l, 1], alpha)
    return x
```

```python
import functools

import jax
import jax.numpy as jnp
from jax import lax
from jax.experimental import pallas as pl
from jax.experimental.pallas import tpu as pltpu

F32 = jnp.float32
BF16 = jnp.bfloat16

GLA_HEADS = 4
GLA_GATE_RANK = 16
GLA_GATE_TAU = 16.0
GLA_CHUNK = 64
SB_HEADS = 16
N_MOD = 6
LN_EPS = 1e-5
RMS_EPS = 1e-5

LANES = 128
MOD_ROWS = 8
VMEM_LIMIT_CAP = 58 << 20
VMEM_HEADROOM = 6 << 20


def _params(semantics, block_bytes):
    limit = min(int(block_bytes) + VMEM_HEADROOM, VMEM_LIMIT_CAP)
    return pltpu.CompilerParams(dimension_semantics=semantics, vmem_limit_bytes=limit)


def _tile(n, want):
    t = min(n, want)
    assert n % t == 0, (n, want)
    return t


def _row_slices(n_rows, n_split):
    assert n_split >= 1 and n_rows % n_split == 0
    step = n_rows // n_split
    return [slice(i * step, (i + 1) * step) for i in range(n_split)]


def _weight_shape(w):
    return (w[0] if isinstance(w, tuple) else w).shape[-2:]


def _weight_operand(w, block, index_map):
    if isinstance(w, tuple):
        stacked, entry = w
        return stacked, pl.BlockSpec((None,) + block, lambda *g: (entry,) + index_map(*g))
    return w, pl.BlockSpec(block, index_map)


def _dot(a, b):
    return jnp.dot(a, b, preferred_element_type=F32)


def _dot_nt(a, b):
    return lax.dot_general(a, b, (((1,), (1,)), ((), ())), preferred_element_type=F32)


def _dot_tn(a, b):
    return lax.dot_general(a, b, (((0,), (0,)), ((), ())), preferred_element_type=F32)


def _split_bf16(x):
    hi = x.astype(BF16)
    lo = (x - hi.astype(F32)).astype(BF16)
    return hi, lo


BF16_SUBLANES = 16


def _cast_specs(weights, n_steps, step_index):
    in_specs, out_specs, out_shapes, nbytes = [], [], [], 0
    for w, layer in weights:
        R, C = w.shape[-2:]
        rows = R // n_steps
        assert R % n_steps == 0 and rows % BF16_SUBLANES == 0, (w.shape, n_steps)
        if layer is None:
            in_specs.append(pl.BlockSpec((rows, C), lambda *g: (step_index(*g), 0)))
        else:
            in_specs.append(pl.BlockSpec((None, rows, C), lambda *g, _l=layer: (_l, step_index(*g), 0)))
        out_specs.append(pl.BlockSpec((rows, C), lambda *g: (step_index(*g), 0)))
        out_shapes.append(jax.ShapeDtypeStruct((R, C), BF16))
        nbytes += 2 * rows * C * (4 + 2)
    return in_specs, out_specs, out_shapes, nbytes


def _cast_slabs(src_refs, dst_refs):
    for src, dst in zip(src_refs, dst_refs):
        dst[...] = src[...].astype(dst.dtype)


def _layer_norm(z, g, b):
    mu = jnp.mean(z, axis=-1, keepdims=True)
    zc = z - mu
    var = jnp.mean(zc * zc, axis=-1, keepdims=True)
    return zc * lax.rsqrt(var + LN_EPS) * g + b


def _mod_kernel(c_ref, w_ref, b_ref, o_ref):
    c = c_ref[...]
    cs = c * (1.0 / (1.0 + jnp.exp(-c)))
    hi, lo = _split_bf16(cs)
    y = _dot(jnp.concatenate([hi, lo], axis=0), w_ref[...].astype(BF16))
    o_ref[...] = y[:MOD_ROWS] + y[MOD_ROWS:] + b_ref[...]


def _mod_vectors(c_pad, w, b):
    L, D, N = w.shape
    tn = _tile(N, 1024)
    blocks = 2 * (D * tn * 4) + MOD_ROWS * D * 4 + 4 * MOD_ROWS * tn * 4
    return pl.pallas_call(
        _mod_kernel,
        grid=(L, N // tn),
        in_specs=[
            pl.BlockSpec((MOD_ROWS, D), lambda l, j: (0, 0)),
            pl.BlockSpec((None, D, tn), lambda l, j: (l, 0, j)),
            pl.BlockSpec((None, 1, tn), lambda l, j: (l, 0, j)),
        ],
        out_specs=pl.BlockSpec((None, MOD_ROWS, tn), lambda l, j: (l, 0, j)),
        out_shape=jax.ShapeDtypeStruct((L, MOD_ROWS, N), F32),
        compiler_params=_params(("arbitrary", "arbitrary"), blocks),
        name="adaln_vectors",
    )(c_pad, w, b.reshape(L, 1, N))


def _mod_spec(layer, slot, D, n_grid):
    if n_grid == 2:
        return pl.BlockSpec((None, None, 1, D), lambda b, i: (layer, b, 0, slot))
    return pl.BlockSpec((None, None, 1, D), lambda b, i, j: (layer, b, 0, slot))


MODMM_ROW_GROUP = 256


def _modmm_kernel(x_ref, sh_ref, sc_ref, w_ref, *rest, out_scale, with_side):
    if with_side:
        ws_ref, o_ref, side_ref, h_ref = rest
    else:
        o_ref, h_ref = rest
    groups = _row_slices(x_ref.shape[0], x_ref.shape[0] // MODMM_ROW_GROUP)

    def emit(sl):
        y = _dot(h_ref[sl, :], w_ref[...])
        if out_scale != 1.0:
            y = y * out_scale
        o_ref[sl, :] = y.astype(o_ref.dtype)

    @pl.when(pl.program_id(2) == 0)
    def _():
        for sl in groups:
            h_ref[sl, :] = (x_ref[sl, :] * (1.0 + sc_ref[...]) + sh_ref[...]).astype(BF16)
            emit(sl)
            if with_side:
                side_ref[sl, :] = _dot(h_ref[sl, :], ws_ref[...])

    @pl.when(pl.program_id(2) > 0)
    def _():
        for sl in groups:
            emit(sl)


def _mod_matmul(x, mod4, layer, sh_slot, sc_slot, w, out_dtype, out_scale=1.0, n_cols=None, w_side=None):
    B, T, D = x.shape
    N = _weight_shape(w)[1] if n_cols is None else n_cols
    tm = _tile(T, 1024)
    osz = jnp.dtype(out_dtype).itemsize
    S = 0 if w_side is None else w_side.shape[1]

    def block_bytes(tn):
        return (2 * tm * D * 4 + 2 * D * tn * 2 + 2 * tm * tn * osz + tm * D * 2
                + 2 * MODMM_ROW_GROUP * tn * 4 + 2 * D * S * 2 + 2 * tm * S * 4)

    tn = next(t for t in (2048, 1536, 1024, 512, 256, 128)
              if N % t == 0 and block_bytes(t) + VMEM_HEADROOM <= VMEM_LIMIT_CAP)
    blocks = block_bytes(tn)
    w_arr, w_spec = _weight_operand(w, (D, tn), lambda b, i, j: (0, j))
    in_specs = [
        pl.BlockSpec((None, tm, D), lambda b, i, j: (b, i, 0)),
        _mod_spec(layer, sh_slot, D, 3),
        _mod_spec(layer, sc_slot, D, 3),
        w_spec,
    ]
    out_specs = pl.BlockSpec((None, tm, tn), lambda b, i, j: (b, i, j))
    out_shape = jax.ShapeDtypeStruct((B, T, N), out_dtype)
    args = (x, mod4, mod4, w_arr)
    if w_side is not None:
        in_specs.append(pl.BlockSpec((D, S), lambda b, i, j: (0, 0)))
        out_specs = (out_specs, pl.BlockSpec((None, tm, S), lambda b, i, j: (b, i, 0)))
        out_shape = (out_shape, jax.ShapeDtypeStruct((B, T, S), F32))
        args = args + (w_side,)
    return pl.pallas_call(
        functools.partial(_modmm_kernel, out_scale=out_scale, with_side=w_side is not None),
        grid=(B, T // tm, N // tn),
        in_specs=in_specs,
        out_specs=out_specs,
        out_shape=out_shape,
        scratch_shapes=[pltpu.VMEM((tm, D), BF16)],
        compiler_params=_params(("arbitrary", "arbitrary", "arbitrary"), blocks),
        name="modulated_matmul",
    )(*args)


def _gla_gate_kernel(gl_ref, wup_ref, bg_ref, tri_ref, o_ref):
    g = _dot(gl_ref[...].astype(BF16), wup_ref[...]) + bg_ref[...]
    log_a = (jnp.minimum(g, 0.0) - jnp.log(1.0 + jnp.exp(-jnp.abs(g)))) * (1.0 / GLA_GATE_TAU)
    hi, lo = _split_bf16(log_a)
    o_ref[...] = _dot(tri_ref[...], jnp.concatenate([hi, lo], axis=0))


def _chunk_cumsum_matrix(tm):
    row = lax.broadcasted_iota(jnp.int32, (tm, tm), 0)
    col = lax.broadcasted_iota(jnp.int32, (tm, tm), 1)
    tri = ((col <= row) & (row // GLA_CHUNK == col // GLA_CHUNK)).astype(BF16)
    return jnp.concatenate([tri, tri], axis=1)


def _gla_cum_log_decay(gl, w_up, b_gate):
    B, T, R = gl.shape
    QK = w_up.shape[1]
    tm = _tile(T, 512)
    assert tm % GLA_CHUNK == 0
    blocks = 2 * tm * R * 4 + 2 * R * QK * 2 + 2 * tm * 2 * tm * 2 + 8 * tm * QK * 4
    return pl.pallas_call(
        _gla_gate_kernel,
        grid=(B, T // tm),
        in_specs=[
            pl.BlockSpec((None, tm, R), lambda b, i: (b, i, 0)),
            pl.BlockSpec((R, QK), lambda b, i: (0, 0)),
            pl.BlockSpec((1, QK), lambda b, i: (0, 0)),
            pl.BlockSpec((tm, 2 * tm), lambda b, i: (0, 0)),
        ],
        out_specs=pl.BlockSpec((None, tm, QK), lambda b, i: (b, i, 0)),
        out_shape=jax.ShapeDtypeStruct((B, T, QK), F32),
        compiler_params=_params(("arbitrary", "arbitrary"), blocks),
        name="gla_cum_log_decay",
    )(gl, w_up, b_gate.reshape(1, QK), _chunk_cumsum_matrix(tm))


GLA_HEADS_PER_STEP = 2


def _gla_rec_kernel(q_ref, k_ref, v_ref, r_ref, bc_ref, ng_ref, o_ref, st_ref, *, n_chunks, dk, dv):
    C = GLA_CHUNK

    @pl.when(pl.program_id(2) == 0)
    def _():
        st_ref[...] = jnp.zeros_like(st_ref)

    crow = lax.broadcasted_iota(jnp.int32, (C, C), 0)
    ccol = lax.broadcasted_iota(jnp.int32, (C, C), 1)
    causal = ccol <= crow
    ng = ng_ref[...]

    bc, k, q_dec, k_intra = [], [], [], []
    for h in range(GLA_HEADS_PER_STEP):
        ks = slice(h * dk, (h + 1) * dk)
        bc.append(bc_ref[:, ks])
        k.append(k_ref[:, ks])
        q_dec.append((q_ref[:, ks] * (dk ** -0.5) * jnp.exp(bc[h])).astype(BF16))
        k_intra.append((k[h] * jnp.exp(-bc[h])).astype(BF16))

    for c in range(n_chunks):
        sl = slice(c * C, (c + 1) * C)
        for h in range(GLA_HEADS_PER_STEP):
            vs = slice(h * dv, (h + 1) * dv)
            b_last = bc[h][(c + 1) * C - 1:(c + 1) * C, :]
            k_state = (k[h][sl] * jnp.exp(b_last - bc[h][sl])).astype(BF16)
            decay = jnp.exp(b_last)
            v_c = v_ref[sl, vs].astype(BF16)
            scores = jnp.where(causal, _dot_nt(q_dec[h][sl], k_intra[h][sl]), 0.0)
            st = st_ref[h]
            o = _dot(scores.astype(BF16), v_c) + _dot_nt(q_dec[h][sl], st.astype(BF16))
            st_ref[h] = st * decay + _dot_tn(v_c, k_state)
            o = o * lax.rsqrt(jnp.mean(o * o, axis=-1, keepdims=True) + RMS_EPS) * ng
            r = r_ref[sl, vs]
            o_ref[sl, vs] = (o * (r * (1.0 / (1.0 + jnp.exp(-r))))).astype(o_ref.dtype)


def _gla_recurrence(proj, bc, norm_g):
    B, T, _ = proj.shape
    QK = bc.shape[-1]
    dk = QK // GLA_HEADS
    dv = norm_g.shape[-1]
    V = dv * GLA_HEADS
    hp = GLA_HEADS_PER_STEP
    assert dk % LANES == 0 and dv % LANES == 0 and GLA_HEADS % hp == 0
    wk, wv = hp * dk, hp * dv
    assert QK % wk == 0 and (2 * QK) % wv == 0 and (2 * QK + V) % wv == 0
    n_chunks = min(8, T // GLA_CHUNK)
    tb = n_chunks * GLA_CHUNK
    assert T % tb == 0
    kk = QK // wk
    kv, kr = 2 * QK // wv, (2 * QK + V) // wv
    blocks = (2 * (3 * tb * wk * 4 + 2 * tb * wv * 4 + tb * wv * 2) + hp * dv * dk * 4 + 8 * tb * wk * 4)
    return pl.pallas_call(
        functools.partial(_gla_rec_kernel, n_chunks=n_chunks, dk=dk, dv=dv),
        grid=(B, GLA_HEADS // hp, T // tb),
        in_specs=[
            pl.BlockSpec((None, tb, wk), lambda b, h, t: (b, t, h)),
            pl.BlockSpec((None, tb, wk), lambda b, h, t: (b, t, kk + h)),
            pl.BlockSpec((None, tb, wv), lambda b, h, t: (b, t, kv + h)),
            pl.BlockSpec((None, tb, wv), lambda b, h, t: (b, t, kr + h)),
            pl.BlockSpec((None, tb, wk), lambda b, h, t: (b, t, h)),
            pl.BlockSpec((1, dv), lambda b, h, t: (0, 0)),
        ],
        out_specs=pl.BlockSpec((None, tb, wv), lambda b, h, t: (b, t, h)),
        out_shape=jax.ShapeDtypeStruct((B, T, V), BF16),
        scratch_shapes=[pltpu.VMEM((hp, dv, dk), F32)],
        compiler_params=_params(("arbitrary", "arbitrary", "arbitrary"), blocks),
        name="gla_recurrence",
    )(proj, proj, proj, proj, bc, norm_g.reshape(1, dv))


def _proj_ln_kernel(a_ref, w_ref, x_ref, g_ref, lg_ref, lb_ref, o_ref, *, alpha):
    for sl in _row_slices(a_ref.shape[0], 4):
        y = _dot(a_ref[sl, :], w_ref[...])
        z = alpha * x_ref[sl, :] + (1.0 + g_ref[...]) * y
        o_ref[sl, :] = _layer_norm(z, lg_ref[...], lb_ref[...])


def _proj_residual_ln(a, w, x, mod4, layer, gate_slot, ln_g, ln_b, alpha):
    B, T, D = x.shape
    K = a.shape[-1]
    tm = _tile(T, 512)
    blocks = 2 * tm * K * 2 + 2 * K * D * 2 + 4 * tm * D * 4 + 3 * tm * D * 4
    w, w_spec = _weight_operand(w, (K, D), lambda b, i: (0, 0))
    return pl.pallas_call(
        functools.partial(_proj_ln_kernel, alpha=alpha),
        grid=(B, T // tm),
        in_specs=[
            pl.BlockSpec((None, tm, K), lambda b, i: (b, i, 0)),
            w_spec,
            pl.BlockSpec((None, tm, D), lambda b, i: (b, i, 0)),
            _mod_spec(layer, gate_slot, D, 2),
            pl.BlockSpec((1, D), lambda b, i: (0, 0)),
            pl.BlockSpec((1, D), lambda b, i: (0, 0)),
        ],
        out_specs=pl.BlockSpec((None, tm, D), lambda b, i: (b, i, 0)),
        out_shape=jax.ShapeDtypeStruct((B, T, D), F32),
        compiler_params=_params(("arbitrary", "arbitrary"), blocks),
        name="proj_residual_ln",
    )(a, w, x, mod4, ln_g.reshape(1, D), ln_b.reshape(1, D))


MLP_TM = 1024
MLP_TF = 1024
MLP_ROW_GROUP = 256

def _mlp_kernel(x_ref, sh_ref, sc_ref, g_ref, w1_ref, b1_ref, w2_ref, b2_ref, lg_ref, lb_ref,
                o_ref, h_ref, *, alpha):
    f = pl.program_id(2)
    last = pl.num_programs(2) - 1
    groups = _row_slices(x_ref.shape[0], x_ref.shape[0] // MLP_ROW_GROUP)

    def contribution(sl):
        u = jnp.maximum(_dot(h_ref[sl, :], w1_ref[...]) + b1_ref[...], 0.0)
        return _dot((u * u).astype(BF16), w2_ref[...])

    @pl.when(f == 0)
    def _():
        for sl in groups:
            h_ref[sl, :] = (x_ref[sl, :] * (1.0 + sc_ref[...]) + sh_ref[...]).astype(BF16)
            o_ref[sl, :] = contribution(sl)

    @pl.when(jnp.logical_and(f > 0, f < last))
    def _():
        for sl in groups:
            o_ref[sl, :] += contribution(sl)

    @pl.when(f == last)
    def _():
        for sl in groups:
            y = o_ref[sl, :] + contribution(sl) + b2_ref[...]
            z = alpha * x_ref[sl, :] + (1.0 + g_ref[...]) * y
            o_ref[sl, :] = _layer_norm(z, lg_ref[...], lb_ref[...])


def _mlp_block(x, mod4, layer, w1, b1, w2, b2, ln_g, ln_b, alpha):
    B, T, D = x.shape
    F = _weight_shape(w1)[1]
    tm, tf = _tile(T, MLP_TM), _tile(F, MLP_TF)
    assert F // tf >= 2, "the first and the last d_ff step must be distinct grid steps"
    assert tm % MLP_ROW_GROUP == 0
    blocks = (4 * tm * D * 4 + tm * D * 2 + 4 * D * tf * 2 + tm * tf * 6 + 3 * MLP_ROW_GROUP * D * 4)
    w1, w1_spec = _weight_operand(w1, (D, tf), lambda b, i, f: (0, f))
    w2, w2_spec = _weight_operand(w2, (tf, D), lambda b, i, f: (f, 0))
    return pl.pallas_call(
        functools.partial(_mlp_kernel, alpha=alpha),
        grid=(B, T // tm, F // tf),
        in_specs=[
            pl.BlockSpec((None, tm, D), lambda b, i, f: (b, i, 0)),
            _mod_spec(layer, 3, D, 3),
            _mod_spec(layer, 4, D, 3),
            _mod_spec(layer, 5, D, 3),
            w1_spec,
            pl.BlockSpec((1, tf), lambda b, i, f: (0, f)),
            w2_spec,
            pl.BlockSpec((1, D), lambda b, i, f: (0, 0)),
            pl.BlockSpec((1, D), lambda b, i, f: (0, 0)),
            pl.BlockSpec((1, D), lambda b, i, f: (0, 0)),
        ],
        out_specs=pl.BlockSpec((None, tm, D), lambda b, i, f: (b, i, 0)),
        out_shape=jax.ShapeDtypeStruct((B, T, D), F32),
        scratch_shapes=[pltpu.VMEM((tm, D), BF16)],
        compiler_params=_params(("arbitrary", "arbitrary", "arbitrary"), blocks),
        name="mlp_block",
    )(x, mod4, mod4, mod4, w1, b1.reshape(1, F), w2, b2.reshape(1, D), ln_g.reshape(1, D), ln_b.reshape(1, D))


SB_BLK = 256
LOG2_E = 1.4426950408889634
LN_2 = 0.6931471805599453
SB_DEAD_LOG2 = 160.0
SB_NO_KEYS = 1e30


def _sb_attn_kernel(q_ref, k_ref, v_ref, *rest, n_pairs, n_cast):
    o_ref = rest[n_cast]
    _cast_slabs(rest[:n_cast], rest[n_cast + 1:])
    _sb_attn_body(q_ref, k_ref, v_ref, o_ref, n_pairs)


def _sb_attn_body(q_ref, k_ref, v_ref, o_ref, n_pairs):
    n = SB_BLK
    dh = q_ref.shape[-1]
    jrow = lax.broadcasted_iota(jnp.int32, (n, n), 0)
    jcol = lax.broadcasted_iota(jnp.int32, (n, n), 1)
    later = (jrow > jcol).astype(BF16)
    diag_mask = jcol < jrow

    def scores(q, blk):
        return _dot_nt(q, k_ref[pl.ds(pl.multiple_of(blk * n, n), n), :])

    def log_terms(z, mask):
        neg_abs = lax.bitcast_convert_type(
            lax.bitcast_convert_type(z, jnp.uint32) | jnp.uint32(0x80000000), F32)
        sp = jnp.maximum(z, 0.0) + jnp.log(1.0 + jnp.exp2(neg_abs)) * LOG2_E
        lb = z - sp
        if mask is not None:
            sp = jnp.where(mask, sp, 0.0)
        return sp.astype(BF16), lb, sp[:, 0:1]

    def suffix(hilo):
        return _dot(hilo, later)

    def weights(lb, sfx, r, sp0, mask):
        a = jnp.exp2(lb - sfx if r is None else lb - (sfx + r))
        if mask is not None:
            a = jnp.where(mask, a, 0.0)
        tot = sfx[:, 0:1] + sp0
        return a.astype(BF16), tot if r is None else r + tot

    def values(a, blk):
        return _dot(a, v_ref[pl.ds(pl.multiple_of(blk * n, n), n), :])

    def step(qa, qb, blk_a, blk_b, ra, rb, acc_a, acc_b, mask):
        za = scores(qa, blk_a)
        zb = scores(qb, blk_b)
        hla, lba, spa0 = log_terms(za, mask)
        sfa = suffix(hla)
        hlb, lbb, spb0 = log_terms(zb, mask)
        aa, ra = weights(lba, sfa, ra, spa0, mask)
        sfb = suffix(hlb)
        acc_a = acc_a + values(aa, blk_a)
        ab, rb = weights(lbb, sfb, rb, spb0, mask)
        acc_b = acc_b + values(ab, blk_b)
        return ra, rb, acc_a, acc_b

    def pair_body(p, _):
        blk_a0 = 2 * p
        blk_b0 = blk_a0 + 1
        row_a = pl.multiple_of(blk_a0 * n, n)
        row_b = pl.multiple_of(blk_b0 * n, n)
        qa = q_ref[pl.ds(row_a, n), :]
        qb = q_ref[pl.ds(row_b, n), :]

        blk_a1 = jnp.maximum(blk_a0 - 1, 0)
        blk_b1 = blk_a0
        za0 = scores(qa, blk_a0)
        zb0 = scores(qb, blk_b0)
        za1 = scores(qa, blk_a1)
        zb1 = scores(qb, blk_b1)
        hla0, lba0, spa0 = log_terms(za0, diag_mask)
        sfa0 = suffix(hla0)
        hlb0, lbb0, spb0 = log_terms(zb0, diag_mask)
        sfb0 = suffix(hlb0)
        hla1, lba1, spa1 = log_terms(za1, None)
        aa0, ra = weights(lba0, sfa0, None, spa0, diag_mask)
        sfa1 = suffix(hla1)
        acc_a = values(aa0, blk_a0)
        hlb1, lbb1, spb1 = log_terms(zb1, None)
        ab0, rb = weights(lbb0, sfb0, None, spb0, diag_mask)
        sfb1 = suffix(hlb1)
        acc_b = values(ab0, blk_b0)
        ra = jnp.where(p > 0, ra, SB_NO_KEYS)
        aa1, ra = weights(lba1, sfa1, ra, spa1, None)
        acc_a = acc_a + values(aa1, blk_a1)
        ab1, rb = weights(lbb1, sfb1, rb, spb1, None)
        acc_b = acc_b + values(ab1, blk_b1)

        def alive(ra, rb):
            return jnp.minimum(jnp.min(ra), jnp.min(rb)) < SB_DEAD_LOG2

        def cond(carry):
            i, go = carry[0], carry[1]
            return jnp.logical_and(i <= blk_b0, go)

        def body(carry):
            i, _, ra, rb, acc_a, acc_b = carry
            has_a = i <= blk_a0
            ra = jnp.where(has_a, ra, SB_NO_KEYS)
            ra, rb, acc_a, acc_b = step(qa, qb, jnp.maximum(blk_a0 - i, 0), blk_b0 - i,
                                        ra, rb, acc_a, acc_b, None)
            return i + 1, alive(ra, rb), ra, rb, acc_a, acc_b

        carry = lax.while_loop(cond, body, (jnp.int32(2), alive(ra, rb), ra, rb, acc_a, acc_b))
        o_ref[pl.ds(row_a, n), :] = carry[4].astype(o_ref.dtype)
        o_ref[pl.ds(row_b, n), :] = carry[5].astype(o_ref.dtype)
        return 0

    lax.fori_loop(0, n_pairs, pair_body, 0)


def _sb_attention(q, kv, cast_weights=()):
    B, T, HD = q.shape
    dh = HD // SB_HEADS
    assert dh % LANES == 0 and T % (2 * SB_BLK) == 0
    c_in, c_out, c_shapes, c_bytes = _cast_specs(cast_weights, B * SB_HEADS, lambda b, h: b * SB_HEADS + h)
    blocks = 8 * T * dh * 2 + 24 * SB_BLK * SB_BLK * 4 + c_bytes
    outs = pl.pallas_call(
        functools.partial(_sb_attn_kernel, n_pairs=T // (2 * SB_BLK), n_cast=len(cast_weights)),
        grid=(B, SB_HEADS),
        in_specs=[
            pl.BlockSpec((None, T, dh), lambda b, h: (b, 0, h)),
            pl.BlockSpec((None, T, dh), lambda b, h: (b, 0, h)),
            pl.BlockSpec((None, T, dh), lambda b, h: (b, 0, SB_HEADS + h)),
        ] + c_in,
        out_specs=[pl.BlockSpec((None, T, dh), lambda b, h: (b, 0, h))] + c_out,
        out_shape=[jax.ShapeDtypeStruct((B, T, HD), BF16)] + c_shapes,
        compiler_params=_params(("arbitrary", "arbitrary"), blocks),
        name="stick_breaking_attention",
    )(q, kv, kv, *[w for w, _ in cast_weights])
    return outs[0], tuple(outs[1:])


def kernel(x, c, ada_w, ada_b, ln_g, ln_b, gla_w_in, gla_w_gate_up, gla_b_gate, gla_norm_g, gla_w_out, kv_ada_w, kv_ada_b, w_kv, sb_w_q, sb_w_out, mlp_w1, mlp_b1, mlp_w2, mlp_b2):
    B, T, D = x.shape
    depth = ada_w.shape[0]
    n_a = gla_w_in.shape[0]
    alpha = (2.0 * depth) ** 0.25
    QK = gla_w_gate_up.shape[-1]
    V = gla_w_out.shape[1]
    n_main = 2 * QK + 2 * V
    dh = sb_w_q.shape[-1] // SB_HEADS

    c_pad = jnp.pad(c, ((0, MOD_ROWS - B), (0, 0)))
    mod4 = _mod_vectors(c_pad, ada_w, ada_b).reshape(depth, MOD_ROWS, 1, N_MOD * D)
    kv_mod4 = _mod_vectors(c_pad, kv_ada_w[None], kv_ada_b[None]).reshape(1, MOD_ROWS, 1, 2 * D)

    gla_w_in_bf = gla_w_in.astype(BF16)
    gla_w_out_bf = gla_w_out.astype(BF16)
    gla_w1_bf = mlp_w1[:n_a].astype(BF16)
    gla_w2_bf = mlp_w2[:n_a].astype(BF16)
    w_kv_bf = w_kv.astype(BF16)
    sb_w_q_bf = sb_w_q.astype(BF16)

    kv = None
    for l in range(depth):
        if l < n_a:
            pad = LANES - GLA_GATE_RANK
            w_gate = jnp.pad(gla_w_in_bf[l, :, n_main:], ((0, 0), (0, pad)))
            w_up = jnp.pad(gla_w_gate_up[l], ((0, pad), (0, 0))).astype(BF16)
            proj, gl = _mod_matmul(x, mod4, l, 0, 1, (gla_w_in_bf, l), F32, n_cols=n_main, w_side=w_gate)
            bc = _gla_cum_log_decay(gl, w_up, gla_b_gate[l])
            mixed = _gla_recurrence(proj, bc, gla_norm_g[l])
            w_out, w1, w2 = (gla_w_out_bf, l), (gla_w1_bf, l), (gla_w2_bf, l)
        else:
            j = l - n_a
            if kv is None:
                kv = _mod_matmul(x, kv_mod4, 0, 0, 1, w_kv_bf, BF16)
            q = _mod_matmul(x, mod4, l, 0, 1, (sb_w_q_bf, j), BF16, out_scale=dh ** -0.5 * LOG2_E)
            mixed, (w_out, w1, w2) = _sb_attention(q, kv, ((sb_w_out, j), (mlp_w1, l), (mlp_w2, l)))
        x = _proj_residual_ln(mixed, w_out, x, mod4, l, 2, ln_g[l, 0], ln_b[l, 0], alpha)
        x = _mlp_block(x, mod4, l, w1, mlp_b1[l], w2, mlp_b2[l], ln_g[l, 1], ln_b[l, 1], alpha)
    return x
```

```python
import functools

import jax
import jax.numpy as jnp
from jax import lax
from jax.experimental import pallas as pl
from jax.experimental.pallas import tpu as pltpu

F32 = jnp.float32
BF16 = jnp.bfloat16

GLA_HEADS = 4
GLA_GATE_RANK = 16
GLA_GATE_TAU = 16.0
GLA_CHUNK = 64
SB_HEADS = 16
N_MOD = 6
LN_EPS = 1e-5
RMS_EPS = 1e-5

LANES = 128
MOD_ROWS = 8
VMEM_LIMIT_CAP = 58 << 20
VMEM_HEADROOM = 6 << 20


def _params(semantics, block_bytes):
    limit = min(int(block_bytes) + VMEM_HEADROOM, VMEM_LIMIT_CAP)
    return pltpu.CompilerParams(dimension_semantics=semantics, vmem_limit_bytes=limit)


def _tile(n, want):
    t = min(n, want)
    assert n % t == 0, (n, want)
    return t


def _row_slices(n_rows, n_split):
    assert n_split >= 1 and n_rows % n_split == 0
    step = n_rows // n_split
    return [slice(i * step, (i + 1) * step) for i in range(n_split)]


def _weight_shape(w):
    return (w[0] if isinstance(w, tuple) else w).shape[-2:]


def _weight_operand(w, block, index_map, **spec_kwargs):
    if isinstance(w, tuple):
        stacked, entry = w
        return stacked, pl.BlockSpec((None,) + block, lambda *g: (entry,) + index_map(*g), **spec_kwargs)
    return w, pl.BlockSpec(block, index_map, **spec_kwargs)


def _dot(a, b):
    return jnp.dot(a, b, preferred_element_type=F32)


def _dot_nt(a, b):
    return lax.dot_general(a, b, (((1,), (1,)), ((), ())), preferred_element_type=F32)


def _dot_tn(a, b):
    return lax.dot_general(a, b, (((0,), (0,)), ((), ())), preferred_element_type=F32)


def _split_bf16(x):
    hi = x.astype(BF16)
    lo = (x - hi.astype(F32)).astype(BF16)
    return hi, lo


BF16_SUBLANES = 16


def _cast_divides(weights, n_steps):
    return all(w.shape[-2] % n_steps == 0 and (w.shape[-2] // n_steps) % BF16_SUBLANES == 0
               for w, _ in weights)


def _cast_specs(weights, n_steps, step_index):
    in_specs, out_specs, out_shapes, nbytes = [], [], [], 0
    for w, layer in weights:
        R, C = w.shape[-2:]
        rows = R // n_steps
        assert R % n_steps == 0 and rows % BF16_SUBLANES == 0, (w.shape, n_steps)
        if layer is None:
            in_specs.append(pl.BlockSpec((rows, C), lambda *g: (step_index(*g), 0)))
        else:
            in_specs.append(pl.BlockSpec((None, rows, C), lambda *g, _l=layer: (_l, step_index(*g), 0)))
        out_specs.append(pl.BlockSpec((rows, C), lambda *g: (step_index(*g), 0)))
        out_shapes.append(jax.ShapeDtypeStruct((R, C), BF16))
        nbytes += 2 * rows * C * (4 + 2)
    return in_specs, out_specs, out_shapes, nbytes


def _cast_slabs(src_refs, dst_refs):
    for src, dst in zip(src_refs, dst_refs):
        dst[...] = src[...].astype(dst.dtype)


def _layer_norm(z, g, b):
    mu = jnp.mean(z, axis=-1, keepdims=True)
    zc = z - mu
    var = jnp.mean(zc * zc, axis=-1, keepdims=True)
    return zc * lax.rsqrt(var + LN_EPS) * g + b


def _mod_kernel(c_ref, w_ref, b_ref, o_ref):
    c = c_ref[...]
    cs = c * (1.0 / (1.0 + jnp.exp(-c)))
    hi, lo = _split_bf16(cs)
    y = _dot(jnp.concatenate([hi, lo], axis=0), w_ref[...].astype(BF16))
    o_ref[...] = y[:MOD_ROWS] + y[MOD_ROWS:] + b_ref[...]


def _mod_vectors(c_pad, w, b):
    L, D, N = w.shape
    tn = _tile(N, 1024)
    blocks = 2 * (D * tn * 4) + MOD_ROWS * D * 4 + 4 * MOD_ROWS * tn * 4
    return pl.pallas_call(
        _mod_kernel,
        grid=(L, N // tn),
        in_specs=[
            pl.BlockSpec((MOD_ROWS, D), lambda l, j: (0, 0)),
            pl.BlockSpec((None, D, tn), lambda l, j: (l, 0, j)),
            pl.BlockSpec((None, 1, tn), lambda l, j: (l, 0, j)),
        ],
        out_specs=pl.BlockSpec((None, MOD_ROWS, tn), lambda l, j: (l, 0, j)),
        out_shape=jax.ShapeDtypeStruct((L, MOD_ROWS, N), F32),
        compiler_params=_params(("arbitrary", "arbitrary"), blocks),
        name="adaln_vectors",
    )(c_pad, w, b.reshape(L, 1, N))


def _mod_spec(layer, slot, D, n_grid):
    if n_grid == 2:
        return pl.BlockSpec((None, None, 1, D), lambda b, i: (layer, b, 0, slot))
    return pl.BlockSpec((None, None, 1, D), lambda b, i, j: (layer, b, 0, slot))


MODMM_ROW_GROUP = 256
MODMM_MIN_HEADROOM = 2 << 20


def _modmm_kernel(x_ref, sh_ref, sc_ref, w_ref, *rest, out_scale, with_side, n_cast):
    rest = list(rest)
    h_ref = rest.pop()
    ws_ref = rest.pop(0) if with_side else None
    cast_src = [rest.pop(0) for _ in range(n_cast)]
    o_ref = rest.pop(0)
    side_ref = rest.pop(0) if with_side else None
    _cast_slabs(cast_src, rest)
    groups = _row_slices(x_ref.shape[0], x_ref.shape[0] // MODMM_ROW_GROUP)

    def emit(sl):
        y = _dot(h_ref[sl, :], w_ref[...])
        if out_scale != 1.0:
            y = y * out_scale
        o_ref[sl, :] = y.astype(o_ref.dtype)

    @pl.when(pl.program_id(2) == 0)
    def _():
        for sl in groups:
            h_ref[sl, :] = (x_ref[sl, :] * (1.0 + sc_ref[...]) + sh_ref[...]).astype(BF16)
            emit(sl)
            if with_side:
                side_ref[sl, :] = _dot(h_ref[sl, :], ws_ref[...])

    @pl.when(pl.program_id(2) > 0)
    def _():
        for sl in groups:
            emit(sl)


def _mod_matmul(x, mod4, layer, sh_slot, sc_slot, w, out_dtype, out_scale=1.0, n_cols=None, w_side=None,
                cast_weights=()):
    B, T, D = x.shape
    N = _weight_shape(w)[1] if n_cols is None else n_cols
    tm = _tile(T, 1024)
    osz = jnp.dtype(out_dtype).itemsize
    S = 0 if w_side is None else w_side.shape[1]
    n_i = T // tm
    col_tiles = [t for t in (2048, 1536, 1024, 512, 256, 128) if N % t == 0]

    def plan(tn):
        n_j = N // tn
        if not _cast_divides(cast_weights, B * n_i * n_j):
            return None, VMEM_LIMIT_CAP
        c = _cast_specs(cast_weights, B * n_i * n_j, lambda b, i, j: (b * n_i + i) * n_j + j)
        nbytes = (2 * tm * D * 4 + 2 * D * tn * 2 + 2 * tm * tn * osz + tm * D * 2
                  + 2 * MODMM_ROW_GROUP * tn * 4 + 2 * D * S * 2 + 2 * tm * S * 4 + c[3])
        return c, nbytes

    tn = next(t for t in col_tiles if plan(t)[1] + MODMM_MIN_HEADROOM <= VMEM_LIMIT_CAP)
    (c_in, c_out, c_shapes, _), blocks = plan(tn)
    w_arr, w_spec = _weight_operand(w, (D, tn), lambda b, i, j: (0, j))
    in_specs = [
        pl.BlockSpec((None, tm, D), lambda b, i, j: (b, i, 0)),
        _mod_spec(layer, sh_slot, D, 3),
        _mod_spec(layer, sc_slot, D, 3),
        w_spec,
    ]
    out_specs = [pl.BlockSpec((None, tm, tn), lambda b, i, j: (b, i, j))]
    out_shape = [jax.ShapeDtypeStruct((B, T, N), out_dtype)]
    args = [x, mod4, mod4, w_arr]
    if w_side is not None:
        in_specs.append(pl.BlockSpec((D, S), lambda b, i, j: (0, 0)))
        out_specs.append(pl.BlockSpec((None, tm, S), lambda b, i, j: (b, i, 0)))
        out_shape.append(jax.ShapeDtypeStruct((B, T, S), F32))
        args.append(w_side)
    outs = pl.pallas_call(
        functools.partial(_modmm_kernel, out_scale=out_scale, with_side=w_side is not None,
                          n_cast=len(cast_weights)),
        grid=(B, n_i, N // tn),
        in_specs=in_specs + c_in,
        out_specs=out_specs + c_out,
        out_shape=out_shape + c_shapes,
        scratch_shapes=[pltpu.VMEM((tm, D), BF16)],
        compiler_params=_params(("arbitrary", "arbitrary", "arbitrary"), blocks),
        name="modulated_matmul",
    )(*args, *[cw for cw, _ in cast_weights])
    n_main = 1 if w_side is None else 2
    return outs[0], (None if w_side is None else outs[1]), tuple(outs[n_main:])


def _gla_gate_kernel(gl_ref, wup_ref, bg_ref, tri_ref, o_ref):
    g = _dot(gl_ref[...].astype(BF16), wup_ref[...]) + bg_ref[...]
    log_a = (jnp.minimum(g, 0.0) - jnp.log(1.0 + jnp.exp(-jnp.abs(g)))) * (1.0 / GLA_GATE_TAU)
    hi, lo = _split_bf16(log_a)
    o_ref[...] = _dot(tri_ref[...], jnp.concatenate([hi, lo], axis=0))


def _chunk_cumsum_matrix(tm):
    row = lax.broadcasted_iota(jnp.int32, (tm, tm), 0)
    col = lax.broadcasted_iota(jnp.int32, (tm, tm), 1)
    tri = ((col <= row) & (row // GLA_CHUNK == col // GLA_CHUNK)).astype(BF16)
    return jnp.concatenate([tri, tri], axis=1)


def _gla_cum_log_decay(gl, w_up, b_gate):
    B, T, R = gl.shape
    QK = w_up.shape[1]
    tm = _tile(T, 512)
    assert tm % GLA_CHUNK == 0
    blocks = 2 * tm * R * 4 + 2 * R * QK * 2 + 2 * tm * 2 * tm * 2 + 8 * tm * QK * 4
    return pl.pallas_call(
        _gla_gate_kernel,
        grid=(B, T // tm),
        in_specs=[
            pl.BlockSpec((None, tm, R), lambda b, i: (b, i, 0)),
            pl.BlockSpec((R, QK), lambda b, i: (0, 0)),
            pl.BlockSpec((1, QK), lambda b, i: (0, 0)),
            pl.BlockSpec((tm, 2 * tm), lambda b, i: (0, 0)),
        ],
        out_specs=pl.BlockSpec((None, tm, QK), lambda b, i: (b, i, 0)),
        out_shape=jax.ShapeDtypeStruct((B, T, QK), F32),
        compiler_params=_params(("arbitrary", "arbitrary"), blocks),
        name="gla_cum_log_decay",
    )(gl, w_up, b_gate.reshape(1, QK), _chunk_cumsum_matrix(tm))


GLA_HEADS_PER_STEP = 2


def _gla_rec_kernel(q_ref, k_ref, v_ref, r_ref, bc_ref, ng_ref, o_ref, st_ref, *, n_chunks, dk, dv):
    C = GLA_CHUNK

    @pl.when(pl.program_id(2) == 0)
    def _():
        st_ref[...] = jnp.zeros_like(st_ref)

    crow = lax.broadcasted_iota(jnp.int32, (C, C), 0)
    ccol = lax.broadcasted_iota(jnp.int32, (C, C), 1)
    causal = ccol <= crow
    ng = ng_ref[...]

    bc, k, q_dec, k_intra = [], [], [], []
    for h in range(GLA_HEADS_PER_STEP):
        ks = slice(h * dk, (h + 1) * dk)
        bc.append(bc_ref[:, ks])
        k.append(k_ref[:, ks])
        q_dec.append((q_ref[:, ks] * (dk ** -0.5) * jnp.exp(bc[h])).astype(BF16))
        k_intra.append((k[h] * jnp.exp(-bc[h])).astype(BF16))

    for c in range(n_chunks):
        sl = slice(c * C, (c + 1) * C)
        for h in range(GLA_HEADS_PER_STEP):
            vs = slice(h * dv, (h + 1) * dv)
            b_last = bc[h][(c + 1) * C - 1:(c + 1) * C, :]
            k_state = (k[h][sl] * jnp.exp(b_last - bc[h][sl])).astype(BF16)
            decay = jnp.exp(b_last)
            v_c = v_ref[sl, vs].astype(BF16)
            scores = jnp.where(causal, _dot_nt(q_dec[h][sl], k_intra[h][sl]), 0.0)
            st = st_ref[h]
            o = _dot(scores.astype(BF16), v_c) + _dot_nt(q_dec[h][sl], st.astype(BF16))
            st_ref[h] = st * decay + _dot_tn(v_c, k_state)
            o = o * lax.rsqrt(jnp.mean(o * o, axis=-1, keepdims=True) + RMS_EPS) * ng
            r = r_ref[sl, vs]
            o_ref[sl, vs] = (o * (r * (1.0 / (1.0 + jnp.exp(-r))))).astype(o_ref.dtype)


def _gla_recurrence(proj, bc, norm_g):
    B, T, _ = proj.shape
    QK = bc.shape[-1]
    dk = QK // GLA_HEADS
    dv = norm_g.shape[-1]
    V = dv * GLA_HEADS
    hp = GLA_HEADS_PER_STEP
    assert dk % LANES == 0 and dv % LANES == 0 and GLA_HEADS % hp == 0
    wk, wv = hp * dk, hp * dv
    assert QK % wk == 0 and (2 * QK) % wv == 0 and (2 * QK + V) % wv == 0
    n_chunks = min(8, T // GLA_CHUNK)
    tb = n_chunks * GLA_CHUNK
    assert T % tb == 0
    kk = QK // wk
    kv, kr = 2 * QK // wv, (2 * QK + V) // wv
    blocks = (2 * (3 * tb * wk * 4 + 2 * tb * wv * 4 + tb * wv * 2) + hp * dv * dk * 4 + 8 * tb * wk * 4)
    return pl.pallas_call(
        functools.partial(_gla_rec_kernel, n_chunks=n_chunks, dk=dk, dv=dv),
        grid=(B, GLA_HEADS // hp, T // tb),
        in_specs=[
            pl.BlockSpec((None, tb, wk), lambda b, h, t: (b, t, h)),
            pl.BlockSpec((None, tb, wk), lambda b, h, t: (b, t, kk + h)),
            pl.BlockSpec((None, tb, wv), lambda b, h, t: (b, t, kv + h)),
            pl.BlockSpec((None, tb, wv), lambda b, h, t: (b, t, kr + h)),
            pl.BlockSpec((None, tb, wk), lambda b, h, t: (b, t, h)),
            pl.BlockSpec((1, dv), lambda b, h, t: (0, 0)),
        ],
        out_specs=pl.BlockSpec((None, tb, wv), lambda b, h, t: (b, t, h)),
        out_shape=jax.ShapeDtypeStruct((B, T, V), BF16),
        scratch_shapes=[pltpu.VMEM((hp, dv, dk), F32)],
        compiler_params=_params(("arbitrary", "arbitrary", "arbitrary"), blocks),
        name="gla_recurrence",
    )(proj, proj, proj, proj, bc, norm_g.reshape(1, dv))


def _proj_ln_kernel(a_ref, w_ref, x_ref, g_ref, lg_ref, lb_ref, o_ref, *, alpha):
    groups = _row_slices(a_ref.shape[0], 4)

    def finish(sl, y):
        z = alpha * x_ref[sl, :] + (1.0 + g_ref[...]) * y
        o_ref[sl, :] = _layer_norm(z, lg_ref[...], lb_ref[...])

    pending = None
    for sl in groups:
        y = _dot(a_ref[sl, :], w_ref[...])
        if pending is not None:
            finish(*pending)
        pending = (sl, y)
    finish(*pending)


def _proj_residual_ln(a, w, x, mod4, layer, gate_slot, ln_g, ln_b, alpha):
    B, T, D = x.shape
    K = a.shape[-1]
    tm = _tile(T, 1024)
    blocks = 2 * tm * K * 2 + K * D * 2 + 4 * tm * D * 4 + 3 * (tm // 4) * D * 4
    w, w_spec = _weight_operand(w, (K, D), lambda b, i: (0, 0), pipeline_mode=pl.Buffered(1))
    return pl.pallas_call(
        functools.partial(_proj_ln_kernel, alpha=alpha),
        grid=(B, T // tm),
        in_specs=[
            pl.BlockSpec((None, tm, K), lambda b, i: (b, i, 0)),
            w_spec,
            pl.BlockSpec((None, tm, D), lambda b, i: (b, i, 0)),
            _mod_spec(layer, gate_slot, D, 2),
            pl.BlockSpec((1, D), lambda b, i: (0, 0)),
            pl.BlockSpec((1, D), lambda b, i: (0, 0)),
        ],
        out_specs=pl.BlockSpec((None, tm, D), lambda b, i: (b, i, 0)),
        out_shape=jax.ShapeDtypeStruct((B, T, D), F32),
        compiler_params=_params(("arbitrary", "arbitrary"), blocks),
        name="proj_residual_ln",
    )(a, w, x, mod4, ln_g.reshape(1, D), ln_b.reshape(1, D))


MLP_TM = 1024
MLP_TF = 1024
MLP_ROW_GROUP = 256

def _mlp_kernel(x_ref, sh_ref, sc_ref, g_ref, w1_ref, b1_ref, w2_ref, b2_ref, lg_ref, lb_ref,
                o_ref, h_ref, *, alpha):
    f = pl.program_id(2)
    last = pl.num_programs(2) - 1
    groups = _row_slices(x_ref.shape[0], x_ref.shape[0] // MLP_ROW_GROUP)

    def contribution(sl):
        u = jnp.maximum(_dot(h_ref[sl, :], w1_ref[...]) + b1_ref[...], 0.0)
        return _dot((u * u).astype(BF16), w2_ref[...])

    @pl.when(f == 0)
    def _():
        for sl in groups:
            h_ref[sl, :] = (x_ref[sl, :] * (1.0 + sc_ref[...]) + sh_ref[...]).astype(BF16)
            o_ref[sl, :] = contribution(sl)

    @pl.when(jnp.logical_and(f > 0, f < last))
    def _():
        for sl in groups:
            o_ref[sl, :] += contribution(sl)

    @pl.when(f == last)
    def _():
        for sl in groups:
            y = o_ref[sl, :] + contribution(sl) + b2_ref[...]
            z = alpha * x_ref[sl, :] + (1.0 + g_ref[...]) * y
            o_ref[sl, :] = _layer_norm(z, lg_ref[...], lb_ref[...])


def _mlp_block(x, mod4, layer, w1, b1, w2, b2, ln_g, ln_b, alpha):
    B, T, D = x.shape
    F = _weight_shape(w1)[1]
    tm, tf = _tile(T, MLP_TM), _tile(F, MLP_TF)
    assert F // tf >= 2, "the first and the last d_ff step must be distinct grid steps"
    assert tm % MLP_ROW_GROUP == 0
    blocks = (4 * tm * D * 4 + tm * D * 2 + 4 * D * tf * 2 + tm * tf * 6 + 3 * MLP_ROW_GROUP * D * 4)
    w1, w1_spec = _weight_operand(w1, (D, tf), lambda b, i, f: (0, f))
    w2, w2_spec = _weight_operand(w2, (tf, D), lambda b, i, f: (f, 0))
    return pl.pallas_call(
        functools.partial(_mlp_kernel, alpha=alpha),
        grid=(B, T // tm, F // tf),
        in_specs=[
            pl.BlockSpec((None, tm, D), lambda b, i, f: (b, i, 0)),
            _mod_spec(layer, 3, D, 3),
            _mod_spec(layer, 4, D, 3),
            _mod_spec(layer, 5, D, 3),
            w1_spec,
            pl.BlockSpec((1, tf), lambda b, i, f: (0, f)),
            w2_spec,
            pl.BlockSpec((1, D), lambda b, i, f: (0, 0)),
            pl.BlockSpec((1, D), lambda b, i, f: (0, 0)),
            pl.BlockSpec((1, D), lambda b, i, f: (0, 0)),
        ],
        out_specs=pl.BlockSpec((None, tm, D), lambda b, i, f: (b, i, 0)),
        out_shape=jax.ShapeDtypeStruct((B, T, D), F32),
        scratch_shapes=[pltpu.VMEM((tm, D), BF16)],
        compiler_params=_params(("arbitrary", "arbitrary", "arbitrary"), blocks),
        name="mlp_block",
    )(x, mod4, mod4, mod4, w1, b1.reshape(1, F), w2, b2.reshape(1, D), ln_g.reshape(1, D), ln_b.reshape(1, D))


SB_BLK = 256
LOG2_E = 1.4426950408889634
LN_2 = 0.6931471805599453
SB_DEAD_LOG2 = 160.0
SB_NO_KEYS = 1e30


def _sb_attn_kernel(q_ref, k_ref, v_ref, *rest, n_pairs, n_cast):
    o_ref = rest[n_cast]
    _cast_slabs(rest[:n_cast], rest[n_cast + 1:])
    _sb_attn_body(q_ref, k_ref, v_ref, o_ref, n_pairs)


def _sb_attn_body(q_ref, k_ref, v_ref, o_ref, n_pairs):
    n = SB_BLK
    dh = q_ref.shape[-1]
    jrow = lax.broadcasted_iota(jnp.int32, (n, n), 0)
    jcol = lax.broadcasted_iota(jnp.int32, (n, n), 1)
    later = (jrow > jcol).astype(BF16)
    diag_mask = jcol < jrow

    def scores(q, blk):
        return _dot_nt(q, k_ref[pl.ds(pl.multiple_of(blk * n, n), n), :])

    def log_terms(z, mask):
        neg_abs = lax.bitcast_convert_type(
            lax.bitcast_convert_type(z, jnp.uint32) | jnp.uint32(0x80000000), F32)
        sp = jnp.maximum(z, 0.0) + jnp.log(1.0 + jnp.exp2(neg_abs)) * LOG2_E
        lb = z - sp
        if mask is not None:
            sp = jnp.where(mask, sp, 0.0)
        return sp.astype(BF16), lb, sp[:, 0:1]

    def suffix(hilo):
        return _dot(hilo, later)

    def weights(lb, sfx, r, sp0, mask):
        a = jnp.exp2(lb - sfx if r is None else lb - (sfx + r))
        if mask is not None:
            a = jnp.where(mask, a, 0.0)
        tot = sfx[:, 0:1] + sp0
        return a.astype(BF16), tot if r is None else r + tot

    def values(a, blk):
        return _dot(a, v_ref[pl.ds(pl.multiple_of(blk * n, n), n), :])

    def step(qa, qb, blk_a, blk_b, ra, rb, acc_a, acc_b, mask):
        za = scores(qa, blk_a)
        zb = scores(qb, blk_b)
        hla, lba, spa0 = log_terms(za, mask)
        sfa = suffix(hla)
        hlb, lbb, spb0 = log_terms(zb, mask)
        aa, ra = weights(lba, sfa, ra, spa0, mask)
        sfb = suffix(hlb)
        acc_a = acc_a + values(aa, blk_a)
        ab, rb = weights(lbb, sfb, rb, spb0, mask)
        acc_b = acc_b + values(ab, blk_b)
        return ra, rb, acc_a, acc_b

    def pair_body(p, _):
        blk_a0 = 2 * p
        blk_b0 = blk_a0 + 1
        row_a = pl.multiple_of(blk_a0 * n, n)
        row_b = pl.multiple_of(blk_b0 * n, n)
        qa = q_ref[pl.ds(row_a, n), :]
        qb = q_ref[pl.ds(row_b, n), :]

        blk_a1 = jnp.maximum(blk_a0 - 1, 0)
        blk_b1 = blk_a0
        za0 = scores(qa, blk_a0)
        zb0 = scores(qb, blk_b0)
        za1 = scores(qa, blk_a1)
        zb1 = scores(qb, blk_b1)
        hla0, lba0, spa0 = log_terms(za0, diag_mask)
        sfa0 = suffix(hla0)
        hlb0, lbb0, spb0 = log_terms(zb0, diag_mask)
        sfb0 = suffix(hlb0)
        hla1, lba1, spa1 = log_terms(za1, None)
        aa0, ra = weights(lba0, sfa0, None, spa0, diag_mask)
        sfa1 = suffix(hla1)
        acc_a = values(aa0, blk_a0)
        hlb1, lbb1, spb1 = log_terms(zb1, None)
        ab0, rb = weights(lbb0, sfb0, None, spb0, diag_mask)
        sfb1 = suffix(hlb1)
        acc_b = values(ab0, blk_b0)
        ra = jnp.where(p > 0, ra, SB_NO_KEYS)
        aa1, ra = weights(lba1, sfa1, ra, spa1, None)
        acc_a = acc_a + values(aa1, blk_a1)
        ab1, rb = weights(lbb1, sfb1, rb, spb1, None)
        acc_b = acc_b + values(ab1, blk_b1)

        def alive(ra, rb):
            return jnp.minimum(jnp.min(ra), jnp.min(rb)) < SB_DEAD_LOG2

        def cond(carry):
            i, go = carry[0], carry[1]
            return jnp.logical_and(i <= blk_b0, go)

        def body(carry):
            i, _, ra, rb, acc_a, acc_b = carry
            has_a = i <= blk_a0
            ra = jnp.where(has_a, ra, SB_NO_KEYS)
            ra, rb, acc_a, acc_b = step(qa, qb, jnp.maximum(blk_a0 - i, 0), blk_b0 - i,
                                        ra, rb, acc_a, acc_b, None)
            return i + 1, alive(ra, rb), ra, rb, acc_a, acc_b

        carry = lax.while_loop(cond, body, (jnp.int32(2), alive(ra, rb), ra, rb, acc_a, acc_b))
        o_ref[pl.ds(row_a, n), :] = carry[4].astype(o_ref.dtype)
        o_ref[pl.ds(row_b, n), :] = carry[5].astype(o_ref.dtype)
        return 0

    lax.fori_loop(0, n_pairs, pair_body, 0)


def _sb_attention(q, kv, cast_weights=()):
    B, T, HD = q.shape
    dh = HD // SB_HEADS
    assert dh % LANES == 0 and T % (2 * SB_BLK) == 0
    c_in, c_out, c_shapes, c_bytes = _cast_specs(cast_weights, B * SB_HEADS, lambda b, h: b * SB_HEADS + h)
    blocks = 8 * T * dh * 2 + 24 * SB_BLK * SB_BLK * 4 + c_bytes
    outs = pl.pallas_call(
        functools.partial(_sb_attn_kernel, n_pairs=T // (2 * SB_BLK), n_cast=len(cast_weights)),
        grid=(B, SB_HEADS),
        in_specs=[
            pl.BlockSpec((None, T, dh), lambda b, h: (b, 0, h)),
            pl.BlockSpec((None, T, dh), lambda b, h: (b, 0, h)),
            pl.BlockSpec((None, T, dh), lambda b, h: (b, 0, SB_HEADS + h)),
        ] + c_in,
        out_specs=[pl.BlockSpec((None, T, dh), lambda b, h: (b, 0, h))] + c_out,
        out_shape=[jax.ShapeDtypeStruct((B, T, HD), BF16)] + c_shapes,
        compiler_params=_params(("arbitrary", "arbitrary"), blocks),
        name="stick_breaking_attention",
    )(q, kv, kv, *[w for w, _ in cast_weights])
    return outs[0], tuple(outs[1:])


def kernel(x, c, ada_w, ada_b, ln_g, ln_b, gla_w_in, gla_w_gate_up, gla_b_gate, gla_norm_g, gla_w_out, kv_ada_w, kv_ada_b, w_kv, sb_w_q, sb_w_out, mlp_w1, mlp_b1, mlp_w2, mlp_b2):
    B, T, D = x.shape
    depth = ada_w.shape[0]
    n_a = gla_w_in.shape[0]
    alpha = (2.0 * depth) ** 0.25
    QK = gla_w_gate_up.shape[-1]
    V = gla_w_out.shape[1]
    n_main = 2 * QK + 2 * V
    dh = sb_w_q.shape[-1] // SB_HEADS

    c_pad = jnp.pad(c, ((0, MOD_ROWS - B), (0, 0)))
    mod4 = _mod_vectors(c_pad, ada_w, ada_b).reshape(depth, MOD_ROWS, 1, N_MOD * D)
    kv_mod4 = _mod_vectors(c_pad, kv_ada_w[None], kv_ada_b[None]).reshape(1, MOD_ROWS, 1, 2 * D)

    gla_w_in_bf = gla_w_in.astype(BF16)
    gla_w_out_bf = gla_w_out.astype(BF16)
    w_kv_bf = w_kv.astype(BF16)
    sb_w_q_bf = sb_w_q.astype(BF16)

    kv = None
    for l in range(depth):
        if l < n_a:
            pad = LANES - GLA_GATE_RANK
            w_gate = jnp.pad(gla_w_in_bf[l, :, n_main:], ((0, 0), (0, pad)))
            w_up = jnp.pad(gla_w_gate_up[l], ((0, pad), (0, 0))).astype(BF16)
            proj, gl, (w1, w2) = _mod_matmul(
                x, mod4, l, 0, 1, (gla_w_in_bf, l), F32, n_cols=n_main, w_side=w_gate,
                cast_weights=((mlp_w1, l), (mlp_w2, l)))
            w_out = (gla_w_out_bf, l)
            bc = _gla_cum_log_decay(gl, w_up, gla_b_gate[l])
            mixed = _gla_recurrence(proj, bc, gla_norm_g[l])
        else:
            j = l - n_a
            if kv is None:
                kv, _, _ = _mod_matmul(x, kv_mod4, 0, 0, 1, w_kv_bf, BF16)
            q, _, _ = _mod_matmul(x, mod4, l, 0, 1, (sb_w_q_bf, j), BF16, out_scale=dh ** -0.5 * LOG2_E)
            mixed, (w_out, w1, w2) = _sb_attention(q, kv, ((sb_w_out, j), (mlp_w1, l), (mlp_w2, l)))
        x = _proj_residual_ln(mixed, w_out, x, mod4, l, 2, ln_g[l, 0], ln_b[l, 0], alpha)
        x = _mlp_block(x, mod4, l, w1, mlp_b1[l], w2, mlp_b2[l], ln_g[l, 1], ln_b[l, 1], alpha)
    return x
```

```python
import functools

import jax
import jax.numpy as jnp
from jax import lax
from jax.experimental import pallas as pl
from jax.experimental.pallas import tpu as pltpu

F32 = jnp.float32
BF16 = jnp.bfloat16

GLA_HEADS = 4
GLA_GATE_RANK = 16
GLA_GATE_TAU = 16.0
GLA_CHUNK = 64
SB_HEADS = 16
N_MOD = 6
LN_EPS = 1e-5
RMS_EPS = 1e-5

LANES = 128
MOD_ROWS = 8
VMEM_LIMIT_CAP = 58 << 20
VMEM_HEADROOM = 6 << 20


def _params(semantics, block_bytes):
    limit = min(int(block_bytes) + VMEM_HEADROOM, VMEM_LIMIT_CAP)
    return pltpu.CompilerParams(dimension_semantics=semantics, vmem_limit_bytes=limit)


def _tile(n, want):
    t = min(n, want)
    assert n % t == 0, (n, want)
    return t


def _row_slices(n_rows, n_split):
    assert n_split >= 1 and n_rows % n_split == 0
    step = n_rows // n_split
    return [slice(i * step, (i + 1) * step) for i in range(n_split)]


def _weight_shape(w):
    return (w[0] if isinstance(w, tuple) else w).shape[-2:]


def _weight_operand(w, block, index_map, **spec_kwargs):
    if isinstance(w, tuple):
        stacked, entry = w
        return stacked, pl.BlockSpec((None,) + block, lambda *g: (entry,) + index_map(*g), **spec_kwargs)
    return w, pl.BlockSpec(block, index_map, **spec_kwargs)


def _dot(a, b):
    return jnp.dot(a, b, preferred_element_type=F32)


def _dot_nt(a, b):
    return lax.dot_general(a, b, (((1,), (1,)), ((), ())), preferred_element_type=F32)


def _dot_tn(a, b):
    return lax.dot_general(a, b, (((0,), (0,)), ((), ())), preferred_element_type=F32)


def _split_bf16(x):
    hi = x.astype(BF16)
    lo = (x - hi.astype(F32)).astype(BF16)
    return hi, lo


BF16_SUBLANES = 16


def _cast_divides(weights, n_steps):
    return all(w.shape[-2] % n_steps == 0 and (w.shape[-2] // n_steps) % BF16_SUBLANES == 0
               for w, _ in weights)


def _cast_specs(weights, n_steps, step_index):
    in_specs, out_specs, out_shapes, nbytes = [], [], [], 0
    for w, layer in weights:
        R, C = w.shape[-2:]
        rows = R // n_steps
        assert R % n_steps == 0 and rows % BF16_SUBLANES == 0, (w.shape, n_steps)
        if layer is None:
            in_specs.append(pl.BlockSpec((rows, C), lambda *g: (step_index(*g), 0)))
        else:
            in_specs.append(pl.BlockSpec((None, rows, C), lambda *g, _l=layer: (_l, step_index(*g), 0)))
        out_specs.append(pl.BlockSpec((rows, C), lambda *g: (step_index(*g), 0)))
        out_shapes.append(jax.ShapeDtypeStruct((R, C), BF16))
        nbytes += 2 * rows * C * (4 + 2)
    return in_specs, out_specs, out_shapes, nbytes


def _cast_slabs(src_refs, dst_refs):
    for src, dst in zip(src_refs, dst_refs):
        dst[...] = src[...].astype(dst.dtype)


def _layer_norm(z, g, b):
    mu = jnp.mean(z, axis=-1, keepdims=True)
    zc = z - mu
    var = jnp.mean(zc * zc, axis=-1, keepdims=True)
    return zc * lax.rsqrt(var + LN_EPS) * g + b


def _mod_kernel(c_ref, w_ref, b_ref, o_ref):
    c = c_ref[...]
    cs = c * (1.0 / (1.0 + jnp.exp(-c)))
    hi, lo = _split_bf16(cs)
    y = _dot(jnp.concatenate([hi, lo], axis=0), w_ref[...].astype(BF16))
    o_ref[...] = y[:MOD_ROWS] + y[MOD_ROWS:] + b_ref[...]


def _mod_vectors(c_pad, w, b):
    L, D, N = w.shape
    tn = _tile(N, 1024)
    blocks = 2 * (D * tn * 4) + MOD_ROWS * D * 4 + 4 * MOD_ROWS * tn * 4
    return pl.pallas_call(
        _mod_kernel,
        grid=(L, N // tn),
        in_specs=[
            pl.BlockSpec((MOD_ROWS, D), lambda l, j: (0, 0)),
            pl.BlockSpec((None, D, tn), lambda l, j: (l, 0, j)),
            pl.BlockSpec((None, 1, tn), lambda l, j: (l, 0, j)),
        ],
        out_specs=pl.BlockSpec((None, MOD_ROWS, tn), lambda l, j: (l, 0, j)),
        out_shape=jax.ShapeDtypeStruct((L, MOD_ROWS, N), F32),
        compiler_params=_params(("arbitrary", "arbitrary"), blocks),
        name="adaln_vectors",
    )(c_pad, w, b.reshape(L, 1, N))


def _mod_spec(layer, slot, D, n_grid):
    if n_grid == 2:
        return pl.BlockSpec((None, None, 1, D), lambda b, i: (layer, b, 0, slot))
    return pl.BlockSpec((None, None, 1, D), lambda b, i, j: (layer, b, 0, slot))


MODMM_ROW_GROUP = 256
MODMM_MIN_HEADROOM = 2 << 20


def _modmm_kernel(x_ref, sh_ref, sc_ref, w_ref, *rest, out_scale, with_side, n_cast):
    rest = list(rest)
    h_ref = rest.pop()
    ws_ref = rest.pop(0) if with_side else None
    cast_src = [rest.pop(0) for _ in range(n_cast)]
    o_ref = rest.pop(0)
    side_ref = rest.pop(0) if with_side else None
    _cast_slabs(cast_src, rest)
    groups = _row_slices(x_ref.shape[0], x_ref.shape[0] // MODMM_ROW_GROUP)

    def emit(sl):
        y = _dot(h_ref[sl, :], w_ref[...])
        if out_scale != 1.0:
            y = y * out_scale
        o_ref[sl, :] = y.astype(o_ref.dtype)

    @pl.when(pl.program_id(2) == 0)
    def _():
        for sl in groups:
            h_ref[sl, :] = (x_ref[sl, :] * (1.0 + sc_ref[...]) + sh_ref[...]).astype(BF16)
            emit(sl)
            if with_side:
                side_ref[sl, :] = _dot(h_ref[sl, :], ws_ref[...])

    @pl.when(pl.program_id(2) > 0)
    def _():
        for sl in groups:
            emit(sl)


def _mod_matmul(x, mod4, layer, sh_slot, sc_slot, w, out_dtype, out_scale=1.0, n_cols=None, w_side=None,
                cast_weights=()):
    B, T, D = x.shape
    N = _weight_shape(w)[1] if n_cols is None else n_cols
    tm = _tile(T, 1024)
    osz = jnp.dtype(out_dtype).itemsize
    S = 0 if w_side is None else w_side.shape[1]
    n_i = T // tm
    col_tiles = [t for t in (2048, 1536, 1024, 512, 256, 128) if N % t == 0]

    def plan(tn):
        n_j = N // tn
        if not _cast_divides(cast_weights, B * n_i * n_j):
            return None, VMEM_LIMIT_CAP
        c = _cast_specs(cast_weights, B * n_i * n_j, lambda b, i, j: (b * n_i + i) * n_j + j)
        nbytes = (2 * tm * D * 4 + 2 * D * tn * 2 + 2 * tm * tn * osz + tm * D * 2
                  + 2 * MODMM_ROW_GROUP * tn * 4 + 2 * D * S * 2 + 2 * tm * S * 4 + c[3])
        return c, nbytes

    tn = next(t for t in col_tiles if plan(t)[1] + MODMM_MIN_HEADROOM <= VMEM_LIMIT_CAP)
    (c_in, c_out, c_shapes, _), blocks = plan(tn)
    w_arr, w_spec = _weight_operand(w, (D, tn), lambda b, i, j: (0, j))
    in_specs = [
        pl.BlockSpec((None, tm, D), lambda b, i, j: (b, i, 0)),
        _mod_spec(layer, sh_slot, D, 3),
        _mod_spec(layer, sc_slot, D, 3),
        w_spec,
    ]
    out_specs = [pl.BlockSpec((None, tm, tn), lambda b, i, j: (b, i, j))]
    out_shape = [jax.ShapeDtypeStruct((B, T, N), out_dtype)]
    args = [x, mod4, mod4, w_arr]
    if w_side is not None:
        in_specs.append(pl.BlockSpec((D, S), lambda b, i, j: (0, 0)))
        out_specs.append(pl.BlockSpec((None, tm, S), lambda b, i, j: (b, i, 0)))
        out_shape.append(jax.ShapeDtypeStruct((B, T, S), F32))
        args.append(w_side)
    outs = pl.pallas_call(
        functools.partial(_modmm_kernel, out_scale=out_scale, with_side=w_side is not None,
                          n_cast=len(cast_weights)),
        grid=(B, n_i, N // tn),
        in_specs=in_specs + c_in,
        out_specs=out_specs + c_out,
        out_shape=out_shape + c_shapes,
        scratch_shapes=[pltpu.VMEM((tm, D), BF16)],
        compiler_params=_params(("arbitrary", "arbitrary", "arbitrary"), blocks),
        name="modulated_matmul",
    )(*args, *[cw for cw, _ in cast_weights])
    n_main = 1 if w_side is None else 2
    return outs[0], (None if w_side is None else outs[1]), tuple(outs[n_main:])


def _gla_gate_kernel(gl_ref, wup_ref, bg_ref, tri_ref, o_ref):
    g = _dot(gl_ref[...].astype(BF16), wup_ref[...]) + bg_ref[...]
    log_a = (jnp.minimum(g, 0.0) - jnp.log(1.0 + jnp.exp(-jnp.abs(g)))) * (1.0 / GLA_GATE_TAU)
    hi, lo = _split_bf16(log_a)
    o_ref[...] = _dot(tri_ref[...], jnp.concatenate([hi, lo], axis=0))


def _chunk_cumsum_matrix(tm):
    row = lax.broadcasted_iota(jnp.int32, (tm, tm), 0)
    col = lax.broadcasted_iota(jnp.int32, (tm, tm), 1)
    tri = ((col <= row) & (row // GLA_CHUNK == col // GLA_CHUNK)).astype(BF16)
    return jnp.concatenate([tri, tri], axis=1)


def _gla_cum_log_decay(gl, w_up, b_gate):
    B, T, R = gl.shape
    QK = w_up.shape[1]
    tm = _tile(T, 512)
    assert tm % GLA_CHUNK == 0
    blocks = 2 * tm * R * 4 + 2 * R * QK * 2 + 2 * tm * 2 * tm * 2 + 8 * tm * QK * 4
    return pl.pallas_call(
        _gla_gate_kernel,
        grid=(B, T // tm),
        in_specs=[
            pl.BlockSpec((None, tm, R), lambda b, i: (b, i, 0)),
            pl.BlockSpec((R, QK), lambda b, i: (0, 0)),
            pl.BlockSpec((1, QK), lambda b, i: (0, 0)),
            pl.BlockSpec((tm, 2 * tm), lambda b, i: (0, 0)),
        ],
        out_specs=pl.BlockSpec((None, tm, QK), lambda b, i: (b, i, 0)),
        out_shape=jax.ShapeDtypeStruct((B, T, QK), F32),
        compiler_params=_params(("arbitrary", "arbitrary"), blocks),
        name="gla_cum_log_decay",
    )(gl, w_up, b_gate.reshape(1, QK), _chunk_cumsum_matrix(tm))


GLA_HEADS_PER_STEP = 2


def _gla_rec_kernel(q_ref, k_ref, v_ref, r_ref, bc_ref, ng_ref, o_ref, st_ref, *, n_chunks, dk, dv):
    C = GLA_CHUNK

    @pl.when(pl.program_id(2) == 0)
    def _():
        st_ref[...] = jnp.zeros_like(st_ref)

    crow = lax.broadcasted_iota(jnp.int32, (C, C), 0)
    ccol = lax.broadcasted_iota(jnp.int32, (C, C), 1)
    causal = ccol <= crow
    prow = lax.broadcasted_iota(jnp.int32, (C, 2 * C), 0)
    pcol = lax.broadcasted_iota(jnp.int32, (C, 2 * C), 1)
    causal_second = pcol <= prow + C
    ng = ng_ref[...]

    bc, k, q_dec, k_intra = [], [], [], []
    for h in range(GLA_HEADS_PER_STEP):
        ks = slice(h * dk, (h + 1) * dk)
        bc.append(bc_ref[:, ks])
        k.append(k_ref[:, ks])
        q_dec.append(q_ref[:, ks] * (dk ** -0.5) * jnp.exp(bc[h]))
        k_intra.append((k[h] * jnp.exp(-bc[h])).astype(BF16))

    for g in range(n_chunks // 2):
        c1 = slice(2 * g * C, (2 * g + 1) * C)
        c2 = slice((2 * g + 1) * C, (2 * g + 2) * C)
        both = slice(2 * g * C, (2 * g + 2) * C)
        for h in range(GLA_HEADS_PER_STEP):
            vs = slice(h * dv, (h + 1) * dv)
            b1 = bc[h][(2 * g + 1) * C - 1:(2 * g + 1) * C, :]
            b2 = bc[h][(2 * g + 2) * C - 1:(2 * g + 2) * C, :]
            k1_mid = (k[h][c1] * jnp.exp(b1 - bc[h][c1])).astype(BF16)
            k1_end = k[h][c1] * jnp.exp((b1 + b2) - bc[h][c1])
            k2_end = k[h][c2] * jnp.exp(b2 - bc[h][c2])
            k_end = jnp.concatenate([k1_end, k2_end], axis=0).astype(BF16)
            q1 = q_dec[h][c1].astype(BF16)
            q2 = q_dec[h][c2].astype(BF16)
            q2_from_start = (q_dec[h][c2] * jnp.exp(b1)).astype(BF16)
            v_both = v_ref[both, vs].astype(BF16)
            s1 = jnp.where(causal, _dot_nt(q1, k_intra[h][c1]), 0.0)
            s2 = jnp.where(causal_second,
                           _dot_nt(q2, jnp.concatenate([k1_mid, k_intra[h][c2]], axis=0)), 0.0)
            st = st_ref[h]
            st_bf = st.astype(BF16)
            o1 = _dot(s1.astype(BF16), v_both[:C]) + _dot_nt(q1, st_bf)
            o2 = _dot(s2.astype(BF16), v_both) + _dot_nt(q2_from_start, st_bf)
            st_ref[h] = st * jnp.exp(b1 + b2) + _dot_tn(v_both, k_end)
            for sl, o in ((c1, o1), (c2, o2)):
                o = o * lax.rsqrt(jnp.mean(o * o, axis=-1, keepdims=True) + RMS_EPS) * ng
                r = r_ref[sl, vs]
                o_ref[sl, vs] = (o * (r * (1.0 / (1.0 + jnp.exp(-r))))).astype(o_ref.dtype)


def _gla_recurrence(proj, bc, norm_g):
    B, T, _ = proj.shape
    QK = bc.shape[-1]
    dk = QK // GLA_HEADS
    dv = norm_g.shape[-1]
    V = dv * GLA_HEADS
    hp = GLA_HEADS_PER_STEP
    assert dk % LANES == 0 and dv % LANES == 0 and GLA_HEADS % hp == 0
    wk, wv = hp * dk, hp * dv
    assert QK % wk == 0 and (2 * QK) % wv == 0 and (2 * QK + V) % wv == 0
    n_chunks = min(8, T // GLA_CHUNK)
    tb = n_chunks * GLA_CHUNK
    assert T % tb == 0 and n_chunks % 2 == 0
    kk = QK // wk
    kv, kr = 2 * QK // wv, (2 * QK + V) // wv
    blocks = (2 * (3 * tb * wk * 4 + 2 * tb * wv * 4 + tb * wv * 2) + hp * dv * dk * 4 + 8 * tb * wk * 4)
    return pl.pallas_call(
        functools.partial(_gla_rec_kernel, n_chunks=n_chunks, dk=dk, dv=dv),
        grid=(B, GLA_HEADS // hp, T // tb),
        in_specs=[
            pl.BlockSpec((None, tb, wk), lambda b, h, t: (b, t, h)),
            pl.BlockSpec((None, tb, wk), lambda b, h, t: (b, t, kk + h)),
            pl.BlockSpec((None, tb, wv), lambda b, h, t: (b, t, kv + h)),
            pl.BlockSpec((None, tb, wv), lambda b, h, t: (b, t, kr + h)),
            pl.BlockSpec((None, tb, wk), lambda b, h, t: (b, t, h)),
            pl.BlockSpec((1, dv), lambda b, h, t: (0, 0)),
        ],
        out_specs=pl.BlockSpec((None, tb, wv), lambda b, h, t: (b, t, h)),
        out_shape=jax.ShapeDtypeStruct((B, T, V), BF16),
        scratch_shapes=[pltpu.VMEM((hp, dv, dk), F32)],
        compiler_params=_params(("arbitrary", "arbitrary", "arbitrary"), blocks),
        name="gla_recurrence",
    )(proj, proj, proj, proj, bc, norm_g.reshape(1, dv))


def _proj_ln_kernel(a_ref, w_ref, x_ref, g_ref, lg_ref, lb_ref, o_ref, *, alpha):
    groups = _row_slices(a_ref.shape[0], 4)

    def finish(sl, y):
        z = alpha * x_ref[sl, :] + (1.0 + g_ref[...]) * y
        o_ref[sl, :] = _layer_norm(z, lg_ref[...], lb_ref[...])

    pending = None
    for sl in groups:
        y = _dot(a_ref[sl, :], w_ref[...])
        if pending is not None:
            finish(*pending)
        pending = (sl, y)
    finish(*pending)


def _proj_residual_ln(a, w, x, mod4, layer, gate_slot, ln_g, ln_b, alpha):
    B, T, D = x.shape
    K = a.shape[-1]
    tm = _tile(T, 1024)
    blocks = 2 * tm * K * 2 + K * D * 2 + 4 * tm * D * 4 + 3 * (tm // 4) * D * 4
    w, w_spec = _weight_operand(w, (K, D), lambda b, i: (0, 0), pipeline_mode=pl.Buffered(1))
    return pl.pallas_call(
        functools.partial(_proj_ln_kernel, alpha=alpha),
        grid=(B, T // tm),
        in_specs=[
            pl.BlockSpec((None, tm, K), lambda b, i: (b, i, 0)),
            w_spec,
            pl.BlockSpec((None, tm, D), lambda b, i: (b, i, 0)),
            _mod_spec(layer, gate_slot, D, 2),
            pl.BlockSpec((1, D), lambda b, i: (0, 0)),
            pl.BlockSpec((1, D), lambda b, i: (0, 0)),
        ],
        out_specs=pl.BlockSpec((None, tm, D), lambda b, i: (b, i, 0)),
        out_shape=jax.ShapeDtypeStruct((B, T, D), F32),
        compiler_params=_params(("arbitrary", "arbitrary"), blocks),
        name="proj_residual_ln",
    )(a, w, x, mod4, ln_g.reshape(1, D), ln_b.reshape(1, D))


MLP_TM = 1024
MLP_TF = 1024
MLP_ROW_GROUP = 256

def _mlp_kernel(x_ref, sh_ref, sc_ref, g_ref, w1_ref, b1_ref, w2_ref, b2_ref, lg_ref, lb_ref,
                o_ref, h_ref, *, alpha):
    f = pl.program_id(2)
    last = pl.num_programs(2) - 1
    groups = _row_slices(x_ref.shape[0], x_ref.shape[0] // MLP_ROW_GROUP)

    def contribution(sl):
        u = jnp.maximum(_dot(h_ref[sl, :], w1_ref[...]) + b1_ref[...], 0.0)
        return _dot((u * u).astype(BF16), w2_ref[...])

    @pl.when(f == 0)
    def _():
        for sl in groups:
            h_ref[sl, :] = (x_ref[sl, :] * (1.0 + sc_ref[...]) + sh_ref[...]).astype(BF16)
            o_ref[sl, :] = contribution(sl)

    @pl.when(jnp.logical_and(f > 0, f < last))
    def _():
        for sl in groups:
            o_ref[sl, :] += contribution(sl)

    @pl.when(f == last)
    def _():
        for sl in groups:
            y = o_ref[sl, :] + contribution(sl) + b2_ref[...]
            z = alpha * x_ref[sl, :] + (1.0 + g_ref[...]) * y
            o_ref[sl, :] = _layer_norm(z, lg_ref[...], lb_ref[...])


def _mlp_block(x, mod4, layer, w1, b1, w2, b2, ln_g, ln_b, alpha):
    B, T, D = x.shape
    F = _weight_shape(w1)[1]
    tm, tf = _tile(T, MLP_TM), _tile(F, MLP_TF)
    assert F // tf >= 2, "the first and the last d_ff step must be distinct grid steps"
    assert tm % MLP_ROW_GROUP == 0
    blocks = (4 * tm * D * 4 + tm * D * 2 + 4 * D * tf * 2 + tm * tf * 6 + 3 * MLP_ROW_GROUP * D * 4)
    w1, w1_spec = _weight_operand(w1, (D, tf), lambda b, i, f: (0, f))
    w2, w2_spec = _weight_operand(w2, (tf, D), lambda b, i, f: (f, 0))
    return pl.pallas_call(
        functools.partial(_mlp_kernel, alpha=alpha),
        grid=(B, T // tm, F // tf),
        in_specs=[
            pl.BlockSpec((None, tm, D), lambda b, i, f: (b, i, 0)),
            _mod_spec(layer, 3, D, 3),
            _mod_spec(layer, 4, D, 3),
            _mod_spec(layer, 5, D, 3),
            w1_spec,
            pl.BlockSpec((1, tf), lambda b, i, f: (0, f)),
            w2_spec,
            pl.BlockSpec((1, D), lambda b, i, f: (0, 0)),
            pl.BlockSpec((1, D), lambda b, i, f: (0, 0)),
            pl.BlockSpec((1, D), lambda b, i, f: (0, 0)),
        ],
        out_specs=pl.BlockSpec((None, tm, D), lambda b, i, f: (b, i, 0)),
        out_shape=jax.ShapeDtypeStruct((B, T, D), F32),
        scratch_shapes=[pltpu.VMEM((tm, D), BF16)],
        compiler_params=_params(("arbitrary", "arbitrary", "arbitrary"), blocks),
        name="mlp_block",
    )(x, mod4, mod4, mod4, w1, b1.reshape(1, F), w2, b2.reshape(1, D), ln_g.reshape(1, D), ln_b.reshape(1, D))


SB_BLK = 256
LOG2_E = 1.4426950408889634
LN_2 = 0.6931471805599453
SB_DEAD_LOG2 = 160.0
SB_NO_KEYS = 1e30


def _sb_attn_kernel(q_ref, k_ref, v_ref, *rest, n_pairs, n_cast):
    o_ref = rest[n_cast]
    _cast_slabs(rest[:n_cast], rest[n_cast + 1:])
    _sb_attn_body(q_ref, k_ref, v_ref, o_ref, n_pairs)


def _sb_attn_body(q_ref, k_ref, v_ref, o_ref, n_pairs):
    n = SB_BLK
    dh = q_ref.shape[-1]
    jrow = lax.broadcasted_iota(jnp.int32, (n, n), 0)
    jcol = lax.broadcasted_iota(jnp.int32, (n, n), 1)
    later = (jrow > jcol).astype(BF16)
    diag_mask = jcol < jrow

    def scores(q, blk):
        return _dot_nt(q, k_ref[pl.ds(pl.multiple_of(blk * n, n), n), :])

    def log_terms(z, mask):
        neg_abs = lax.bitcast_convert_type(
            lax.bitcast_convert_type(z, jnp.uint32) | jnp.uint32(0x80000000), F32)
        sp = jnp.maximum(z, 0.0) + jnp.log(1.0 + jnp.exp2(neg_abs)) * LOG2_E
        lb = z - sp
        if mask is not None:
            sp = jnp.where(mask, sp, 0.0)
        return sp.astype(BF16), lb, sp[:, 0:1]

    def suffix(hilo):
        return _dot(hilo, later)

    def weights(lb, sfx, r, sp0, mask):
        a = jnp.exp2(lb - sfx if r is None else lb - (sfx + r))
        if mask is not None:
            a = jnp.where(mask, a, 0.0)
        tot = sfx[:, 0:1] + sp0
        return a.astype(BF16), tot if r is None else r + tot

    def values(a, blk):
        return _dot(a, v_ref[pl.ds(pl.multiple_of(blk * n, n), n), :])

    def step(qa, qb, blk_a, blk_b, ra, rb, acc_a, acc_b, mask):
        za = scores(qa, blk_a)
        zb = scores(qb, blk_b)
        hla, lba, spa0 = log_terms(za, mask)
        sfa = suffix(hla)
        hlb, lbb, spb0 = log_terms(zb, mask)
        aa, ra = weights(lba, sfa, ra, spa0, mask)
        sfb = suffix(hlb)
        acc_a = acc_a + values(aa, blk_a)
        ab, rb = weights(lbb, sfb, rb, spb0, mask)
        acc_b = acc_b + values(ab, blk_b)
        return ra, rb, acc_a, acc_b

    def pair_body(p, _):
        blk_a0 = 2 * p
        blk_b0 = blk_a0 + 1
        row_a = pl.multiple_of(blk_a0 * n, n)
        row_b = pl.multiple_of(blk_b0 * n, n)
        qa = q_ref[pl.ds(row_a, n), :]
        qb = q_ref[pl.ds(row_b, n), :]

        blk_a1 = jnp.maximum(blk_a0 - 1, 0)
        blk_b1 = blk_a0
        za0 = scores(qa, blk_a0)
        zb0 = scores(qb, blk_b0)
        za1 = scores(qa, blk_a1)
        zb1 = scores(qb, blk_b1)
        hla0, lba0, spa0 = log_terms(za0, diag_mask)
        sfa0 = suffix(hla0)
        hlb0, lbb0, spb0 = log_terms(zb0, diag_mask)
        sfb0 = suffix(hlb0)
        hla1, lba1, spa1 = log_terms(za1, None)
        aa0, ra = weights(lba0, sfa0, None, spa0, diag_mask)
        sfa1 = suffix(hla1)
        acc_a = values(aa0, blk_a0)
        hlb1, lbb1, spb1 = log_terms(zb1, None)
        ab0, rb = weights(lbb0, sfb0, None, spb0, diag_mask)
        sfb1 = suffix(hlb1)
        acc_b = values(ab0, blk_b0)
        ra = jnp.where(p > 0, ra, SB_NO_KEYS)
        aa1, ra = weights(lba1, sfa1, ra, spa1, None)
        acc_a = acc_a + values(aa1, blk_a1)
        ab1, rb = weights(lbb1, sfb1, rb, spb1, None)
        acc_b = acc_b + values(ab1, blk_b1)

        def alive(ra, rb):
            return jnp.minimum(jnp.min(ra), jnp.min(rb)) < SB_DEAD_LOG2

        def cond(carry):
            i, go = carry[0], carry[1]
            return jnp.logical_and(i <= blk_b0, go)

        def body(carry):
            i, _, ra, rb, acc_a, acc_b = carry
            has_a = i <= blk_a0
            ra = jnp.where(has_a, ra, SB_NO_KEYS)
            ra, rb, acc_a, acc_b = step(qa, qb, jnp.maximum(blk_a0 - i, 0), blk_b0 - i,
                                        ra, rb, acc_a, acc_b, None)
            return i + 1, alive(ra, rb), ra, rb, acc_a, acc_b

        carry = lax.while_loop(cond, body, (jnp.int32(2), alive(ra, rb), ra, rb, acc_a, acc_b))
        o_ref[pl.ds(row_a, n), :] = carry[4].astype(o_ref.dtype)
        o_ref[pl.ds(row_b, n), :] = carry[5].astype(o_ref.dtype)
        return 0

    lax.fori_loop(0, n_pairs, pair_body, 0)


def _sb_attention(q, kv, cast_weights=()):
    B, T, HD = q.shape
    dh = HD // SB_HEADS
    assert dh % LANES == 0 and T % (2 * SB_BLK) == 0
    c_in, c_out, c_shapes, c_bytes = _cast_specs(cast_weights, B * SB_HEADS, lambda b, h: b * SB_HEADS + h)
    blocks = 8 * T * dh * 2 + 24 * SB_BLK * SB_BLK * 4 + c_bytes
    outs = pl.pallas_call(
        functools.partial(_sb_attn_kernel, n_pairs=T // (2 * SB_BLK), n_cast=len(cast_weights)),
        grid=(B, SB_HEADS),
        in_specs=[
            pl.BlockSpec((None, T, dh), lambda b, h: (b, 0, h)),
            pl.BlockSpec((None, T, dh), lambda b, h: (b, 0, h)),
            pl.BlockSpec((None, T, dh), lambda b, h: (b, 0, SB_HEADS + h)),
        ] + c_in,
        out_specs=[pl.BlockSpec((None, T, dh), lambda b, h: (b, 0, h))] + c_out,
        out_shape=[jax.ShapeDtypeStruct((B, T, HD), BF16)] + c_shapes,
        compiler_params=_params(("arbitrary", "arbitrary"), blocks),
        name="stick_breaking_attention",
    )(q, kv, kv, *[w for w, _ in cast_weights])
    return outs[0], tuple(outs[1:])


def kernel(x, c, ada_w, ada_b, ln_g, ln_b, gla_w_in, gla_w_gate_up, gla_b_gate, gla_norm_g, gla_w_out, kv_ada_w, kv_ada_b, w_kv, sb_w_q, sb_w_out, mlp_w1, mlp_b1, mlp_w2, mlp_b2):
    B, T, D = x.shape
    depth = ada_w.shape[0]
    n_a = gla_w_in.shape[0]
    alpha = (2.0 * depth) ** 0.25
    QK = gla_w_gate_up.shape[-1]
    V = gla_w_out.shape[1]
    n_main = 2 * QK + 2 * V
    dh = sb_w_q.shape[-1] // SB_HEADS

    c_pad = jnp.pad(c, ((0, MOD_ROWS - B), (0, 0)))
    mod4 = _mod_vectors(c_pad, ada_w, ada_b).reshape(depth, MOD_ROWS, 1, N_MOD * D)
    kv_mod4 = _mod_vectors(c_pad, kv_ada_w[None], kv_ada_b[None]).reshape(1, MOD_ROWS, 1, 2 * D)

    gla_w_in_bf = gla_w_in.astype(BF16)
    gla_w_out_bf = gla_w_out.astype(BF16)
    w_kv_bf = w_kv.astype(BF16)
    sb_w_q_bf = sb_w_q.astype(BF16)

    kv = None
    for l in range(depth):
        if l < n_a:
            pad = LANES - GLA_GATE_RANK
            w_gate = jnp.pad(gla_w_in_bf[l, :, n_main:], ((0, 0), (0, pad)))
            w_up = jnp.pad(gla_w_gate_up[l], ((0, pad), (0, 0))).astype(BF16)
            proj, gl, (w1, w2) = _mod_matmul(
                x, mod4, l, 0, 1, (gla_w_in_bf, l), F32, n_cols=n_main, w_side=w_gate,
                cast_weights=((mlp_w1, l), (mlp_w2, l)))
            w_out = (gla_w_out_bf, l)
            bc = _gla_cum_log_decay(gl, w_up, gla_b_gate[l])
            mixed = _gla_recurrence(proj, bc, gla_norm_g[l])
        else:
            j = l - n_a
            if kv is None:
                kv, _, _ = _mod_matmul(x, kv_mod4, 0, 0, 1, w_kv_bf, BF16)
            q, _, _ = _mod_matmul(x, mod4, l, 0, 1, (sb_w_q_bf, j), BF16, out_scale=dh ** -0.5 * LOG2_E)
            mixed, (w_out, w1, w2) = _sb_attention(q, kv, ((sb_w_out, j), (mlp_w1, l), (mlp_w2, l)))
        x = _proj_residual_ln(mixed, w_out, x, mod4, l, 2, ln_g[l, 0], ln_b[l, 0], alpha)
        x = _mlp_block(x, mod4, l, w1, mlp_b1[l], w2, mlp_b2[l], ln_g[l, 1], ln_b[l, 1], alpha)
    return x
```

```python
import functools

import jax
import jax.numpy as jnp
from jax import lax
from jax.experimental import pallas as pl
from jax.experimental.pallas import tpu as pltpu

F32 = jnp.float32
BF16 = jnp.bfloat16

GLA_HEADS = 4
GLA_GATE_RANK = 16
GLA_GATE_TAU = 16.0
GLA_CHUNK = 64
SB_HEADS = 16
N_MOD = 6
LN_EPS = 1e-5
RMS_EPS = 1e-5

LANES = 128
MOD_ROWS = 8
VMEM_LIMIT_CAP = 58 << 20
VMEM_HEADROOM = 6 << 20


def _params(semantics, block_bytes):
    limit = min(int(block_bytes) + VMEM_HEADROOM, VMEM_LIMIT_CAP)
    return pltpu.CompilerParams(dimension_semantics=semantics, vmem_limit_bytes=limit)


def _tile(n, want):
    t = min(n, want)
    assert n % t == 0, (n, want)
    return t


def _row_slices(n_rows, n_split):
    assert n_split >= 1 and n_rows % n_split == 0
    step = n_rows // n_split
    return [slice(i * step, (i + 1) * step) for i in range(n_split)]


def _weight_shape(w):
    return (w[0] if isinstance(w, tuple) else w).shape[-2:]


def _weight_operand(w, block, index_map, **spec_kwargs):
    if isinstance(w, tuple):
        stacked, entry = w
        return stacked, pl.BlockSpec((None,) + block, lambda *g: (entry,) + index_map(*g), **spec_kwargs)
    return w, pl.BlockSpec(block, index_map, **spec_kwargs)


def _dot(a, b):
    return jnp.dot(a, b, preferred_element_type=F32)


def _dot_nt(a, b):
    return lax.dot_general(a, b, (((1,), (1,)), ((), ())), preferred_element_type=F32)


def _dot_tn(a, b):
    return lax.dot_general(a, b, (((0,), (0,)), ((), ())), preferred_element_type=F32)


def _split_bf16(x):
    hi = x.astype(BF16)
    lo = (x - hi.astype(F32)).astype(BF16)
    return hi, lo


BF16_SUBLANES = 16


def _cast_divides(weights, n_steps):
    return all(w.shape[-2] % n_steps == 0 and (w.shape[-2] // n_steps) % BF16_SUBLANES == 0
               for w, _ in weights)


def _cast_specs(weights, n_steps, step_index):
    in_specs, out_specs, out_shapes, nbytes = [], [], [], 0
    for w, layer in weights:
        R, C = w.shape[-2:]
        rows = R // n_steps
        assert R % n_steps == 0 and rows % BF16_SUBLANES == 0, (w.shape, n_steps)
        if layer is None:
            in_specs.append(pl.BlockSpec((rows, C), lambda *g: (step_index(*g), 0)))
        else:
            in_specs.append(pl.BlockSpec((None, rows, C), lambda *g, _l=layer: (_l, step_index(*g), 0)))
        out_specs.append(pl.BlockSpec((rows, C), lambda *g: (step_index(*g), 0)))
        out_shapes.append(jax.ShapeDtypeStruct((R, C), BF16))
        nbytes += 2 * rows * C * (4 + 2)
    return in_specs, out_specs, out_shapes, nbytes


def _cast_slabs(src_refs, dst_refs):
    for src, dst in zip(src_refs, dst_refs):
        dst[...] = src[...].astype(dst.dtype)


def _layer_norm(z, g, b):
    mu = jnp.mean(z, axis=-1, keepdims=True)
    zc = z - mu
    var = jnp.mean(zc * zc, axis=-1, keepdims=True)
    return zc * lax.rsqrt(var + LN_EPS) * g + b


def _mod_kernel(c_ref, w_ref, b_ref, o_ref):
    c = c_ref[...]
    cs = c * (1.0 / (1.0 + jnp.exp(-c)))
    hi, lo = _split_bf16(cs)
    y = _dot(jnp.concatenate([hi, lo], axis=0), w_ref[...].astype(BF16))
    o_ref[...] = y[:MOD_ROWS] + y[MOD_ROWS:] + b_ref[...]


def _mod_vectors(c_pad, w, b):
    L, D, N = w.shape
    tn = _tile(N, 2048)
    blocks = 2 * (D * tn * 4) + MOD_ROWS * D * 4 + 4 * MOD_ROWS * tn * 4
    return pl.pallas_call(
        _mod_kernel,
        grid=(L, N // tn),
        in_specs=[
            pl.BlockSpec((MOD_ROWS, D), lambda l, j: (0, 0)),
            pl.BlockSpec((None, D, tn), lambda l, j: (l, 0, j)),
            pl.BlockSpec((None, 1, tn), lambda l, j: (l, 0, j)),
        ],
        out_specs=pl.BlockSpec((None, MOD_ROWS, tn), lambda l, j: (l, 0, j)),
        out_shape=jax.ShapeDtypeStruct((L, MOD_ROWS, N), F32),
        compiler_params=_params(("arbitrary", "arbitrary"), blocks),
        name="adaln_vectors",
    )(c_pad, w, b.reshape(L, 1, N))


def _mod_spec(layer, slot, D, n_grid):
    if n_grid == 2:
        return pl.BlockSpec((None, None, 1, D), lambda b, i: (layer, b, 0, slot))
    return pl.BlockSpec((None, None, 1, D), lambda b, i, j: (layer, b, 0, slot))


MODMM_ROW_GROUP = 256
MODMM_MIN_HEADROOM = 2 << 20


def _modmm_kernel(x_ref, sh_ref, sc_ref, w_ref, *rest, out_scale, with_side, n_cast):
    rest = list(rest)
    h_ref = rest.pop()
    ws_ref = rest.pop(0) if with_side else None
    cast_src = [rest.pop(0) for _ in range(n_cast)]
    o_ref = rest.pop(0)
    side_ref = rest.pop(0) if with_side else None
    _cast_slabs(cast_src, rest)
    groups = _row_slices(x_ref.shape[0], x_ref.shape[0] // MODMM_ROW_GROUP)

    def emit(sl):
        y = _dot(h_ref[sl, :], w_ref[...])
        if out_scale != 1.0:
            y = y * out_scale
        o_ref[sl, :] = y.astype(o_ref.dtype)

    @pl.when(pl.program_id(2) == 0)
    def _():
        for sl in groups:
            h_ref[sl, :] = (x_ref[sl, :] * (1.0 + sc_ref[...]) + sh_ref[...]).astype(BF16)
            emit(sl)
            if with_side:
                side_ref[sl, :] = _dot(h_ref[sl, :], ws_ref[...])

    @pl.when(pl.program_id(2) > 0)
    def _():
        for sl in groups:
            emit(sl)


def _mod_matmul(x, mod4, layer, sh_slot, sc_slot, w, out_dtype, out_scale=1.0, n_cols=None, w_side=None,
                cast_weights=()):
    B, T, D = x.shape
    N = _weight_shape(w)[1] if n_cols is None else n_cols
    tm = _tile(T, 1024)
    osz = jnp.dtype(out_dtype).itemsize
    S = 0 if w_side is None else w_side.shape[1]
    n_i = T // tm
    col_tiles = [t for t in (2048, 1536, 1024, 512, 256, 128) if N % t == 0]

    def plan(tn):
        n_j = N // tn
        if not _cast_divides(cast_weights, B * n_i * n_j):
            return None, VMEM_LIMIT_CAP
        c = _cast_specs(cast_weights, B * n_i * n_j, lambda b, i, j: (b * n_i + i) * n_j + j)
        nbytes = (2 * tm * D * 4 + 2 * D * tn * 2 + 2 * tm * tn * osz + tm * D * 2
                  + 2 * MODMM_ROW_GROUP * tn * 4 + 2 * D * S * 2 + 2 * tm * S * 4 + c[3])
        return c, nbytes

    tn = next(t for t in col_tiles if plan(t)[1] + MODMM_MIN_HEADROOM <= VMEM_LIMIT_CAP)
    (c_in, c_out, c_shapes, _), blocks = plan(tn)
    w_arr, w_spec = _weight_operand(w, (D, tn), lambda b, i, j: (0, j))
    in_specs = [
        pl.BlockSpec((None, tm, D), lambda b, i, j: (b, i, 0)),
        _mod_spec(layer, sh_slot, D, 3),
        _mod_spec(layer, sc_slot, D, 3),
        w_spec,
    ]
    out_specs = [pl.BlockSpec((None, tm, tn), lambda b, i, j: (b, i, j))]
    out_shape = [jax.ShapeDtypeStruct((B, T, N), out_dtype)]
    args = [x, mod4, mod4, w_arr]
    if w_side is not None:
        in_specs.append(pl.BlockSpec((D, S), lambda b, i, j: (0, 0)))
        out_specs.append(pl.BlockSpec((None, tm, S), lambda b, i, j: (b, i, 0)))
        out_shape.append(jax.ShapeDtypeStruct((B, T, S), F32))
        args.append(w_side)
    outs = pl.pallas_call(
        functools.partial(_modmm_kernel, out_scale=out_scale, with_side=w_side is not None,
                          n_cast=len(cast_weights)),
        grid=(B, n_i, N // tn),
        in_specs=in_specs + c_in,
        out_specs=out_specs + c_out,
        out_shape=out_shape + c_shapes,
        scratch_shapes=[pltpu.VMEM((tm, D), BF16)],
        compiler_params=_params(("arbitrary", "arbitrary", "arbitrary"), blocks),
        name="modulated_matmul",
    )(*args, *[cw for cw, _ in cast_weights])
    n_main = 1 if w_side is None else 2
    return outs[0], (None if w_side is None else outs[1]), tuple(outs[n_main:])


GLA_CUMSUM_ROWS = 2 * GLA_CHUNK


def _gla_gate_kernel(gl_ref, wup_ref, bg_ref, tri_ref, o_ref):
    n = GLA_CUMSUM_ROWS
    for sl in _row_slices(gl_ref.shape[0], gl_ref.shape[0] // n):
        g = _dot(gl_ref[sl, :].astype(BF16), wup_ref[...]) + bg_ref[...]
        log_a = (jnp.minimum(g, 0.0) - jnp.log(1.0 + jnp.exp(-jnp.abs(g)))) * (1.0 / GLA_GATE_TAU)
        hi, lo = _split_bf16(log_a)
        o_ref[sl, :] = _dot(tri_ref[...], jnp.concatenate([hi, lo], axis=0))


def _chunk_cumsum_matrix():
    n = GLA_CUMSUM_ROWS
    row = lax.broadcasted_iota(jnp.int32, (n, n), 0)
    col = lax.broadcasted_iota(jnp.int32, (n, n), 1)
    tri = ((col <= row) & (row // GLA_CHUNK == col // GLA_CHUNK)).astype(BF16)
    return jnp.concatenate([tri, tri], axis=1)


def _gla_cum_log_decay(gl, w_up, b_gate):
    B, T, R = gl.shape
    QK = w_up.shape[1]
    tm = _tile(T, 1024)
    n = GLA_CUMSUM_ROWS
    assert tm % n == 0
    blocks = 2 * tm * R * 4 + 2 * R * QK * 2 + 2 * n * 2 * n * 2 + 2 * tm * QK * 4 + 8 * n * QK * 4
    return pl.pallas_call(
        _gla_gate_kernel,
        grid=(B, T // tm),
        in_specs=[
            pl.BlockSpec((None, tm, R), lambda b, i: (b, i, 0)),
            pl.BlockSpec((R, QK), lambda b, i: (0, 0)),
            pl.BlockSpec((1, QK), lambda b, i: (0, 0)),
            pl.BlockSpec((n, 2 * n), lambda b, i: (0, 0)),
        ],
        out_specs=pl.BlockSpec((None, tm, QK), lambda b, i: (b, i, 0)),
        out_shape=jax.ShapeDtypeStruct((B, T, QK), F32),
        compiler_params=_params(("arbitrary", "arbitrary"), blocks),
        name="gla_cum_log_decay",
    )(gl, w_up, b_gate.reshape(1, QK), _chunk_cumsum_matrix())


GLA_HEADS_PER_STEP = 2


def _gla_rec_kernel(q_ref, k_ref, v_ref, r_ref, bc_ref, ng_ref, o_ref, st_ref, *, n_chunks, dk, dv):
    C = GLA_CHUNK

    @pl.when(pl.program_id(2) == 0)
    def _():
        st_ref[...] = jnp.zeros_like(st_ref)

    crow = lax.broadcasted_iota(jnp.int32, (C, C), 0)
    ccol = lax.broadcasted_iota(jnp.int32, (C, C), 1)
    causal = ccol <= crow
    prow = lax.broadcasted_iota(jnp.int32, (C, 2 * C), 0)
    pcol = lax.broadcasted_iota(jnp.int32, (C, 2 * C), 1)
    causal_second = pcol <= prow + C
    ng = ng_ref[...]

    bc, k, q_dec, k_intra = [], [], [], []
    for h in range(GLA_HEADS_PER_STEP):
        ks = slice(h * dk, (h + 1) * dk)
        bc.append(bc_ref[:, ks])
        k.append(k_ref[:, ks])
        q_dec.append(q_ref[:, ks] * (dk ** -0.5) * jnp.exp(bc[h]))
        k_intra.append((k[h] * jnp.exp(-bc[h])).astype(BF16))

    for g in range(n_chunks // 2):
        c1 = slice(2 * g * C, (2 * g + 1) * C)
        c2 = slice((2 * g + 1) * C, (2 * g + 2) * C)
        both = slice(2 * g * C, (2 * g + 2) * C)
        for h in range(GLA_HEADS_PER_STEP):
            vs = slice(h * dv, (h + 1) * dv)
            b1 = bc[h][(2 * g + 1) * C - 1:(2 * g + 1) * C, :]
            b2 = bc[h][(2 * g + 2) * C - 1:(2 * g + 2) * C, :]
            k1_mid = (k[h][c1] * jnp.exp(b1 - bc[h][c1])).astype(BF16)
            k1_end = k[h][c1] * jnp.exp((b1 + b2) - bc[h][c1])
            k2_end = k[h][c2] * jnp.exp(b2 - bc[h][c2])
            k_end = jnp.concatenate([k1_end, k2_end], axis=0).astype(BF16)
            q1 = q_dec[h][c1].astype(BF16)
            q2 = q_dec[h][c2].astype(BF16)
            q2_from_start = (q_dec[h][c2] * jnp.exp(b1)).astype(BF16)
            v_both = v_ref[both, vs].astype(BF16)
            s1 = jnp.where(causal, _dot_nt(q1, k_intra[h][c1]), 0.0)
            s2 = jnp.where(causal_second,
                           _dot_nt(q2, jnp.concatenate([k1_mid, k_intra[h][c2]], axis=0)), 0.0)
            st = st_ref[h]
            st_bf = st.astype(BF16)
            o1 = _dot(s1.astype(BF16), v_both[:C]) + _dot_nt(q1, st_bf)
            o2 = _dot(s2.astype(BF16), v_both) + _dot_nt(q2_from_start, st_bf)
            st_ref[h] = st * jnp.exp(b1 + b2) + _dot_tn(v_both, k_end)
            for sl, o in ((c1, o1), (c2, o2)):
                o = o * lax.rsqrt(jnp.mean(o * o, axis=-1, keepdims=True) + RMS_EPS) * ng
                r = r_ref[sl, vs]
                o_ref[sl, vs] = (o * (r * (1.0 / (1.0 + jnp.exp(-r))))).astype(o_ref.dtype)


def _gla_recurrence(proj, bc, norm_g):
    B, T, _ = proj.shape
    QK = bc.shape[-1]
    dk = QK // GLA_HEADS
    dv = norm_g.shape[-1]
    V = dv * GLA_HEADS
    hp = GLA_HEADS_PER_STEP
    assert dk % LANES == 0 and dv % LANES == 0 and GLA_HEADS % hp == 0
    wk, wv = hp * dk, hp * dv
    assert QK % wk == 0 and (2 * QK) % wv == 0 and (2 * QK + V) % wv == 0
    n_chunks = min(8, T // GLA_CHUNK)
    tb = n_chunks * GLA_CHUNK
    assert T % tb == 0 and n_chunks % 2 == 0
    kk = QK // wk
    kv, kr = 2 * QK // wv, (2 * QK + V) // wv
    blocks = (2 * (3 * tb * wk * 4 + 2 * tb * wv * 4 + tb * wv * 2) + hp * dv * dk * 4 + 8 * tb * wk * 4)
    return pl.pallas_call(
        functools.partial(_gla_rec_kernel, n_chunks=n_chunks, dk=dk, dv=dv),
        grid=(B, GLA_HEADS // hp, T // tb),
        in_specs=[
            pl.BlockSpec((None, tb, wk), lambda b, h, t: (b, t, h)),
            pl.BlockSpec((None, tb, wk), lambda b, h, t: (b, t, kk + h)),
            pl.BlockSpec((None, tb, wv), lambda b, h, t: (b, t, kv + h)),
            pl.BlockSpec((None, tb, wv), lambda b, h, t: (b, t, kr + h)),
            pl.BlockSpec((None, tb, wk), lambda b, h, t: (b, t, h)),
            pl.BlockSpec((1, dv), lambda b, h, t: (0, 0)),
        ],
        out_specs=pl.BlockSpec((None, tb, wv), lambda b, h, t: (b, t, h)),
        out_shape=jax.ShapeDtypeStruct((B, T, V), BF16),
        scratch_shapes=[pltpu.VMEM((hp, dv, dk), F32)],
        compiler_params=_params(("arbitrary", "arbitrary", "arbitrary"), blocks),
        name="gla_recurrence",
    )(proj, proj, proj, proj, bc, norm_g.reshape(1, dv))


def _proj_ln_kernel(a_ref, w_ref, x_ref, g_ref, lg_ref, lb_ref, o_ref, *, alpha):
    groups = _row_slices(a_ref.shape[0], 4)

    def finish(sl, y):
        z = alpha * x_ref[sl, :] + (1.0 + g_ref[...]) * y
        o_ref[sl, :] = _layer_norm(z, lg_ref[...], lb_ref[...])

    pending = None
    for sl in groups:
        y = _dot(a_ref[sl, :], w_ref[...])
        if pending is not None:
            finish(*pending)
        pending = (sl, y)
    finish(*pending)


def _proj_residual_ln(a, w, x, mod4, layer, gate_slot, ln_g, ln_b, alpha):
    B, T, D = x.shape
    K = a.shape[-1]
    tm = _tile(T, 1024)
    blocks = 2 * tm * K * 2 + K * D * 2 + 4 * tm * D * 4 + 3 * (tm // 4) * D * 4
    w, w_spec = _weight_operand(w, (K, D), lambda b, i: (0, 0), pipeline_mode=pl.Buffered(1))
    return pl.pallas_call(
        functools.partial(_proj_ln_kernel, alpha=alpha),
        grid=(B, T // tm),
        in_specs=[
            pl.BlockSpec((None, tm, K), lambda b, i: (b, i, 0)),
            w_spec,
            pl.BlockSpec((None, tm, D), lambda b, i: (b, i, 0)),
            _mod_spec(layer, gate_slot, D, 2),
            pl.BlockSpec((1, D), lambda b, i: (0, 0)),
            pl.BlockSpec((1, D), lambda b, i: (0, 0)),
        ],
        out_specs=pl.BlockSpec((None, tm, D), lambda b, i: (b, i, 0)),
        out_shape=jax.ShapeDtypeStruct((B, T, D), F32),
        compiler_params=_params(("arbitrary", "arbitrary"), blocks),
        name="proj_residual_ln",
    )(a, w, x, mod4, ln_g.reshape(1, D), ln_b.reshape(1, D))


MLP_TM = 1024
MLP_TF = 1024
MLP_ROW_GROUP = 256

def _mlp_kernel(x_ref, sh_ref, sc_ref, g_ref, w1_ref, b1_ref, w2_ref, b2_ref, lg_ref, lb_ref,
                o_ref, h_ref, *, alpha):
    f = pl.program_id(2)
    last = pl.num_programs(2) - 1
    groups = _row_slices(x_ref.shape[0], x_ref.shape[0] // MLP_ROW_GROUP)

    def contribution(sl):
        u = jnp.maximum(_dot(h_ref[sl, :], w1_ref[...]) + b1_ref[...], 0.0)
        return _dot((u * u).astype(BF16), w2_ref[...])

    @pl.when(f == 0)
    def _():
        for sl in groups:
            h_ref[sl, :] = (x_ref[sl, :] * (1.0 + sc_ref[...]) + sh_ref[...]).astype(BF16)
            o_ref[sl, :] = contribution(sl)

    @pl.when(jnp.logical_and(f > 0, f < last))
    def _():
        for sl in groups:
            o_ref[sl, :] += contribution(sl)

    @pl.when(f == last)
    def _():
        for sl in groups:
            y = o_ref[sl, :] + contribution(sl) + b2_ref[...]
            z = alpha * x_ref[sl, :] + (1.0 + g_ref[...]) * y
            o_ref[sl, :] = _layer_norm(z, lg_ref[...], lb_ref[...])


def _mlp_block(x, mod4, layer, w1, b1, w2, b2, ln_g, ln_b, alpha):
    B, T, D = x.shape
    F = _weight_shape(w1)[1]
    tm, tf = _tile(T, MLP_TM), _tile(F, MLP_TF)
    assert F // tf >= 2, "the first and the last d_ff step must be distinct grid steps"
    assert tm % MLP_ROW_GROUP == 0
    blocks = (4 * tm * D * 4 + tm * D * 2 + 4 * D * tf * 2 + tm * tf * 6 + 3 * MLP_ROW_GROUP * D * 4)
    w1, w1_spec = _weight_operand(w1, (D, tf), lambda b, i, f: (0, f))
    w2, w2_spec = _weight_operand(w2, (tf, D), lambda b, i, f: (f, 0))
    return pl.pallas_call(
        functools.partial(_mlp_kernel, alpha=alpha),
        grid=(B, T // tm, F // tf),
        in_specs=[
            pl.BlockSpec((None, tm, D), lambda b, i, f: (b, i, 0)),
            _mod_spec(layer, 3, D, 3),
            _mod_spec(layer, 4, D, 3),
            _mod_spec(layer, 5, D, 3),
            w1_spec,
            pl.BlockSpec((1, tf), lambda b, i, f: (0, f)),
            w2_spec,
            pl.BlockSpec((1, D), lambda b, i, f: (0, 0)),
            pl.BlockSpec((1, D), lambda b, i, f: (0, 0)),
            pl.BlockSpec((1, D), lambda b, i, f: (0, 0)),
        ],
        out_specs=pl.BlockSpec((None, tm, D), lambda b, i, f: (b, i, 0)),
        out_shape=jax.ShapeDtypeStruct((B, T, D), F32),
        scratch_shapes=[pltpu.VMEM((tm, D), BF16)],
        compiler_params=_params(("arbitrary", "arbitrary", "arbitrary"), blocks),
        name="mlp_block",
    )(x, mod4, mod4, mod4, w1, b1.reshape(1, F), w2, b2.reshape(1, D), ln_g.reshape(1, D), ln_b.reshape(1, D))


SB_BLK = 256
LOG2_E = 1.4426950408889634
SB_DEAD_LOG2 = 160.0
SB_NO_KEYS = 1e30


def _sb_attn_kernel(q_ref, k_ref, v_ref, *rest, n_pairs, n_cast):
    o_ref = rest[n_cast]
    _cast_slabs(rest[:n_cast], rest[n_cast + 1:])
    _sb_attn_body(q_ref, k_ref, v_ref, o_ref, n_pairs)


def _sb_attn_body(q_ref, k_ref, v_ref, o_ref, n_pairs):
    n = SB_BLK
    dh = q_ref.shape[-1]
    jrow = lax.broadcasted_iota(jnp.int32, (n, n), 0)
    jcol = lax.broadcasted_iota(jnp.int32, (n, n), 1)
    later = (jrow > jcol).astype(BF16)
    diag_mask = jcol < jrow

    def scores(q, blk):
        return _dot_nt(q, k_ref[pl.ds(pl.multiple_of(blk * n, n), n), :])

    def log_terms(z, mask):
        neg_abs = lax.bitcast_convert_type(
            lax.bitcast_convert_type(z, jnp.uint32) | jnp.uint32(0x80000000), F32)
        sp = jnp.maximum(z, 0.0) + jnp.log(1.0 + jnp.exp2(neg_abs)) * LOG2_E
        lb = z - sp
        if mask is not None:
            sp = jnp.where(mask, sp, 0.0)
        return sp.astype(BF16), lb, sp[:, 0:1]

    def suffix(hilo):
        return _dot(hilo, later)

    def weights(lb, sfx, r, sp0, mask):
        a = jnp.exp2(lb - sfx if r is None else lb - (sfx + r))
        if mask is not None:
            a = jnp.where(mask, a, 0.0)
        tot = sfx[:, 0:1] + sp0
        return a.astype(BF16), tot if r is None else r + tot

    def values(a, blk):
        return _dot(a, v_ref[pl.ds(pl.multiple_of(blk * n, n), n), :])

    def step(qa, qb, blk_a, blk_b, ra, rb, acc_a, acc_b, mask):
        za = scores(qa, blk_a)
        zb = scores(qb, blk_b)
        hla, lba, spa0 = log_terms(za, mask)
        sfa = suffix(hla)
        hlb, lbb, spb0 = log_terms(zb, mask)
        aa, ra = weights(lba, sfa, ra, spa0, mask)
        sfb = suffix(hlb)
        acc_a = acc_a + values(aa, blk_a)
        ab, rb = weights(lbb, sfb, rb, spb0, mask)
        acc_b = acc_b + values(ab, blk_b)
        return ra, rb, acc_a, acc_b

    def pair_body(p, _):
        blk_a0 = 2 * p
        blk_b0 = blk_a0 + 1
        row_a = pl.multiple_of(blk_a0 * n, n)
        row_b = pl.multiple_of(blk_b0 * n, n)
        qa = q_ref[pl.ds(row_a, n), :]
        qb = q_ref[pl.ds(row_b, n), :]

        blk_a1 = jnp.maximum(blk_a0 - 1, 0)
        blk_b1 = blk_a0
        za0 = scores(qa, blk_a0)
        zb0 = scores(qb, blk_b0)
        za1 = scores(qa, blk_a1)
        zb1 = scores(qb, blk_b1)
        hla0, lba0, spa0 = log_terms(za0, diag_mask)
        sfa0 = suffix(hla0)
        hlb0, lbb0, spb0 = log_terms(zb0, diag_mask)
        sfb0 = suffix(hlb0)
        hla1, lba1, spa1 = log_terms(za1, None)
        aa0, ra = weights(lba0, sfa0, None, spa0, diag_mask)
        sfa1 = suffix(hla1)
        acc_a = values(aa0, blk_a0)
        hlb1, lbb1, spb1 = log_terms(zb1, None)
        ab0, rb = weights(lbb0, sfb0, None, spb0, diag_mask)
        sfb1 = suffix(hlb1)
        acc_b = values(ab0, blk_b0)
        ra = jnp.where(p > 0, ra, SB_NO_KEYS)
        aa1, ra = weights(lba1, sfa1, ra, spa1, None)
        acc_a = acc_a + values(aa1, blk_a1)
        ab1, rb = weights(lbb1, sfb1, rb, spb1, None)
        acc_b = acc_b + values(ab1, blk_b1)

        def alive(ra, rb):
            return jnp.minimum(jnp.min(ra), jnp.min(rb)) < SB_DEAD_LOG2

        def cond(carry):
            i, go = carry[0], carry[1]
            return jnp.logical_and(i <= blk_b0, go)

        def body(carry):
            i, _, ra, rb, acc_a, acc_b = carry
            has_a = i <= blk_a0
            ra = jnp.where(has_a, ra, SB_NO_KEYS)
            ra, rb, acc_a, acc_b = step(qa, qb, jnp.maximum(blk_a0 - i, 0), blk_b0 - i,
                                        ra, rb, acc_a, acc_b, None)
            return i + 1, alive(ra, rb), ra, rb, acc_a, acc_b

        carry = lax.while_loop(cond, body, (jnp.int32(2), alive(ra, rb), ra, rb, acc_a, acc_b))
        o_ref[pl.ds(row_a, n), :] = carry[4].astype(o_ref.dtype)
        o_ref[pl.ds(row_b, n), :] = carry[5].astype(o_ref.dtype)
        return 0

    lax.fori_loop(0, n_pairs, pair_body, 0)


def _sb_attention(q, kv, cast_weights=()):
    B, T, HD = q.shape
    dh = HD // SB_HEADS
    assert dh % LANES == 0 and T % (2 * SB_BLK) == 0
    c_in, c_out, c_shapes, c_bytes = _cast_specs(cast_weights, B * SB_HEADS, lambda b, h: b * SB_HEADS + h)
    blocks = 8 * T * dh * 2 + 24 * SB_BLK * SB_BLK * 4 + c_bytes
    outs = pl.pallas_call(
        functools.partial(_sb_attn_kernel, n_pairs=T // (2 * SB_BLK), n_cast=len(cast_weights)),
        grid=(B, SB_HEADS),
        in_specs=[
            pl.BlockSpec((None, T, dh), lambda b, h: (b, 0, h)),
            pl.BlockSpec((None, T, dh), lambda b, h: (b, 0, h)),
            pl.BlockSpec((None, T, dh), lambda b, h: (b, 0, SB_HEADS + h)),
        ] + c_in,
        out_specs=[pl.BlockSpec((None, T, dh), lambda b, h: (b, 0, h))] + c_out,
        out_shape=[jax.ShapeDtypeStruct((B, T, HD), BF16)] + c_shapes,
        compiler_params=_params(("arbitrary", "arbitrary"), blocks),
        name="stick_breaking_attention",
    )(q, kv, kv, *[w for w, _ in cast_weights])
    return outs[0], tuple(outs[1:])


def kernel(x, c, ada_w, ada_b, ln_g, ln_b, gla_w_in, gla_w_gate_up, gla_b_gate, gla_norm_g, gla_w_out, kv_ada_w, kv_ada_b, w_kv, sb_w_q, sb_w_out, mlp_w1, mlp_b1, mlp_w2, mlp_b2):
    B, T, D = x.shape
    depth = ada_w.shape[0]
    n_a = gla_w_in.shape[0]
    alpha = (2.0 * depth) ** 0.25
    QK = gla_w_gate_up.shape[-1]
    V = gla_w_out.shape[1]
    n_main = 2 * QK + 2 * V
    dh = sb_w_q.shape[-1] // SB_HEADS

    c_pad = jnp.pad(c, ((0, MOD_ROWS - B), (0, 0)))
    mod4 = _mod_vectors(c_pad, ada_w, ada_b).reshape(depth, MOD_ROWS, 1, N_MOD * D)
    kv_mod4 = _mod_vectors(c_pad, kv_ada_w[None], kv_ada_b[None]).reshape(1, MOD_ROWS, 1, 2 * D)

    gla_w_in_bf = gla_w_in.astype(BF16)
    gla_w_out_bf = gla_w_out.astype(BF16)
    w_kv_bf = w_kv.astype(BF16)
    sb_w_q_bf = sb_w_q.astype(BF16)

    kv = None
    for l in range(depth):
        if l < n_a:
            pad = LANES - GLA_GATE_RANK
            w_gate = jnp.pad(gla_w_in_bf[l, :, n_main:], ((0, 0), (0, pad)))
            w_up = jnp.pad(gla_w_gate_up[l], ((0, pad), (0, 0))).astype(BF16)
            proj, gl, (w1, w2) = _mod_matmul(
                x, mod4, l, 0, 1, (gla_w_in_bf, l), F32, n_cols=n_main, w_side=w_gate,
                cast_weights=((mlp_w1, l), (mlp_w2, l)))
            w_out = (gla_w_out_bf, l)
            bc = _gla_cum_log_decay(gl, w_up, gla_b_gate[l])
            mixed = _gla_recurrence(proj, bc, gla_norm_g[l])
        else:
            j = l - n_a
            if kv is None:
                kv, _, _ = _mod_matmul(x, kv_mod4, 0, 0, 1, w_kv_bf, BF16)
            q, _, _ = _mod_matmul(x, mod4, l, 0, 1, (sb_w_q_bf, j), BF16, out_scale=dh ** -0.5 * LOG2_E)
            mixed, (w_out, w1, w2) = _sb_attention(q, kv, ((sb_w_out, j), (mlp_w1, l), (mlp_w2, l)))
        x = _proj_residual_ln(mixed, w_out, x, mod4, l, 2, ln_g[l, 0], ln_b[l, 0], alpha)
        x = _mlp_block(x, mod4, l, w1, mlp_b1[l], w2, mlp_b2[l], ln_g[l, 1], ln_b[l, 1], alpha)
    return x
```

```python
import functools

import jax
import jax.numpy as jnp
from jax import lax
from jax.experimental import pallas as pl
from jax.experimental.pallas import tpu as pltpu

F32 = jnp.float32
BF16 = jnp.bfloat16

GLA_HEADS = 4
GLA_GATE_RANK = 16
GLA_GATE_TAU = 16.0
GLA_CHUNK = 64
SB_HEADS = 16
N_MOD = 6
LN_EPS = 1e-5
RMS_EPS = 1e-5

LANES = 128
MOD_ROWS = 8
VMEM_LIMIT_CAP = 58 << 20
VMEM_HEADROOM = 6 << 20


def _params(semantics, block_bytes):
    limit = min(int(block_bytes) + VMEM_HEADROOM, VMEM_LIMIT_CAP)
    return pltpu.CompilerParams(dimension_semantics=semantics, vmem_limit_bytes=limit)


def _tile(n, want):
    t = min(n, want)
    assert n % t == 0, (n, want)
    return t


def _row_slices(n_rows, n_split):
    assert n_split >= 1 and n_rows % n_split == 0
    step = n_rows // n_split
    return [slice(i * step, (i + 1) * step) for i in range(n_split)]


def _weight_shape(w):
    return (w[0] if isinstance(w, tuple) else w).shape[-2:]


def _weight_operand(w, block, index_map, **spec_kwargs):
    if isinstance(w, tuple):
        stacked, entry = w
        return stacked, pl.BlockSpec((None,) + block, lambda *g: (entry,) + index_map(*g), **spec_kwargs)
    return w, pl.BlockSpec(block, index_map, **spec_kwargs)


def _dot(a, b):
    return jnp.dot(a, b, preferred_element_type=F32)


def _dot_nt(a, b):
    return lax.dot_general(a, b, (((1,), (1,)), ((), ())), preferred_element_type=F32)


def _dot_tn(a, b):
    return lax.dot_general(a, b, (((0,), (0,)), ((), ())), preferred_element_type=F32)


def _split_bf16(x):
    hi = x.astype(BF16)
    lo = (x - hi.astype(F32)).astype(BF16)
    return hi, lo


BF16_SUBLANES = 16


def _cast_divides(weights, n_steps):
    return all(w.shape[-2] % n_steps == 0 and (w.shape[-2] // n_steps) % BF16_SUBLANES == 0
               for w, _ in weights)


def _cast_specs(weights, n_steps, step_index):
    in_specs, out_specs, out_shapes, nbytes = [], [], [], 0
    for w, layer in weights:
        R, C = w.shape[-2:]
        rows = R // n_steps
        assert R % n_steps == 0 and rows % BF16_SUBLANES == 0, (w.shape, n_steps)
        if layer is None:
            in_specs.append(pl.BlockSpec((rows, C), lambda *g: (step_index(*g), 0)))
        else:
            in_specs.append(pl.BlockSpec((None, rows, C), lambda *g, _l=layer: (_l, step_index(*g), 0)))
        out_specs.append(pl.BlockSpec((rows, C), lambda *g: (step_index(*g), 0)))
        out_shapes.append(jax.ShapeDtypeStruct((R, C), BF16))
        nbytes += 2 * rows * C * (4 + 2)
    return in_specs, out_specs, out_shapes, nbytes


def _cast_slabs(src_refs, dst_refs):
    for src, dst in zip(src_refs, dst_refs):
        dst[...] = src[...].astype(dst.dtype)


def _layer_norm(z, g, b):
    mu = jnp.mean(z, axis=-1, keepdims=True)
    zc = z - mu
    var = jnp.mean(zc * zc, axis=-1, keepdims=True)
    return zc * lax.rsqrt(var + LN_EPS) * g + b


def _mod_kernel(c_ref, w_ref, b_ref, o_ref):
    c = c_ref[...]
    cs = c * (1.0 / (1.0 + jnp.exp(-c)))
    hi, lo = _split_bf16(cs)
    y = _dot(jnp.concatenate([hi, lo], axis=0), w_ref[...].astype(BF16))
    o_ref[...] = y[:MOD_ROWS] + y[MOD_ROWS:] + b_ref[...]


def _mod_vectors(c_pad, w, b):
    L, D, N = w.shape
    tn = _tile(N, 2048)
    blocks = 2 * (D * tn * 4) + MOD_ROWS * D * 4 + 4 * MOD_ROWS * tn * 4
    return pl.pallas_call(
        _mod_kernel,
        grid=(L, N // tn),
        in_specs=[
            pl.BlockSpec((MOD_ROWS, D), lambda l, j: (0, 0)),
            pl.BlockSpec((None, D, tn), lambda l, j: (l, 0, j)),
            pl.BlockSpec((None, 1, tn), lambda l, j: (l, 0, j)),
        ],
        out_specs=pl.BlockSpec((None, MOD_ROWS, tn), lambda l, j: (l, 0, j)),
        out_shape=jax.ShapeDtypeStruct((L, MOD_ROWS, N), F32),
        compiler_params=_params(("arbitrary", "arbitrary"), blocks),
        name="adaln_vectors",
    )(c_pad, w, b.reshape(L, 1, N))


def _mod_spec(layer, slot, D, n_grid):
    if n_grid == 2:
        return pl.BlockSpec((None, None, 1, D), lambda b, i: (layer, b, 0, slot))
    return pl.BlockSpec((None, None, 1, D), lambda b, i, j: (layer, b, 0, slot))


MODMM_ROW_GROUP = 256
MODMM_MIN_HEADROOM = 2 << 20


def _modmm_kernel(x_ref, sh_ref, sc_ref, w_ref, *rest, out_scale, with_side, n_cast):
    rest = list(rest)
    h_ref = rest.pop()
    ws_ref = rest.pop(0) if with_side else None
    cast_src = [rest.pop(0) for _ in range(n_cast)]
    o_ref = rest.pop(0)
    side_ref = rest.pop(0) if with_side else None
    _cast_slabs(cast_src, rest)
    groups = _row_slices(x_ref.shape[0], x_ref.shape[0] // MODMM_ROW_GROUP)

    def emit(sl):
        y = _dot(h_ref[sl, :], w_ref[...])
        if out_scale != 1.0:
            y = y * out_scale
        o_ref[sl, :] = y.astype(o_ref.dtype)

    @pl.when(pl.program_id(2) == 0)
    def _():
        for sl in groups:
            h_ref[sl, :] = (x_ref[sl, :] * (1.0 + sc_ref[...]) + sh_ref[...]).astype(BF16)
            emit(sl)
            if with_side:
                side_ref[sl, :] = _dot(h_ref[sl, :], ws_ref[...])

    @pl.when(pl.program_id(2) > 0)
    def _():
        for sl in groups:
            emit(sl)


def _mod_matmul(x, mod4, layer, sh_slot, sc_slot, w, out_dtype, out_scale=1.0, n_cols=None, w_side=None,
                cast_weights=()):
    B, T, D = x.shape
    N = _weight_shape(w)[1] if n_cols is None else n_cols
    tm = _tile(T, 1024)
    osz = jnp.dtype(out_dtype).itemsize
    S = 0 if w_side is None else w_side.shape[1]
    n_i = T // tm
    col_tiles = [t for t in (2048, 1536, 1024, 512, 256, 128) if N % t == 0]

    def plan(tn):
        n_j = N // tn
        if not _cast_divides(cast_weights, B * n_i * n_j):
            return None, VMEM_LIMIT_CAP
        c = _cast_specs(cast_weights, B * n_i * n_j, lambda b, i, j: (b * n_i + i) * n_j + j)
        nbytes = (2 * tm * D * 4 + 2 * D * tn * 2 + 2 * tm * tn * osz + tm * D * 2
                  + 2 * MODMM_ROW_GROUP * tn * 4 + 2 * D * S * 2 + 2 * tm * S * 4 + c[3])
        return c, nbytes

    tn = next(t for t in col_tiles if plan(t)[1] + MODMM_MIN_HEADROOM <= VMEM_LIMIT_CAP)
    (c_in, c_out, c_shapes, _), blocks = plan(tn)
    w_arr, w_spec = _weight_operand(w, (D, tn), lambda b, i, j: (0, j))
    in_specs = [
        pl.BlockSpec((None, tm, D), lambda b, i, j: (b, i, 0)),
        _mod_spec(layer, sh_slot, D, 3),
        _mod_spec(layer, sc_slot, D, 3),
        w_spec,
    ]
    out_specs = [pl.BlockSpec((None, tm, tn), lambda b, i, j: (b, i, j))]
    out_shape = [jax.ShapeDtypeStruct((B, T, N), out_dtype)]
    args = [x, mod4, mod4, w_arr]
    if w_side is not None:
        in_specs.append(pl.BlockSpec((D, S), lambda b, i, j: (0, 0)))
        out_specs.append(pl.BlockSpec((None, tm, S), lambda b, i, j: (b, i, 0)))
        out_shape.append(jax.ShapeDtypeStruct((B, T, S), F32))
        args.append(w_side)
    outs = pl.pallas_call(
        functools.partial(_modmm_kernel, out_scale=out_scale, with_side=w_side is not None,
                          n_cast=len(cast_weights)),
        grid=(B, n_i, N // tn),
        in_specs=in_specs + c_in,
        out_specs=out_specs + c_out,
        out_shape=out_shape + c_shapes,
        scratch_shapes=[pltpu.VMEM((tm, D), BF16)],
        compiler_params=_params(("arbitrary", "arbitrary", "arbitrary"), blocks),
        name="modulated_matmul",
    )(*args, *[cw for cw, _ in cast_weights])
    n_main = 1 if w_side is None else 2
    return outs[0], (None if w_side is None else outs[1]), tuple(outs[n_main:])


GLA_CUMSUM_ROWS = 2 * GLA_CHUNK


def _gla_gate_kernel(gl_ref, wup_ref, bg_ref, tri_ref, o_ref):
    n = GLA_CUMSUM_ROWS
    for sl in _row_slices(gl_ref.shape[0], gl_ref.shape[0] // n):
        g = _dot(gl_ref[sl, :].astype(BF16), wup_ref[...]) + bg_ref[...]
        log_a = (jnp.minimum(g, 0.0) - jnp.log(1.0 + jnp.exp(-jnp.abs(g)))) * (1.0 / GLA_GATE_TAU)
        hi, lo = _split_bf16(log_a)
        o_ref[sl, :] = _dot(tri_ref[...], jnp.concatenate([hi, lo], axis=0))


def _chunk_cumsum_matrix():
    n = GLA_CUMSUM_ROWS
    row = lax.broadcasted_iota(jnp.int32, (n, n), 0)
    col = lax.broadcasted_iota(jnp.int32, (n, n), 1)
    tri = ((col <= row) & (row // GLA_CHUNK == col // GLA_CHUNK)).astype(BF16)
    return jnp.concatenate([tri, tri], axis=1)


def _gla_cum_log_decay(gl, w_up, b_gate):
    B, T, R = gl.shape
    QK = w_up.shape[1]
    tm = _tile(T, 1024)
    n = GLA_CUMSUM_ROWS
    assert tm % n == 0
    blocks = 2 * tm * R * 4 + 2 * R * QK * 2 + 2 * n * 2 * n * 2 + 2 * tm * QK * 4 + 8 * n * QK * 4
    return pl.pallas_call(
        _gla_gate_kernel,
        grid=(B, T // tm),
        in_specs=[
            pl.BlockSpec((None, tm, R), lambda b, i: (b, i, 0)),
            pl.BlockSpec((R, QK), lambda b, i: (0, 0)),
            pl.BlockSpec((1, QK), lambda b, i: (0, 0)),
            pl.BlockSpec((n, 2 * n), lambda b, i: (0, 0)),
        ],
        out_specs=pl.BlockSpec((None, tm, QK), lambda b, i: (b, i, 0)),
        out_shape=jax.ShapeDtypeStruct((B, T, QK), F32),
        compiler_params=_params(("arbitrary", "arbitrary"), blocks),
        name="gla_cum_log_decay",
    )(gl, w_up, b_gate.reshape(1, QK), _chunk_cumsum_matrix())


GLA_HEADS_PER_STEP = 2


def _gla_rec_kernel(q_ref, k_ref, v_ref, r_ref, bc_ref, ng_ref, o_ref, st_ref, *, n_chunks, dk, dv):
    C = GLA_CHUNK

    @pl.when(pl.program_id(2) == 0)
    def _():
        st_ref[...] = jnp.zeros_like(st_ref)

    crow = lax.broadcasted_iota(jnp.int32, (C, C), 0)
    ccol = lax.broadcasted_iota(jnp.int32, (C, C), 1)
    causal = ccol <= crow
    prow = lax.broadcasted_iota(jnp.int32, (C, 2 * C), 0)
    pcol = lax.broadcasted_iota(jnp.int32, (C, 2 * C), 1)
    causal_second = pcol <= prow + C
    ng = ng_ref[...]

    bc, k, q_dec, k_intra = [], [], [], []
    for h in range(GLA_HEADS_PER_STEP):
        ks = slice(h * dk, (h + 1) * dk)
        bc.append(bc_ref[:, ks])
        k.append(k_ref[:, ks])
        q_dec.append(q_ref[:, ks] * (dk ** -0.5) * jnp.exp(bc[h]))
        k_intra.append((k[h] * jnp.exp(-bc[h])).astype(BF16))

    for g in range(n_chunks // 2):
        c1 = slice(2 * g * C, (2 * g + 1) * C)
        c2 = slice((2 * g + 1) * C, (2 * g + 2) * C)
        both = slice(2 * g * C, (2 * g + 2) * C)
        for h in range(GLA_HEADS_PER_STEP):
            vs = slice(h * dv, (h + 1) * dv)
            b1 = bc[h][(2 * g + 1) * C - 1:(2 * g + 1) * C, :]
            b2 = bc[h][(2 * g + 2) * C - 1:(2 * g + 2) * C, :]
            k1_mid = (k[h][c1] * jnp.exp(b1 - bc[h][c1])).astype(BF16)
            k1_end = k[h][c1] * jnp.exp((b1 + b2) - bc[h][c1])
            k2_end = k[h][c2] * jnp.exp(b2 - bc[h][c2])
            k_end = jnp.concatenate([k1_end, k2_end], axis=0).astype(BF16)
            q1 = q_dec[h][c1].astype(BF16)
            q2 = q_dec[h][c2].astype(BF16)
            q2_from_start = (q_dec[h][c2] * jnp.exp(b1)).astype(BF16)
            v_both = v_ref[both, vs].astype(BF16)
            s1 = jnp.where(causal, _dot_nt(q1, k_intra[h][c1]), 0.0)
            s2 = jnp.where(causal_second,
                           _dot_nt(q2, jnp.concatenate([k1_mid, k_intra[h][c2]], axis=0)), 0.0)
            st = st_ref[h]
            st_bf = st.astype(BF16)
            o1 = _dot(s1.astype(BF16), v_both[:C]) + _dot_nt(q1, st_bf)
            o2 = _dot(s2.astype(BF16), v_both) + _dot_nt(q2_from_start, st_bf)
            st_ref[h] = st * jnp.exp(b1 + b2) + _dot_tn(v_both, k_end)
            for sl, o in ((c1, o1), (c2, o2)):
                o = o * lax.rsqrt(jnp.mean(o * o, axis=-1, keepdims=True) + RMS_EPS) * ng
                r = r_ref[sl, vs]
                o_ref[sl, vs] = (o * (r * (1.0 / (1.0 + jnp.exp(-r))))).astype(o_ref.dtype)


def _gla_recurrence(proj, bc, norm_g):
    B, T, _ = proj.shape
    QK = bc.shape[-1]
    dk = QK // GLA_HEADS
    dv = norm_g.shape[-1]
    V = dv * GLA_HEADS
    hp = GLA_HEADS_PER_STEP
    assert dk % LANES == 0 and dv % LANES == 0 and GLA_HEADS % hp == 0
    wk, wv = hp * dk, hp * dv
    assert QK % wk == 0 and (2 * QK) % wv == 0 and (2 * QK + V) % wv == 0
    n_chunks = min(16, T // GLA_CHUNK)
    tb = n_chunks * GLA_CHUNK
    assert T % tb == 0 and n_chunks % 2 == 0
    kk = QK // wk
    kv, kr = 2 * QK // wv, (2 * QK + V) // wv
    blocks = (2 * (3 * tb * wk * 4 + 2 * tb * wv * 4 + tb * wv * 2) + hp * dv * dk * 4 + 8 * tb * wk * 4)
    return pl.pallas_call(
        functools.partial(_gla_rec_kernel, n_chunks=n_chunks, dk=dk, dv=dv),
        grid=(B, GLA_HEADS // hp, T // tb),
        in_specs=[
            pl.BlockSpec((None, tb, wk), lambda b, h, t: (b, t, h)),
            pl.BlockSpec((None, tb, wk), lambda b, h, t: (b, t, kk + h)),
            pl.BlockSpec((None, tb, wv), lambda b, h, t: (b, t, kv + h)),
            pl.BlockSpec((None, tb, wv), lambda b, h, t: (b, t, kr + h)),
            pl.BlockSpec((None, tb, wk), lambda b, h, t: (b, t, h)),
            pl.BlockSpec((1, dv), lambda b, h, t: (0, 0)),
        ],
        out_specs=pl.BlockSpec((None, tb, wv), lambda b, h, t: (b, t, h)),
        out_shape=jax.ShapeDtypeStruct((B, T, V), BF16),
        scratch_shapes=[pltpu.VMEM((hp, dv, dk), F32)],
        compiler_params=_params(("arbitrary", "arbitrary", "arbitrary"), blocks),
        name="gla_recurrence",
    )(proj, proj, proj, proj, bc, norm_g.reshape(1, dv))


def _proj_ln_kernel(a_ref, w_ref, x_ref, g_ref, lg_ref, lb_ref, o_ref, *, alpha):
    groups = _row_slices(a_ref.shape[0], 4)

    def finish(sl, y):
        z = alpha * x_ref[sl, :] + (1.0 + g_ref[...]) * y
        o_ref[sl, :] = _layer_norm(z, lg_ref[...], lb_ref[...])

    pending = None
    for sl in groups:
        y = _dot(a_ref[sl, :], w_ref[...])
        if pending is not None:
            finish(*pending)
        pending = (sl, y)
    finish(*pending)


def _proj_residual_ln(a, w, x, mod4, layer, gate_slot, ln_g, ln_b, alpha):
    B, T, D = x.shape
    K = a.shape[-1]
    tm = _tile(T, 1024)
    blocks = 2 * tm * K * 2 + K * D * 2 + 4 * tm * D * 4 + 3 * (tm // 4) * D * 4
    w, w_spec = _weight_operand(w, (K, D), lambda b, i: (0, 0), pipeline_mode=pl.Buffered(1))
    return pl.pallas_call(
        functools.partial(_proj_ln_kernel, alpha=alpha),
        grid=(B, T // tm),
        in_specs=[
            pl.BlockSpec((None, tm, K), lambda b, i: (b, i, 0)),
            w_spec,
            pl.BlockSpec((None, tm, D), lambda b, i: (b, i, 0)),
            _mod_spec(layer, gate_slot, D, 2),
            pl.BlockSpec((1, D), lambda b, i: (0, 0)),
            pl.BlockSpec((1, D), lambda b, i: (0, 0)),
        ],
        out_specs=pl.BlockSpec((None, tm, D), lambda b, i: (b, i, 0)),
        out_shape=jax.ShapeDtypeStruct((B, T, D), F32),
        compiler_params=_params(("arbitrary", "arbitrary"), blocks),
        name="proj_residual_ln",
    )(a, w, x, mod4, ln_g.reshape(1, D), ln_b.reshape(1, D))


MLP_TM = 1024
MLP_TF = 1024
MLP_ROW_GROUP = 256

def _mlp_kernel(x_ref, sh_ref, sc_ref, g_ref, w1_ref, b1_ref, w2_ref, b2_ref, lg_ref, lb_ref,
                o_ref, h_ref, *, alpha):
    f = pl.program_id(2)
    last = pl.num_programs(2) - 1
    groups = _row_slices(x_ref.shape[0], x_ref.shape[0] // MLP_ROW_GROUP)

    def contribution(sl):
        u = jnp.maximum(_dot(h_ref[sl, :], w1_ref[...]) + b1_ref[...], 0.0)
        return _dot((u * u).astype(BF16), w2_ref[...])

    @pl.when(f == 0)
    def _():
        for sl in groups:
            h_ref[sl, :] = (x_ref[sl, :] * (1.0 + sc_ref[...]) + sh_ref[...]).astype(BF16)
            o_ref[sl, :] = contribution(sl)

    @pl.when(jnp.logical_and(f > 0, f < last))
    def _():
        for sl in groups:
            o_ref[sl, :] += contribution(sl)

    @pl.when(f == last)
    def _():
        for sl in groups:
            y = o_ref[sl, :] + contribution(sl) + b2_ref[...]
            z = alpha * x_ref[sl, :] + (1.0 + g_ref[...]) * y
            o_ref[sl, :] = _layer_norm(z, lg_ref[...], lb_ref[...])


def _mlp_block(x, mod4, layer, w1, b1, w2, b2, ln_g, ln_b, alpha):
    B, T, D = x.shape
    F = _weight_shape(w1)[1]
    tm, tf = _tile(T, MLP_TM), _tile(F, MLP_TF)
    assert F // tf >= 2, "the first and the last d_ff step must be distinct grid steps"
    assert tm % MLP_ROW_GROUP == 0
    blocks = (4 * tm * D * 4 + tm * D * 2 + 4 * D * tf * 2 + tm * tf * 6 + 3 * MLP_ROW_GROUP * D * 4)
    w1, w1_spec = _weight_operand(w1, (D, tf), lambda b, i, f: (0, f))
    w2, w2_spec = _weight_operand(w2, (tf, D), lambda b, i, f: (f, 0))
    return pl.pallas_call(
        functools.partial(_mlp_kernel, alpha=alpha),
        grid=(B, T // tm, F // tf),
        in_specs=[
            pl.BlockSpec((None, tm, D), lambda b, i, f: (b, i, 0)),
            _mod_spec(layer, 3, D, 3),
            _mod_spec(layer, 4, D, 3),
            _mod_spec(layer, 5, D, 3),
            w1_spec,
            pl.BlockSpec((1, tf), lambda b, i, f: (0, f)),
            w2_spec,
            pl.BlockSpec((1, D), lambda b, i, f: (0, 0)),
            pl.BlockSpec((1, D), lambda b, i, f: (0, 0)),
            pl.BlockSpec((1, D), lambda b, i, f: (0, 0)),
        ],
        out_specs=pl.BlockSpec((None, tm, D), lambda b, i, f: (b, i, 0)),
        out_shape=jax.ShapeDtypeStruct((B, T, D), F32),
        scratch_shapes=[pltpu.VMEM((tm, D), BF16)],
        compiler_params=_params(("arbitrary", "arbitrary", "arbitrary"), blocks),
        name="mlp_block",
    )(x, mod4, mod4, mod4, w1, b1.reshape(1, F), w2, b2.reshape(1, D), ln_g.reshape(1, D), ln_b.reshape(1, D))


SB_BLK = 256
LOG2_E = 1.4426950408889634
SB_DEAD_LOG2 = 160.0
SB_NO_KEYS = 1e30


def _sb_attn_kernel(q_ref, k_ref, v_ref, *rest, n_pairs, n_cast):
    o_ref = rest[n_cast]
    _cast_slabs(rest[:n_cast], rest[n_cast + 1:])
    _sb_attn_body(q_ref, k_ref, v_ref, o_ref, n_pairs)


def _sb_attn_body(q_ref, k_ref, v_ref, o_ref, n_pairs):
    n = SB_BLK
    dh = q_ref.shape[-1]
    jrow = lax.broadcasted_iota(jnp.int32, (n, n), 0)
    jcol = lax.broadcasted_iota(jnp.int32, (n, n), 1)
    later = (jrow > jcol).astype(BF16)
    diag_mask = jcol < jrow

    def scores(q, blk):
        return _dot_nt(q, k_ref[pl.ds(pl.multiple_of(blk * n, n), n), :])

    def log_terms(z, mask):
        neg_abs = lax.bitcast_convert_type(
            lax.bitcast_convert_type(z, jnp.uint32) | jnp.uint32(0x80000000), F32)
        sp = jnp.maximum(z, 0.0) + jnp.log(1.0 + jnp.exp2(neg_abs)) * LOG2_E
        lb = z - sp
        if mask is not None:
            sp = jnp.where(mask, sp, 0.0)
        return sp.astype(BF16), lb, sp[:, 0:1]

    def suffix(hilo):
        return _dot(hilo, later)

    def weights(lb, sfx, r, sp0, mask):
        a = jnp.exp2(lb - sfx if r is None else lb - (sfx + r))
        if mask is not None:
            a = jnp.where(mask, a, 0.0)
        tot = sfx[:, 0:1] + sp0
        return a.astype(BF16), tot if r is None else r + tot

    def values(a, blk):
        return _dot(a, v_ref[pl.ds(pl.multiple_of(blk * n, n), n), :])

    def step(qa, qb, blk_a, blk_b, ra, rb, acc_a, acc_b, mask):
        za = scores(qa, blk_a)
        zb = scores(qb, blk_b)
        hla, lba, spa0 = log_terms(za, mask)
        sfa = suffix(hla)
        hlb, lbb, spb0 = log_terms(zb, mask)
        aa, ra = weights(lba, sfa, ra, spa0, mask)
        sfb = suffix(hlb)
        acc_a = acc_a + values(aa, blk_a)
        ab, rb = weights(lbb, sfb, rb, spb0, mask)
        acc_b = acc_b + values(ab, blk_b)
        return ra, rb, acc_a, acc_b

    def pair_body(p, _):
        blk_a0 = 2 * p
        blk_b0 = blk_a0 + 1
        row_a = pl.multiple_of(blk_a0 * n, n)
        row_b = pl.multiple_of(blk_b0 * n, n)
        qa = q_ref[pl.ds(row_a, n), :]
        qb = q_ref[pl.ds(row_b, n), :]

        blk_a1 = jnp.maximum(blk_a0 - 1, 0)
        blk_b1 = blk_a0
        za0 = scores(qa, blk_a0)
        zb0 = scores(qb, blk_b0)
        za1 = scores(qa, blk_a1)
        zb1 = scores(qb, blk_b1)
        hla0, lba0, spa0 = log_terms(za0, diag_mask)
        sfa0 = suffix(hla0)
        hlb0, lbb0, spb0 = log_terms(zb0, diag_mask)
        sfb0 = suffix(hlb0)
        hla1, lba1, spa1 = log_terms(za1, None)
        aa0, ra = weights(lba0, sfa0, None, spa0, diag_mask)
        sfa1 = suffix(hla1)
        acc_a = values(aa0, blk_a0)
        hlb1, lbb1, spb1 = log_terms(zb1, None)
        ab0, rb = weights(lbb0, sfb0, None, spb0, diag_mask)
        sfb1 = suffix(hlb1)
        acc_b = values(ab0, blk_b0)
        ra = jnp.where(p > 0, ra, SB_NO_KEYS)
        aa1, ra = weights(lba1, sfa1, ra, spa1, None)
        acc_a = acc_a + values(aa1, blk_a1)
        ab1, rb = weights(lbb1, sfb1, rb, spb1, None)
        acc_b = acc_b + values(ab1, blk_b1)

        def alive(ra, rb):
            return jnp.minimum(jnp.min(ra), jnp.min(rb)) < SB_DEAD_LOG2

        def cond(carry):
            i, go = carry[0], carry[1]
            return jnp.logical_and(i <= blk_b0, go)

        def body(carry):
            i, _, ra, rb, acc_a, acc_b = carry
            has_a = i <= blk_a0
            ra = jnp.where(has_a, ra, SB_NO_KEYS)
            ra, rb, acc_a, acc_b = step(qa, qb, jnp.maximum(blk_a0 - i, 0), blk_b0 - i,
                                        ra, rb, acc_a, acc_b, None)
            return i + 1, alive(ra, rb), ra, rb, acc_a, acc_b

        carry = lax.while_loop(cond, body, (jnp.int32(2), alive(ra, rb), ra, rb, acc_a, acc_b))
        o_ref[pl.ds(row_a, n), :] = carry[4].astype(o_ref.dtype)
        o_ref[pl.ds(row_b, n), :] = carry[5].astype(o_ref.dtype)
        return 0

    lax.fori_loop(0, n_pairs, pair_body, 0)


def _sb_attention(q, kv, cast_weights=()):
    B, T, HD = q.shape
    dh = HD // SB_HEADS
    assert dh % LANES == 0 and T % (2 * SB_BLK) == 0
    c_in, c_out, c_shapes, c_bytes = _cast_specs(cast_weights, B * SB_HEADS, lambda b, h: b * SB_HEADS + h)
    blocks = 8 * T * dh * 2 + 24 * SB_BLK * SB_BLK * 4 + c_bytes
    outs = pl.pallas_call(
        functools.partial(_sb_attn_kernel, n_pairs=T // (2 * SB_BLK), n_cast=len(cast_weights)),
        grid=(B, SB_HEADS),
        in_specs=[
            pl.BlockSpec((None, T, dh), lambda b, h: (b, 0, h)),
            pl.BlockSpec((None, T, dh), lambda b, h: (b, 0, h)),
            pl.BlockSpec((None, T, dh), lambda b, h: (b, 0, SB_HEADS + h)),
        ] + c_in,
        out_specs=[pl.BlockSpec((None, T, dh), lambda b, h: (b, 0, h))] + c_out,
        out_shape=[jax.ShapeDtypeStruct((B, T, HD), BF16)] + c_shapes,
        compiler_params=_params(("arbitrary", "arbitrary"), blocks),
        name="stick_breaking_attention",
    )(q, kv, kv, *[w for w, _ in cast_weights])
    return outs[0], tuple(outs[1:])


def kernel(x, c, ada_w, ada_b, ln_g, ln_b, gla_w_in, gla_w_gate_up, gla_b_gate, gla_norm_g, gla_w_out, kv_ada_w, kv_ada_b, w_kv, sb_w_q, sb_w_out, mlp_w1, mlp_b1, mlp_w2, mlp_b2):
    B, T, D = x.shape
    depth = ada_w.shape[0]
    n_a = gla_w_in.shape[0]
    alpha = (2.0 * depth) ** 0.25
    QK = gla_w_gate_up.shape[-1]
    V = gla_w_out.shape[1]
    n_main = 2 * QK + 2 * V
    dh = sb_w_q.shape[-1] // SB_HEADS

    c_pad = jnp.pad(c, ((0, MOD_ROWS - B), (0, 0)))
    mod4 = _mod_vectors(c_pad, ada_w, ada_b).reshape(depth, MOD_ROWS, 1, N_MOD * D)
    kv_mod4 = _mod_vectors(c_pad, kv_ada_w[None], kv_ada_b[None]).reshape(1, MOD_ROWS, 1, 2 * D)

    gla_w_in_bf = gla_w_in.astype(BF16)
    gla_w_out_bf = gla_w_out.astype(BF16)
    w_kv_bf = w_kv.astype(BF16)
    sb_w_q_bf = sb_w_q.astype(BF16)

    kv = None
    for l in range(depth):
        if l < n_a:
            pad = LANES - GLA_GATE_RANK
            w_gate = jnp.pad(gla_w_in_bf[l, :, n_main:], ((0, 0), (0, pad)))
            w_up = jnp.pad(gla_w_gate_up[l], ((0, pad), (0, 0))).astype(BF16)
            proj, gl, (w1, w2) = _mod_matmul(
                x, mod4, l, 0, 1, (gla_w_in_bf, l), F32, n_cols=n_main, w_side=w_gate,
                cast_weights=((mlp_w1, l), (mlp_w2, l)))
            w_out = (gla_w_out_bf, l)
            bc = _gla_cum_log_decay(gl, w_up, gla_b_gate[l])
            mixed = _gla_recurrence(proj, bc, gla_norm_g[l])
        else:
            j = l - n_a
            if kv is None:
                kv, _, _ = _mod_matmul(x, kv_mod4, 0, 0, 1, w_kv_bf, BF16)
            q, _, _ = _mod_matmul(x, mod4, l, 0, 1, (sb_w_q_bf, j), BF16, out_scale=dh ** -0.5 * LOG2_E)
            mixed, (w_out, w1, w2) = _sb_attention(q, kv, ((sb_w_out, j), (mlp_w1, l), (mlp_w2, l)))
        x = _proj_residual_ln(mixed, w_out, x, mod4, l, 2, ln_g[l, 0], ln_b[l, 0], alpha)
        x = _mlp_block(x, mod4, l, w1, mlp_b1[l], w2, mlp_b2[l], ln_g[l, 1], ln_b[l, 1], alpha)
    return x
```

```python
import functools

import jax
import jax.numpy as jnp
from jax import lax
from jax.experimental import pallas as pl
from jax.experimental.pallas import tpu as pltpu

F32 = jnp.float32
BF16 = jnp.bfloat16

GLA_HEADS = 4
GLA_GATE_RANK = 16
GLA_GATE_TAU = 16.0
GLA_CHUNK = 64
SB_HEADS = 16
N_MOD = 6
LN_EPS = 1e-5
RMS_EPS = 1e-5

LANES = 128
MOD_ROWS = 8
VMEM_LIMIT_CAP = 58 << 20
VMEM_HEADROOM = 6 << 20


def _params(semantics, block_bytes):
    limit = min(int(block_bytes) + VMEM_HEADROOM, VMEM_LIMIT_CAP)
    return pltpu.CompilerParams(dimension_semantics=semantics, vmem_limit_bytes=limit)


def _tile(n, want):
    t = min(n, want)
    assert n % t == 0, (n, want)
    return t


def _row_slices(n_rows, n_split):
    assert n_split >= 1 and n_rows % n_split == 0
    step = n_rows // n_split
    return [slice(i * step, (i + 1) * step) for i in range(n_split)]


def _weight_shape(w):
    return (w[0] if isinstance(w, tuple) else w).shape[-2:]


def _weight_operand(w, block, index_map, **spec_kwargs):
    if isinstance(w, tuple):
        stacked, entry = w
        return stacked, pl.BlockSpec((None,) + block, lambda *g: (entry,) + index_map(*g), **spec_kwargs)
    return w, pl.BlockSpec(block, index_map, **spec_kwargs)


def _dot(a, b):
    return jnp.dot(a, b, preferred_element_type=F32)


def _dot_nt(a, b):
    return lax.dot_general(a, b, (((1,), (1,)), ((), ())), preferred_element_type=F32)


def _dot_tn(a, b):
    return lax.dot_general(a, b, (((0,), (0,)), ((), ())), preferred_element_type=F32)


def _split_bf16(x):
    hi = x.astype(BF16)
    lo = (x - hi.astype(F32)).astype(BF16)
    return hi, lo


BF16_SUBLANES = 16


def _cast_divides(weights, n_steps):
    return all(w.shape[-2] % n_steps == 0 and (w.shape[-2] // n_steps) % BF16_SUBLANES == 0
               for w, _ in weights)


def _cast_specs(weights, n_steps, step_index):
    in_specs, out_specs, out_shapes, nbytes = [], [], [], 0
    for w, layer in weights:
        R, C = w.shape[-2:]
        rows = R // n_steps
        assert R % n_steps == 0 and rows % BF16_SUBLANES == 0, (w.shape, n_steps)
        if layer is None:
            in_specs.append(pl.BlockSpec((rows, C), lambda *g: (step_index(*g), 0)))
        else:
            in_specs.append(pl.BlockSpec((None, rows, C), lambda *g, _l=layer: (_l, step_index(*g), 0)))
        out_specs.append(pl.BlockSpec((rows, C), lambda *g: (step_index(*g), 0)))
        out_shapes.append(jax.ShapeDtypeStruct((R, C), BF16))
        nbytes += 2 * rows * C * (4 + 2)
    return in_specs, out_specs, out_shapes, nbytes


def _cast_slabs(src_refs, dst_refs):
    for src, dst in zip(src_refs, dst_refs):
        dst[...] = src[...].astype(dst.dtype)


def _layer_norm(z, g, b):
    mu = jnp.mean(z, axis=-1, keepdims=True)
    zc = z - mu
    var = jnp.mean(zc * zc, axis=-1, keepdims=True)
    return zc * lax.rsqrt(var + LN_EPS) * g + b


def _mod_kernel(c_ref, w_ref, b_ref, o_ref):
    c = c_ref[...]
    cs = c * (1.0 / (1.0 + jnp.exp(-c)))
    hi, lo = _split_bf16(cs)
    y = _dot(jnp.concatenate([hi, lo], axis=0), w_ref[...].astype(BF16))
    o_ref[...] = y[:MOD_ROWS] + y[MOD_ROWS:] + b_ref[...]


def _mod_vectors(c_pad, w, b):
    L, D, N = w.shape
    tn = _tile(N, 2048)
    blocks = 2 * (D * tn * 4) + MOD_ROWS * D * 4 + 4 * MOD_ROWS * tn * 4
    return pl.pallas_call(
        _mod_kernel,
        grid=(L, N // tn),
        in_specs=[
            pl.BlockSpec((MOD_ROWS, D), lambda l, j: (0, 0)),
            pl.BlockSpec((None, D, tn), lambda l, j: (l, 0, j)),
            pl.BlockSpec((None, 1, tn), lambda l, j: (l, 0, j)),
        ],
        out_specs=pl.BlockSpec((None, MOD_ROWS, tn), lambda l, j: (l, 0, j)),
        out_shape=jax.ShapeDtypeStruct((L, MOD_ROWS, N), F32),
        compiler_params=_params(("arbitrary", "arbitrary"), blocks),
        name="adaln_vectors",
    )(c_pad, w, b.reshape(L, 1, N))


def _mod_spec(layer, slot, D, n_grid):
    if n_grid == 2:
        return pl.BlockSpec((None, None, 1, D), lambda b, i: (layer, b, 0, slot))
    return pl.BlockSpec((None, None, 1, D), lambda b, i, j: (layer, b, 0, slot))


MODMM_ROW_GROUP = 256
MODMM_MIN_HEADROOM = 2 << 20


def _modmm_kernel(x_ref, sh_ref, sc_ref, w_ref, *rest, out_scale, with_side, n_cast):
    rest = list(rest)
    h_ref = rest.pop()
    ws_ref = rest.pop(0) if with_side else None
    cast_src = [rest.pop(0) for _ in range(n_cast)]
    o_ref = rest.pop(0)
    side_ref = rest.pop(0) if with_side else None
    _cast_slabs(cast_src, rest)
    groups = _row_slices(x_ref.shape[0], x_ref.shape[0] // MODMM_ROW_GROUP)

    def emit(sl):
        y = _dot(h_ref[sl, :], w_ref[...])
        if out_scale != 1.0:
            y = y * out_scale
        o_ref[sl, :] = y.astype(o_ref.dtype)

    @pl.when(pl.program_id(2) == 0)
    def _():
        for sl in groups:
            h_ref[sl, :] = (x_ref[sl, :] * (1.0 + sc_ref[...]) + sh_ref[...]).astype(BF16)
            emit(sl)
            if with_side:
                side_ref[sl, :] = _dot(h_ref[sl, :], ws_ref[...])

    @pl.when(pl.program_id(2) > 0)
    def _():
        for sl in groups:
            emit(sl)


def _mod_matmul(x, mod4, layer, sh_slot, sc_slot, w, out_dtype, out_scale=1.0, n_cols=None, w_side=None,
                cast_weights=()):
    B, T, D = x.shape
    N = _weight_shape(w)[1] if n_cols is None else n_cols
    tm = _tile(T, 1024)
    osz = jnp.dtype(out_dtype).itemsize
    S = 0 if w_side is None else w_side.shape[1]
    n_i = T // tm
    col_tiles = [t for t in (2048, 1536, 1024, 512, 256, 128) if N % t == 0]

    def plan(tn):
        n_j = N // tn
        if not _cast_divides(cast_weights, B * n_i * n_j):
            return None, VMEM_LIMIT_CAP
        c = _cast_specs(cast_weights, B * n_i * n_j, lambda b, i, j: (b * n_i + i) * n_j + j)
        nbytes = (2 * tm * D * 4 + 2 * D * tn * 2 + 2 * tm * tn * osz + tm * D * 2
                  + 2 * MODMM_ROW_GROUP * tn * 4 + 2 * D * S * 2 + 2 * tm * S * 4 + c[3])
        return c, nbytes

    tn = next(t for t in col_tiles if plan(t)[1] + MODMM_MIN_HEADROOM <= VMEM_LIMIT_CAP)
    (c_in, c_out, c_shapes, _), blocks = plan(tn)
    w_arr, w_spec = _weight_operand(w, (D, tn), lambda b, i, j: (0, j))
    in_specs = [
        pl.BlockSpec((None, tm, D), lambda b, i, j: (b, i, 0)),
        _mod_spec(layer, sh_slot, D, 3),
        _mod_spec(layer, sc_slot, D, 3),
        w_spec,
    ]
    out_specs = [pl.BlockSpec((None, tm, tn), lambda b, i, j: (b, i, j))]
    out_shape = [jax.ShapeDtypeStruct((B, T, N), out_dtype)]
    args = [x, mod4, mod4, w_arr]
    if w_side is not None:
        in_specs.append(pl.BlockSpec((D, S), lambda b, i, j: (0, 0)))
        out_specs.append(pl.BlockSpec((None, tm, S), lambda b, i, j: (b, i, 0)))
        out_shape.append(jax.ShapeDtypeStruct((B, T, S), F32))
        args.append(w_side)
    outs = pl.pallas_call(
        functools.partial(_modmm_kernel, out_scale=out_scale, with_side=w_side is not None,
                          n_cast=len(cast_weights)),
        grid=(B, n_i, N // tn),
        in_specs=in_specs + c_in,
        out_specs=out_specs + c_out,
        out_shape=out_shape + c_shapes,
        scratch_shapes=[pltpu.VMEM((tm, D), BF16)],
        compiler_params=_params(("arbitrary", "arbitrary", "arbitrary"), blocks),
        name="modulated_matmul",
    )(*args, *[cw for cw, _ in cast_weights])
    n_main = 1 if w_side is None else 2
    return outs[0], (None if w_side is None else outs[1]), tuple(outs[n_main:])


GLA_CUMSUM_ROWS = 2 * GLA_CHUNK


def _cum_log_decay(gl, w_up, b_gate, tri):
    n = GLA_CUMSUM_ROWS
    gl = gl.astype(BF16)
    parts = []
    for sl in _row_slices(gl.shape[0], gl.shape[0] // n):
        g = _dot(gl[sl, :], w_up) + b_gate
        log_a = (jnp.minimum(g, 0.0) - jnp.log(1.0 + jnp.exp(-jnp.abs(g)))) * (1.0 / GLA_GATE_TAU)
        hi, lo = _split_bf16(log_a)
        parts.append(_dot(tri, jnp.concatenate([hi, lo], axis=0)))
    return jnp.concatenate(parts, axis=0)


def _chunk_cumsum_matrix():
    n = GLA_CUMSUM_ROWS
    row = lax.broadcasted_iota(jnp.int32, (n, n), 0)
    col = lax.broadcasted_iota(jnp.int32, (n, n), 1)
    tri = ((col <= row) & (row // GLA_CHUNK == col // GLA_CHUNK)).astype(BF16)
    return jnp.concatenate([tri, tri], axis=1)


GLA_HEADS_PER_STEP = 2


def _gla_rec_kernel(q_ref, k_ref, v_ref, r_ref, gl_ref, wup_ref, bg_ref, tri_ref, ng_ref, o_ref, st_ref,
                    *, n_chunks, dk, dv):
    C = GLA_CHUNK
    bc_all = _cum_log_decay(gl_ref[...], wup_ref[...], bg_ref[...], tri_ref[...])

    @pl.when(pl.program_id(2) == 0)
    def _():
        st_ref[...] = jnp.zeros_like(st_ref)

    crow = lax.broadcasted_iota(jnp.int32, (C, C), 0)
    ccol = lax.broadcasted_iota(jnp.int32, (C, C), 1)
    causal = ccol <= crow
    prow = lax.broadcasted_iota(jnp.int32, (C, 2 * C), 0)
    pcol = lax.broadcasted_iota(jnp.int32, (C, 2 * C), 1)
    causal_second = pcol <= prow + C
    ng = ng_ref[...]

    bc, k, q_dec, k_intra = [], [], [], []
    for h in range(GLA_HEADS_PER_STEP):
        ks = slice(h * dk, (h + 1) * dk)
        bc.append(bc_all[:, ks])
        k.append(k_ref[:, ks])
        q_dec.append(q_ref[:, ks] * (dk ** -0.5) * jnp.exp(bc[h]))
        k_intra.append((k[h] * jnp.exp(-bc[h])).astype(BF16))

    for g in range(n_chunks // 2):
        c1 = slice(2 * g * C, (2 * g + 1) * C)
        c2 = slice((2 * g + 1) * C, (2 * g + 2) * C)
        both = slice(2 * g * C, (2 * g + 2) * C)
        for h in range(GLA_HEADS_PER_STEP):
            vs = slice(h * dv, (h + 1) * dv)
            b1 = bc[h][(2 * g + 1) * C - 1:(2 * g + 1) * C, :]
            b2 = bc[h][(2 * g + 2) * C - 1:(2 * g + 2) * C, :]
            k1_mid = (k[h][c1] * jnp.exp(b1 - bc[h][c1])).astype(BF16)
            k1_end = k[h][c1] * jnp.exp((b1 + b2) - bc[h][c1])
            k2_end = k[h][c2] * jnp.exp(b2 - bc[h][c2])
            k_end = jnp.concatenate([k1_end, k2_end], axis=0).astype(BF16)
            q1 = q_dec[h][c1].astype(BF16)
            q2 = q_dec[h][c2].astype(BF16)
            q2_from_start = (q_dec[h][c2] * jnp.exp(b1)).astype(BF16)
            v_both = v_ref[both, vs].astype(BF16)
            s1 = jnp.where(causal, _dot_nt(q1, k_intra[h][c1]), 0.0)
            s2 = jnp.where(causal_second,
                           _dot_nt(q2, jnp.concatenate([k1_mid, k_intra[h][c2]], axis=0)), 0.0)
            st = st_ref[h]
            st_bf = st.astype(BF16)
            o1 = _dot(s1.astype(BF16), v_both[:C]) + _dot_nt(q1, st_bf)
            o2 = _dot(s2.astype(BF16), v_both) + _dot_nt(q2_from_start, st_bf)
            st_ref[h] = st * jnp.exp(b1 + b2) + _dot_tn(v_both, k_end)
            for sl, o in ((c1, o1), (c2, o2)):
                o = o * lax.rsqrt(jnp.mean(o * o, axis=-1, keepdims=True) + RMS_EPS) * ng
                r = r_ref[sl, vs]
                o_ref[sl, vs] = (o * (r * (1.0 / (1.0 + jnp.exp(-r))))).astype(o_ref.dtype)


def _gla_recurrence(proj, gl, w_up, b_gate, norm_g):
    B, T, _ = proj.shape
    R, QK = w_up.shape
    n = GLA_CUMSUM_ROWS
    dk = QK // GLA_HEADS
    dv = norm_g.shape[-1]
    V = dv * GLA_HEADS
    hp = GLA_HEADS_PER_STEP
    assert dk % LANES == 0 and dv % LANES == 0 and GLA_HEADS % hp == 0
    wk, wv = hp * dk, hp * dv
    assert QK % wk == 0 and (2 * QK) % wv == 0 and (2 * QK + V) % wv == 0
    n_chunks = min(16, T // GLA_CHUNK)
    tb = n_chunks * GLA_CHUNK
    assert T % tb == 0 and n_chunks % 2 == 0 and tb % n == 0
    kk = QK // wk
    kv, kr = 2 * QK // wv, (2 * QK + V) // wv
    blocks = (2 * (2 * tb * wk * 4 + 2 * tb * wv * 4 + tb * wv * 2 + tb * R * 4) + hp * dv * dk * 4
              + 10 * tb * wk * 4)
    return pl.pallas_call(
        functools.partial(_gla_rec_kernel, n_chunks=n_chunks, dk=dk, dv=dv),
        grid=(B, GLA_HEADS // hp, T // tb),
        in_specs=[
            pl.BlockSpec((None, tb, wk), lambda b, h, t: (b, t, h)),
            pl.BlockSpec((None, tb, wk), lambda b, h, t: (b, t, kk + h)),
            pl.BlockSpec((None, tb, wv), lambda b, h, t: (b, t, kv + h)),
            pl.BlockSpec((None, tb, wv), lambda b, h, t: (b, t, kr + h)),
            pl.BlockSpec((None, tb, R), lambda b, h, t: (b, t, 0)),
            pl.BlockSpec((R, wk), lambda b, h, t: (0, h)),
            pl.BlockSpec((1, wk), lambda b, h, t: (0, h)),
            pl.BlockSpec((n, 2 * n), lambda b, h, t: (0, 0)),
            pl.BlockSpec((1, dv), lambda b, h, t: (0, 0)),
        ],
        out_specs=pl.BlockSpec((None, tb, wv), lambda b, h, t: (b, t, h)),
        out_shape=jax.ShapeDtypeStruct((B, T, V), BF16),
        scratch_shapes=[pltpu.VMEM((hp, dv, dk), F32)],
        compiler_params=_params(("arbitrary", "arbitrary", "arbitrary"), blocks),
        name="gla_recurrence",
    )(proj, proj, proj, proj, gl, w_up, b_gate.reshape(1, QK), _chunk_cumsum_matrix(), norm_g.reshape(1, dv))


def _proj_ln_kernel(a_ref, w_ref, x_ref, g_ref, lg_ref, lb_ref, o_ref, *, alpha):
    groups = _row_slices(a_ref.shape[0], 4)

    def finish(sl, y):
        z = alpha * x_ref[sl, :] + (1.0 + g_ref[...]) * y
        o_ref[sl, :] = _layer_norm(z, lg_ref[...], lb_ref[...])

    pending = None
    for sl in groups:
        y = _dot(a_ref[sl, :], w_ref[...])
        if pending is not None:
            finish(*pending)
        pending = (sl, y)
    finish(*pending)


def _proj_residual_ln(a, w, x, mod4, layer, gate_slot, ln_g, ln_b, alpha):
    B, T, D = x.shape
    K = a.shape[-1]
    tm = _tile(T, 1024)
    blocks = 2 * tm * K * 2 + K * D * 2 + 4 * tm * D * 4 + 3 * (tm // 4) * D * 4
    w, w_spec = _weight_operand(w, (K, D), lambda b, i: (0, 0), pipeline_mode=pl.Buffered(1))
    return pl.pallas_call(
        functools.partial(_proj_ln_kernel, alpha=alpha),
        grid=(B, T // tm),
        in_specs=[
            pl.BlockSpec((None, tm, K), lambda b, i: (b, i, 0)),
            w_spec,
            pl.BlockSpec((None, tm, D), lambda b, i: (b, i, 0)),
            _mod_spec(layer, gate_slot, D, 2),
            pl.BlockSpec((1, D), lambda b, i: (0, 0)),
            pl.BlockSpec((1, D), lambda b, i: (0, 0)),
        ],
        out_specs=pl.BlockSpec((None, tm, D), lambda b, i: (b, i, 0)),
        out_shape=jax.ShapeDtypeStruct((B, T, D), F32),
        compiler_params=_params(("arbitrary", "arbitrary"), blocks),
        name="proj_residual_ln",
    )(a, w, x, mod4, ln_g.reshape(1, D), ln_b.reshape(1, D))


MLP_TM = 1024
MLP_TF = 1024
MLP_ROW_GROUP = 256

def _mlp_kernel(x_ref, sh_ref, sc_ref, g_ref, w1_ref, b1_ref, w2_ref, b2_ref, lg_ref, lb_ref,
                o_ref, h_ref, *, alpha):
    f = pl.program_id(2)
    last = pl.num_programs(2) - 1
    groups = _row_slices(x_ref.shape[0], x_ref.shape[0] // MLP_ROW_GROUP)

    def contribution(sl):
        u = jnp.maximum(_dot(h_ref[sl, :], w1_ref[...]) + b1_ref[...], 0.0)
        return _dot((u * u).astype(BF16), w2_ref[...])

    @pl.when(f == 0)
    def _():
        for sl in groups:
            h_ref[sl, :] = (x_ref[sl, :] * (1.0 + sc_ref[...]) + sh_ref[...]).astype(BF16)
            o_ref[sl, :] = contribution(sl)

    @pl.when(jnp.logical_and(f > 0, f < last))
    def _():
        for sl in groups:
            o_ref[sl, :] += contribution(sl)

    @pl.when(f == last)
    def _():
        for sl in groups:
            y = o_ref[sl, :] + contribution(sl) + b2_ref[...]
            z = alpha * x_ref[sl, :] + (1.0 + g_ref[...]) * y
            o_ref[sl, :] = _layer_norm(z, lg_ref[...], lb_ref[...])


def _mlp_block(x, mod4, layer, w1, b1, w2, b2, ln_g, ln_b, alpha):
    B, T, D = x.shape
    F = _weight_shape(w1)[1]
    tm, tf = _tile(T, MLP_TM), _tile(F, MLP_TF)
    assert F // tf >= 2, "the first and the last d_ff step must be distinct grid steps"
    assert tm % MLP_ROW_GROUP == 0
    blocks = (4 * tm * D * 4 + tm * D * 2 + 4 * D * tf * 2 + tm * tf * 6 + 3 * MLP_ROW_GROUP * D * 4)
    w1, w1_spec = _weight_operand(w1, (D, tf), lambda b, i, f: (0, f))
    w2, w2_spec = _weight_operand(w2, (tf, D), lambda b, i, f: (f, 0))
    return pl.pallas_call(
        functools.partial(_mlp_kernel, alpha=alpha),
        grid=(B, T // tm, F // tf),
        in_specs=[
            pl.BlockSpec((None, tm, D), lambda b, i, f: (b, i, 0)),
            _mod_spec(layer, 3, D, 3),
            _mod_spec(layer, 4, D, 3),
            _mod_spec(layer, 5, D, 3),
            w1_spec,
            pl.BlockSpec((1, tf), lambda b, i, f: (0, f)),
            w2_spec,
            pl.BlockSpec((1, D), lambda b, i, f: (0, 0)),
            pl.BlockSpec((1, D), lambda b, i, f: (0, 0)),
            pl.BlockSpec((1, D), lambda b, i, f: (0, 0)),
        ],
        out_specs=pl.BlockSpec((None, tm, D), lambda b, i, f: (b, i, 0)),
        out_shape=jax.ShapeDtypeStruct((B, T, D), F32),
        scratch_shapes=[pltpu.VMEM((tm, D), BF16)],
        compiler_params=_params(("arbitrary", "arbitrary", "arbitrary"), blocks),
        name="mlp_block",
    )(x, mod4, mod4, mod4, w1, b1.reshape(1, F), w2, b2.reshape(1, D), ln_g.reshape(1, D), ln_b.reshape(1, D))


SB_BLK = 256
LOG2_E = 1.4426950408889634
SB_DEAD_LOG2 = 160.0
SB_NO_KEYS = 1e30


def _sb_attn_kernel(q_ref, k_ref, v_ref, *rest, n_pairs, n_cast):
    o_ref = rest[n_cast]
    _cast_slabs(rest[:n_cast], rest[n_cast + 1:])
    _sb_attn_body(q_ref, k_ref, v_ref, o_ref, n_pairs)


def _sb_attn_body(q_ref, k_ref, v_ref, o_ref, n_pairs):
    n = SB_BLK
    dh = q_ref.shape[-1]
    jrow = lax.broadcasted_iota(jnp.int32, (n, n), 0)
    jcol = lax.broadcasted_iota(jnp.int32, (n, n), 1)
    later = (jrow > jcol).astype(BF16)
    diag_mask = jcol < jrow

    def scores(q, blk):
        return _dot_nt(q, k_ref[pl.ds(pl.multiple_of(blk * n, n), n), :])

    def log_terms(z, mask):
        neg_abs = lax.bitcast_convert_type(
            lax.bitcast_convert_type(z, jnp.uint32) | jnp.uint32(0x80000000), F32)
        sp = jnp.maximum(z, 0.0) + jnp.log(1.0 + jnp.exp2(neg_abs)) * LOG2_E
        lb = z - sp
        if mask is not None:
            sp = jnp.where(mask, sp, 0.0)
        return sp.astype(BF16), lb, sp[:, 0:1]

    def suffix(hilo):
        return _dot(hilo, later)

    def weights(lb, sfx, r, sp0, mask):
        a = jnp.exp2(lb - sfx if r is None else lb - (sfx + r))
        if mask is not None:
            a = jnp.where(mask, a, 0.0)
        tot = sfx[:, 0:1] + sp0
        return a.astype(BF16), tot if r is None else r + tot

    def values(a, blk):
        return _dot(a, v_ref[pl.ds(pl.multiple_of(blk * n, n), n), :])

    def step(qa, qb, blk_a, blk_b, ra, rb, acc_a, acc_b, mask):
        za = scores(qa, blk_a)
        zb = scores(qb, blk_b)
        hla, lba, spa0 = log_terms(za, mask)
        sfa = suffix(hla)
        hlb, lbb, spb0 = log_terms(zb, mask)
        aa, ra = weights(lba, sfa, ra, spa0, mask)
        sfb = suffix(hlb)
        acc_a = acc_a + values(aa, blk_a)
        ab, rb = weights(lbb, sfb, rb, spb0, mask)
        acc_b = acc_b + values(ab, blk_b)
        return ra, rb, acc_a, acc_b

    def pair_body(p, _):
        blk_a0 = 2 * p
        blk_b0 = blk_a0 + 1
        row_a = pl.multiple_of(blk_a0 * n, n)
        row_b = pl.multiple_of(blk_b0 * n, n)
        qa = q_ref[pl.ds(row_a, n), :]
        qb = q_ref[pl.ds(row_b, n), :]

        blk_a1 = jnp.maximum(blk_a0 - 1, 0)
        blk_b1 = blk_a0
        za0 = scores(qa, blk_a0)
        zb0 = scores(qb, blk_b0)
        za1 = scores(qa, blk_a1)
        zb1 = scores(qb, blk_b1)
        hla0, lba0, spa0 = log_terms(za0, diag_mask)
        sfa0 = suffix(hla0)
        hlb0, lbb0, spb0 = log_terms(zb0, diag_mask)
        sfb0 = suffix(hlb0)
        hla1, lba1, spa1 = log_terms(za1, None)
        aa0, ra = weights(lba0, sfa0, None, spa0, diag_mask)
        sfa1 = suffix(hla1)
        acc_a = values(aa0, blk_a0)
        hlb1, lbb1, spb1 = log_terms(zb1, None)
        ab0, rb = weights(lbb0, sfb0, None, spb0, diag_mask)
        sfb1 = suffix(hlb1)
        acc_b = values(ab0, blk_b0)
        ra = jnp.where(p > 0, ra, SB_NO_KEYS)
        aa1, ra = weights(lba1, sfa1, ra, spa1, None)
        acc_a = acc_a + values(aa1, blk_a1)
        ab1, rb = weights(lbb1, sfb1, rb, spb1, None)
        acc_b = acc_b + values(ab1, blk_b1)

        def alive(ra, rb):
            return jnp.minimum(jnp.min(ra), jnp.min(rb)) < SB_DEAD_LOG2

        def cond(carry):
            i, go = carry[0], carry[1]
            return jnp.logical_and(i <= blk_b0, go)

        def body(carry):
            i, _, ra, rb, acc_a, acc_b = carry
            has_a = i <= blk_a0
            ra = jnp.where(has_a, ra, SB_NO_KEYS)
            ra, rb, acc_a, acc_b = step(qa, qb, jnp.maximum(blk_a0 - i, 0), blk_b0 - i,
                                        ra, rb, acc_a, acc_b, None)
            return i + 1, alive(ra, rb), ra, rb, acc_a, acc_b

        carry = lax.while_loop(cond, body, (jnp.int32(2), alive(ra, rb), ra, rb, acc_a, acc_b))
        o_ref[pl.ds(row_a, n), :] = carry[4].astype(o_ref.dtype)
        o_ref[pl.ds(row_b, n), :] = carry[5].astype(o_ref.dtype)
        return 0

    lax.fori_loop(0, n_pairs, pair_body, 0)


def _sb_attention(q, kv, cast_weights=()):
    B, T, HD = q.shape
    dh = HD // SB_HEADS
    assert dh % LANES == 0 and T % (2 * SB_BLK) == 0
    c_in, c_out, c_shapes, c_bytes = _cast_specs(cast_weights, B * SB_HEADS, lambda b, h: b * SB_HEADS + h)
    blocks = 8 * T * dh * 2 + 24 * SB_BLK * SB_BLK * 4 + c_bytes
    outs = pl.pallas_call(
        functools.partial(_sb_attn_kernel, n_pairs=T // (2 * SB_BLK), n_cast=len(cast_weights)),
        grid=(B, SB_HEADS),
        in_specs=[
            pl.BlockSpec((None, T, dh), lambda b, h: (b, 0, h)),
            pl.BlockSpec((None, T, dh), lambda b, h: (b, 0, h)),
            pl.BlockSpec((None, T, dh), lambda b, h: (b, 0, SB_HEADS + h)),
        ] + c_in,
        out_specs=[pl.BlockSpec((None, T, dh), lambda b, h: (b, 0, h))] + c_out,
        out_shape=[jax.ShapeDtypeStruct((B, T, HD), BF16)] + c_shapes,
        compiler_params=_params(("arbitrary", "arbitrary"), blocks),
        name="stick_breaking_attention",
    )(q, kv, kv, *[w for w, _ in cast_weights])
    return outs[0], tuple(outs[1:])


def kernel(x, c, ada_w, ada_b, ln_g, ln_b, gla_w_in, gla_w_gate_up, gla_b_gate, gla_norm_g, gla_w_out, kv_ada_w, kv_ada_b, w_kv, sb_w_q, sb_w_out, mlp_w1, mlp_b1, mlp_w2, mlp_b2):
    B, T, D = x.shape
    depth = ada_w.shape[0]
    n_a = gla_w_in.shape[0]
    alpha = (2.0 * depth) ** 0.25
    QK = gla_w_gate_up.shape[-1]
    V = gla_w_out.shape[1]
    n_main = 2 * QK + 2 * V
    dh = sb_w_q.shape[-1] // SB_HEADS

    c_pad = jnp.pad(c, ((0, MOD_ROWS - B), (0, 0)))
    mod4 = _mod_vectors(c_pad, ada_w, ada_b).reshape(depth, MOD_ROWS, 1, N_MOD * D)
    kv_mod4 = _mod_vectors(c_pad, kv_ada_w[None], kv_ada_b[None]).reshape(1, MOD_ROWS, 1, 2 * D)

    gla_w_in_bf = gla_w_in.astype(BF16)
    gla_w_out_bf = gla_w_out.astype(BF16)
    w_kv_bf = w_kv.astype(BF16)
    sb_w_q_bf = sb_w_q.astype(BF16)

    kv = None
    for l in range(depth):
        if l < n_a:
            pad = LANES - GLA_GATE_RANK
            w_gate = jnp.pad(gla_w_in_bf[l, :, n_main:], ((0, 0), (0, pad)))
            w_up = jnp.pad(gla_w_gate_up[l], ((0, pad), (0, 0))).astype(BF16)
            proj, gl, (w1, w2) = _mod_matmul(
                x, mod4, l, 0, 1, (gla_w_in_bf, l), F32, n_cols=n_main, w_side=w_gate,
                cast_weights=((mlp_w1, l), (mlp_w2, l)))
            w_out = (gla_w_out_bf, l)
            mixed = _gla_recurrence(proj, gl, w_up, gla_b_gate[l], gla_norm_g[l])
        else:
            j = l - n_a
            if kv is None:
                kv, _, _ = _mod_matmul(x, kv_mod4, 0, 0, 1, w_kv_bf, BF16)
            q, _, _ = _mod_matmul(x, mod4, l, 0, 1, (sb_w_q_bf, j), BF16, out_scale=dh ** -0.5 * LOG2_E)
            mixed, (w_out, w1, w2) = _sb_attention(q, kv, ((sb_w_out, j), (mlp_w1, l), (mlp_w2, l)))
        x = _proj_residual_ln(mixed, w_out, x, mod4, l, 2, ln_g[l, 0], ln_b[l, 0], alpha)
        x = _mlp_block(x, mod4, l, w1, mlp_b1[l], w2, mlp_b2[l], ln_g[l, 1], ln_b[l, 1], alpha)
    return x
```

```python
import functools

import jax
import jax.numpy as jnp
from jax import lax
from jax.experimental import pallas as pl
from jax.experimental.pallas import tpu as pltpu

F32 = jnp.float32
BF16 = jnp.bfloat16

GLA_HEADS = 4
GLA_GATE_RANK = 16
GLA_GATE_TAU = 16.0
GLA_CHUNK = 64
SB_HEADS = 16
N_MOD = 6
LN_EPS = 1e-5
RMS_EPS = 1e-5

LANES = 128
MOD_ROWS = 8
VMEM_LIMIT_CAP = 58 << 20
VMEM_HEADROOM = 6 << 20


def _params(semantics, block_bytes):
    limit = min(int(block_bytes) + VMEM_HEADROOM, VMEM_LIMIT_CAP)
    return pltpu.CompilerParams(dimension_semantics=semantics, vmem_limit_bytes=limit)


def _tile(n, want):
    t = min(n, want)
    assert n % t == 0, (n, want)
    return t


def _row_slices(n_rows, n_split):
    assert n_split >= 1 and n_rows % n_split == 0
    step = n_rows // n_split
    return [slice(i * step, (i + 1) * step) for i in range(n_split)]


def _weight_shape(w):
    return (w[0] if isinstance(w, tuple) else w).shape[-2:]


def _weight_operand(w, block, index_map, **spec_kwargs):
    if isinstance(w, tuple):
        stacked, entry = w
        return stacked, pl.BlockSpec((None,) + block, lambda *g: (entry,) + index_map(*g), **spec_kwargs)
    return w, pl.BlockSpec(block, index_map, **spec_kwargs)


def _dot(a, b):
    return jnp.dot(a, b, preferred_element_type=F32)


def _dot_nt(a, b):
    return lax.dot_general(a, b, (((1,), (1,)), ((), ())), preferred_element_type=F32)


def _dot_tn(a, b):
    return lax.dot_general(a, b, (((0,), (0,)), ((), ())), preferred_element_type=F32)


def _split_bf16(x):
    hi = x.astype(BF16)
    lo = (x - hi.astype(F32)).astype(BF16)
    return hi, lo


BF16_SUBLANES = 16


def _cast_divides(weights, n_steps):
    return all(w.shape[-2] % n_steps == 0 and (w.shape[-2] // n_steps) % BF16_SUBLANES == 0
               for w, _ in weights)


def _cast_specs(weights, n_steps, step_index):
    in_specs, out_specs, out_shapes, nbytes = [], [], [], 0
    for w, layer in weights:
        R, C = w.shape[-2:]
        rows = R // n_steps
        assert R % n_steps == 0 and rows % BF16_SUBLANES == 0, (w.shape, n_steps)
        if layer is None:
            in_specs.append(pl.BlockSpec((rows, C), lambda *g: (step_index(*g), 0)))
        else:
            in_specs.append(pl.BlockSpec((None, rows, C), lambda *g, _l=layer: (_l, step_index(*g), 0)))
        out_specs.append(pl.BlockSpec((rows, C), lambda *g: (step_index(*g), 0)))
        out_shapes.append(jax.ShapeDtypeStruct((R, C), BF16))
        nbytes += 2 * rows * C * (4 + 2)
    return in_specs, out_specs, out_shapes, nbytes


def _cast_slabs(src_refs, dst_refs):
    for src, dst in zip(src_refs, dst_refs):
        dst[...] = src[...].astype(dst.dtype)


def _layer_norm(z, g, b):
    mu = jnp.mean(z, axis=-1, keepdims=True)
    zc = z - mu
    var = jnp.mean(zc * zc, axis=-1, keepdims=True)
    return zc * lax.rsqrt(var + LN_EPS) * g + b


def _mod_kernel(c_ref, w_ref, b_ref, o_ref):
    c = c_ref[...]
    cs = c * (1.0 / (1.0 + jnp.exp(-c)))
    hi, lo = _split_bf16(cs)
    y = _dot(jnp.concatenate([hi, lo], axis=0), w_ref[...].astype(BF16))
    o_ref[...] = y[:MOD_ROWS] + y[MOD_ROWS:] + b_ref[...]


def _mod_vectors(c_pad, w, b):
    L, D, N = w.shape
    tn = _tile(N, 2048)
    blocks = 2 * (D * tn * 4) + MOD_ROWS * D * 4 + 4 * MOD_ROWS * tn * 4
    return pl.pallas_call(
        _mod_kernel,
        grid=(L, N // tn),
        in_specs=[
            pl.BlockSpec((MOD_ROWS, D), lambda l, j: (0, 0)),
            pl.BlockSpec((None, D, tn), lambda l, j: (l, 0, j)),
            pl.BlockSpec((None, 1, tn), lambda l, j: (l, 0, j)),
        ],
        out_specs=pl.BlockSpec((None, MOD_ROWS, tn), lambda l, j: (l, 0, j)),
        out_shape=jax.ShapeDtypeStruct((L, MOD_ROWS, N), F32),
        compiler_params=_params(("arbitrary", "arbitrary"), blocks),
        name="adaln_vectors",
    )(c_pad, w, b.reshape(L, 1, N))


def _mod_spec(layer, slot, D, n_grid):
    if n_grid == 2:
        return pl.BlockSpec((None, None, 1, D), lambda b, i: (layer, b, 0, slot))
    return pl.BlockSpec((None, None, 1, D), lambda b, i, j: (layer, b, 0, slot))


MODMM_ROW_GROUP = 256
MODMM_MIN_HEADROOM = 2 << 20


def _modmm_kernel(x_ref, sh_ref, sc_ref, w_ref, *rest, out_scale, with_side, n_cast):
    rest = list(rest)
    h_ref = rest.pop()
    ws_ref = rest.pop(0) if with_side else None
    cast_src = [rest.pop(0) for _ in range(n_cast)]
    o_ref = rest.pop(0)
    side_ref = rest.pop(0) if with_side else None
    _cast_slabs(cast_src, rest)
    groups = _row_slices(x_ref.shape[0], x_ref.shape[0] // MODMM_ROW_GROUP)

    def emit(sl):
        y = _dot(h_ref[sl, :], w_ref[...])
        if out_scale != 1.0:
            y = y * out_scale
        o_ref[sl, :] = y.astype(o_ref.dtype)

    @pl.when(pl.program_id(2) == 0)
    def _():
        for sl in groups:
            h_ref[sl, :] = (x_ref[sl, :] * (1.0 + sc_ref[...]) + sh_ref[...]).astype(BF16)
            emit(sl)
            if with_side:
                side_ref[sl, :] = _dot(h_ref[sl, :], ws_ref[...])

    @pl.when(pl.program_id(2) > 0)
    def _():
        for sl in groups:
            emit(sl)


def _mod_matmul(x, mod4, layer, sh_slot, sc_slot, w, out_dtype, out_scale=1.0, n_cols=None, w_side=None,
                cast_weights=()):
    B, T, D = x.shape
    N = _weight_shape(w)[1] if n_cols is None else n_cols
    tm = _tile(T, 1024)
    osz = jnp.dtype(out_dtype).itemsize
    S = 0 if w_side is None else w_side.shape[1]
    n_i = T // tm
    col_tiles = [t for t in (2048, 1536, 1024, 512, 256, 128) if N % t == 0]

    def plan(tn):
        n_j = N // tn
        if not _cast_divides(cast_weights, B * n_i * n_j):
            return None, VMEM_LIMIT_CAP
        c = _cast_specs(cast_weights, B * n_i * n_j, lambda b, i, j: (b * n_i + i) * n_j + j)
        nbytes = (2 * tm * D * 4 + 2 * D * tn * 2 + 2 * tm * tn * osz + tm * D * 2
                  + 2 * MODMM_ROW_GROUP * tn * 4 + 2 * D * S * 2 + 2 * tm * S * 4 + c[3])
        return c, nbytes

    tn = next(t for t in col_tiles if plan(t)[1] + MODMM_MIN_HEADROOM <= VMEM_LIMIT_CAP)
    (c_in, c_out, c_shapes, _), blocks = plan(tn)
    w_arr, w_spec = _weight_operand(w, (D, tn), lambda b, i, j: (0, j))
    in_specs = [
        pl.BlockSpec((None, tm, D), lambda b, i, j: (b, i, 0)),
        _mod_spec(layer, sh_slot, D, 3),
        _mod_spec(layer, sc_slot, D, 3),
        w_spec,
    ]
    out_specs = [pl.BlockSpec((None, tm, tn), lambda b, i, j: (b, i, j))]
    out_shape = [jax.ShapeDtypeStruct((B, T, N), out_dtype)]
    args = [x, mod4, mod4, w_arr]
    if w_side is not None:
        in_specs.append(pl.BlockSpec((D, S), lambda b, i, j: (0, 0)))
        out_specs.append(pl.BlockSpec((None, tm, S), lambda b, i, j: (b, i, 0)))
        out_shape.append(jax.ShapeDtypeStruct((B, T, S), F32))
        args.append(w_side)
    outs = pl.pallas_call(
        functools.partial(_modmm_kernel, out_scale=out_scale, with_side=w_side is not None,
                          n_cast=len(cast_weights)),
        grid=(B, n_i, N // tn),
        in_specs=in_specs + c_in,
        out_specs=out_specs + c_out,
        out_shape=out_shape + c_shapes,
        scratch_shapes=[pltpu.VMEM((tm, D), BF16)],
        compiler_params=_params(("arbitrary", "arbitrary", "arbitrary"), blocks),
        name="modulated_matmul",
    )(*args, *[cw for cw, _ in cast_weights])
    n_main = 1 if w_side is None else 2
    return outs[0], (None if w_side is None else outs[1]), tuple(outs[n_main:])


GLA_CUMSUM_ROWS = 2 * GLA_CHUNK


def _cum_log_decay(gl, w_up, b_gate, tri):
    n = GLA_CUMSUM_ROWS
    gl = gl.astype(BF16)
    parts = []
    for sl in _row_slices(gl.shape[0], gl.shape[0] // n):
        g = _dot(gl[sl, :], w_up) + b_gate
        log_a = (jnp.minimum(g, 0.0) - jnp.log(1.0 + jnp.exp(-jnp.abs(g)))) * (1.0 / GLA_GATE_TAU)
        hi, lo = _split_bf16(log_a)
        parts.append(_dot(tri, jnp.concatenate([hi, lo], axis=0)))
    return jnp.concatenate(parts, axis=0)


def _chunk_cumsum_matrix():
    n = GLA_CUMSUM_ROWS
    row = lax.broadcasted_iota(jnp.int32, (n, n), 0)
    col = lax.broadcasted_iota(jnp.int32, (n, n), 1)
    tri = ((col <= row) & (row // GLA_CHUNK == col // GLA_CHUNK)).astype(BF16)
    return jnp.concatenate([tri, tri], axis=1)


GLA_HEADS_PER_STEP = 2


def _gla_rec_kernel(q_ref, k_ref, v_ref, r_ref, gl_ref, wup_ref, bg_ref, tri_ref, ng_ref, o_ref, st_ref,
                    *, n_chunks, dk, dv):
    C = GLA_CHUNK
    assert GLA_CUMSUM_ROWS == 2 * C

    @pl.when(pl.program_id(2) == 0)
    def _():
        st_ref[...] = jnp.zeros_like(st_ref)

    crow = lax.broadcasted_iota(jnp.int32, (C, C), 0)
    ccol = lax.broadcasted_iota(jnp.int32, (C, C), 1)
    causal = ccol <= crow
    prow = lax.broadcasted_iota(jnp.int32, (C, 2 * C), 0)
    pcol = lax.broadcasted_iota(jnp.int32, (C, 2 * C), 1)
    causal_second = pcol <= prow + C
    ng = ng_ref[...]

    p1, p2 = slice(0, C), slice(C, 2 * C)
    for g in range(n_chunks // 2):
        c1 = slice(2 * g * C, (2 * g + 1) * C)
        c2 = slice((2 * g + 1) * C, (2 * g + 2) * C)
        both = slice(2 * g * C, (2 * g + 2) * C)
        bc_pair = _cum_log_decay(gl_ref[both, :], wup_ref[...], bg_ref[...], tri_ref[...])
        for h in range(GLA_HEADS_PER_STEP):
            ks = slice(h * dk, (h + 1) * dk)
            vs = slice(h * dv, (h + 1) * dv)
            bc = bc_pair[:, ks]
            k = k_ref[both, ks]
            q_dec = q_ref[both, ks] * (dk ** -0.5) * jnp.exp(bc)
            k_intra = (k * jnp.exp(-bc)).astype(BF16)
            b1 = bc[C - 1:C, :]
            b2 = bc[2 * C - 1:2 * C, :]
            k1_mid = (k[p1] * jnp.exp(b1 - bc[p1])).astype(BF16)
            k1_end = k[p1] * jnp.exp((b1 + b2) - bc[p1])
            k2_end = k[p2] * jnp.exp(b2 - bc[p2])
            k_end = jnp.concatenate([k1_end, k2_end], axis=0).astype(BF16)
            q1 = q_dec[p1].astype(BF16)
            q2 = q_dec[p2].astype(BF16)
            q2_from_start = (q_dec[p2] * jnp.exp(b1)).astype(BF16)
            v_both = v_ref[both, vs].astype(BF16)
            s1 = jnp.where(causal, _dot_nt(q1, k_intra[p1]), 0.0)
            s2 = jnp.where(causal_second,
                           _dot_nt(q2, jnp.concatenate([k1_mid, k_intra[p2]], axis=0)), 0.0)
            st = st_ref[h]
            st_bf = st.astype(BF16)
            o1 = _dot(s1.astype(BF16), v_both[:C]) + _dot_nt(q1, st_bf)
            o2 = _dot(s2.astype(BF16), v_both) + _dot_nt(q2_from_start, st_bf)
            st_ref[h] = st * jnp.exp(b1 + b2) + _dot_tn(v_both, k_end)
            for sl, o in ((c1, o1), (c2, o2)):
                o = o * lax.rsqrt(jnp.mean(o * o, axis=-1, keepdims=True) + RMS_EPS) * ng
                r = r_ref[sl, vs]
                o_ref[sl, vs] = (o * (r * (1.0 / (1.0 + jnp.exp(-r))))).astype(o_ref.dtype)


def _gla_recurrence(proj, gl, w_up, b_gate, norm_g):
    B, T, _ = proj.shape
    R, QK = w_up.shape
    n = GLA_CUMSUM_ROWS
    dk = QK // GLA_HEADS
    dv = norm_g.shape[-1]
    V = dv * GLA_HEADS
    hp = GLA_HEADS_PER_STEP
    assert dk % LANES == 0 and dv % LANES == 0 and GLA_HEADS % hp == 0
    wk, wv = hp * dk, hp * dv
    assert QK % wk == 0 and (2 * QK) % wv == 0 and (2 * QK + V) % wv == 0
    n_chunks = min(16, T // GLA_CHUNK)
    tb = n_chunks * GLA_CHUNK
    assert T % tb == 0 and n_chunks % 2 == 0 and tb % n == 0
    kk = QK // wk
    kv, kr = 2 * QK // wv, (2 * QK + V) // wv
    blocks = (2 * (2 * tb * wk * 4 + 2 * tb * wv * 4 + tb * wv * 2 + tb * R * 4) + hp * dv * dk * 4
              + 10 * tb * wk * 4)
    return pl.pallas_call(
        functools.partial(_gla_rec_kernel, n_chunks=n_chunks, dk=dk, dv=dv),
        grid=(B, GLA_HEADS // hp, T // tb),
        in_specs=[
            pl.BlockSpec((None, tb, wk), lambda b, h, t: (b, t, h)),
            pl.BlockSpec((None, tb, wk), lambda b, h, t: (b, t, kk + h)),
            pl.BlockSpec((None, tb, wv), lambda b, h, t: (b, t, kv + h)),
            pl.BlockSpec((None, tb, wv), lambda b, h, t: (b, t, kr + h)),
            pl.BlockSpec((None, tb, R), lambda b, h, t: (b, t, 0)),
            pl.BlockSpec((R, wk), lambda b, h, t: (0, h)),
            pl.BlockSpec((1, wk), lambda b, h, t: (0, h)),
            pl.BlockSpec((n, 2 * n), lambda b, h, t: (0, 0)),
            pl.BlockSpec((1, dv), lambda b, h, t: (0, 0)),
        ],
        out_specs=pl.BlockSpec((None, tb, wv), lambda b, h, t: (b, t, h)),
        out_shape=jax.ShapeDtypeStruct((B, T, V), BF16),
        scratch_shapes=[pltpu.VMEM((hp, dv, dk), F32)],
        compiler_params=_params(("arbitrary", "arbitrary", "arbitrary"), blocks),
        name="gla_recurrence",
    )(proj, proj, proj, proj, gl, w_up, b_gate.reshape(1, QK), _chunk_cumsum_matrix(), norm_g.reshape(1, dv))


def _proj_ln_kernel(a_ref, w_ref, x_ref, g_ref, lg_ref, lb_ref, o_ref, *, alpha):
    groups = _row_slices(a_ref.shape[0], 4)

    def finish(sl, y):
        z = alpha * x_ref[sl, :] + (1.0 + g_ref[...]) * y
        o_ref[sl, :] = _layer_norm(z, lg_ref[...], lb_ref[...])

    pending = None
    for sl in groups:
        y = _dot(a_ref[sl, :], w_ref[...])
        if pending is not None:
            finish(*pending)
        pending = (sl, y)
    finish(*pending)


def _proj_residual_ln(a, w, x, mod4, layer, gate_slot, ln_g, ln_b, alpha):
    B, T, D = x.shape
    K = a.shape[-1]
    tm = _tile(T, 1024)
    blocks = 2 * tm * K * 2 + K * D * 2 + 4 * tm * D * 4 + 3 * (tm // 4) * D * 4
    w, w_spec = _weight_operand(w, (K, D), lambda b, i: (0, 0), pipeline_mode=pl.Buffered(1))
    return pl.pallas_call(
        functools.partial(_proj_ln_kernel, alpha=alpha),
        grid=(B, T // tm),
        in_specs=[
            pl.BlockSpec((None, tm, K), lambda b, i: (b, i, 0)),
            w_spec,
            pl.BlockSpec((None, tm, D), lambda b, i: (b, i, 0)),
            _mod_spec(layer, gate_slot, D, 2),
            pl.BlockSpec((1, D), lambda b, i: (0, 0)),
            pl.BlockSpec((1, D), lambda b, i: (0, 0)),
        ],
        out_specs=pl.BlockSpec((None, tm, D), lambda b, i: (b, i, 0)),
        out_shape=jax.ShapeDtypeStruct((B, T, D), F32),
        compiler_params=_params(("arbitrary", "arbitrary"), blocks),
        name="proj_residual_ln",
    )(a, w, x, mod4, ln_g.reshape(1, D), ln_b.reshape(1, D))


MLP_TM = 1024
MLP_TF = 1024
MLP_ROW_GROUP = 256

def _mlp_kernel(x_ref, sh_ref, sc_ref, g_ref, w1_ref, b1_ref, w2_ref, b2_ref, lg_ref, lb_ref,
                o_ref, h_ref, *, alpha):
    f = pl.program_id(2)
    last = pl.num_programs(2) - 1
    groups = _row_slices(x_ref.shape[0], x_ref.shape[0] // MLP_ROW_GROUP)

    def contribution(sl):
        u = jnp.maximum(_dot(h_ref[sl, :], w1_ref[...]) + b1_ref[...], 0.0)
        return _dot((u * u).astype(BF16), w2_ref[...])

    @pl.when(f == 0)
    def _():
        for sl in groups:
            h_ref[sl, :] = (x_ref[sl, :] * (1.0 + sc_ref[...]) + sh_ref[...]).astype(BF16)
            o_ref[sl, :] = contribution(sl)

    @pl.when(jnp.logical_and(f > 0, f < last))
    def _():
        for sl in groups:
            o_ref[sl, :] += contribution(sl)

    @pl.when(f == last)
    def _():
        for sl in groups:
            y = o_ref[sl, :] + contribution(sl) + b2_ref[...]
            z = alpha * x_ref[sl, :] + (1.0 + g_ref[...]) * y
            o_ref[sl, :] = _layer_norm(z, lg_ref[...], lb_ref[...])


def _mlp_block(x, mod4, layer, w1, b1, w2, b2, ln_g, ln_b, alpha):
    B, T, D = x.shape
    F = _weight_shape(w1)[1]
    tm, tf = _tile(T, MLP_TM), _tile(F, MLP_TF)
    assert F // tf >= 2, "the first and the last d_ff step must be distinct grid steps"
    assert tm % MLP_ROW_GROUP == 0
    blocks = (4 * tm * D * 4 + tm * D * 2 + 4 * D * tf * 2 + tm * tf * 6 + 3 * MLP_ROW_GROUP * D * 4)
    w1, w1_spec = _weight_operand(w1, (D, tf), lambda b, i, f: (0, f))
    w2, w2_spec = _weight_operand(w2, (tf, D), lambda b, i, f: (f, 0))
    return pl.pallas_call(
        functools.partial(_mlp_kernel, alpha=alpha),
        grid=(B, T // tm, F // tf),
        in_specs=[
            pl.BlockSpec((None, tm, D), lambda b, i, f: (b, i, 0)),
            _mod_spec(layer, 3, D, 3),
            _mod_spec(layer, 4, D, 3),
            _mod_spec(layer, 5, D, 3),
            w1_spec,
            pl.BlockSpec((1, tf), lambda b, i, f: (0, f)),
            w2_spec,
            pl.BlockSpec((1, D), lambda b, i, f: (0, 0)),
            pl.BlockSpec((1, D), lambda b, i, f: (0, 0)),
            pl.BlockSpec((1, D), lambda b, i, f: (0, 0)),
        ],
        out_specs=pl.BlockSpec((None, tm, D), lambda b, i, f: (b, i, 0)),
        out_shape=jax.ShapeDtypeStruct((B, T, D), F32),
        scratch_shapes=[pltpu.VMEM((tm, D), BF16)],
        compiler_params=_params(("arbitrary", "arbitrary", "arbitrary"), blocks),
        name="mlp_block",
    )(x, mod4, mod4, mod4, w1, b1.reshape(1, F), w2, b2.reshape(1, D), ln_g.reshape(1, D), ln_b.reshape(1, D))


SB_BLK = 256
LOG2_E = 1.4426950408889634
SB_DEAD_LOG2 = 160.0
SB_NO_KEYS = 1e30


def _sb_attn_kernel(q_ref, k_ref, v_ref, *rest, n_pairs, n_cast):
    o_ref = rest[n_cast]
    _cast_slabs(rest[:n_cast], rest[n_cast + 1:])
    _sb_attn_body(q_ref, k_ref, v_ref, o_ref, n_pairs)


def _sb_attn_body(q_ref, k_ref, v_ref, o_ref, n_pairs):
    n = SB_BLK
    dh = q_ref.shape[-1]
    jrow = lax.broadcasted_iota(jnp.int32, (n, n), 0)
    jcol = lax.broadcasted_iota(jnp.int32, (n, n), 1)
    later = (jrow > jcol).astype(BF16)
    diag_mask = jcol < jrow

    def scores(q, blk):
        return _dot_nt(q, k_ref[pl.ds(pl.multiple_of(blk * n, n), n), :])

    def log_terms(z, mask):
        neg_abs = lax.bitcast_convert_type(
            lax.bitcast_convert_type(z, jnp.uint32) | jnp.uint32(0x80000000), F32)
        sp = jnp.maximum(z, 0.0) + jnp.log(1.0 + jnp.exp2(neg_abs)) * LOG2_E
        lb = z - sp
        if mask is not None:
            sp = jnp.where(mask, sp, 0.0)
        return sp.astype(BF16), lb, sp[:, 0:1]

    def suffix(hilo):
        return _dot(hilo, later)

    def weights(lb, sfx, r, sp0, mask):
        a = jnp.exp2(lb - sfx if r is None else lb - (sfx + r))
        if mask is not None:
            a = jnp.where(mask, a, 0.0)
        tot = sfx[:, 0:1] + sp0
        return a.astype(BF16), tot if r is None else r + tot

    def values(a, blk):
        return _dot(a, v_ref[pl.ds(pl.multiple_of(blk * n, n), n), :])

    def step(qa, qb, blk_a, blk_b, ra, rb, acc_a, acc_b, mask):
        za = scores(qa, blk_a)
        zb = scores(qb, blk_b)
        hla, lba, spa0 = log_terms(za, mask)
        sfa = suffix(hla)
        hlb, lbb, spb0 = log_terms(zb, mask)
        aa, ra = weights(lba, sfa, ra, spa0, mask)
        sfb = suffix(hlb)
        acc_a = acc_a + values(aa, blk_a)
        ab, rb = weights(lbb, sfb, rb, spb0, mask)
        acc_b = acc_b + values(ab, blk_b)
        return ra, rb, acc_a, acc_b

    def pair_body(p, _):
        blk_a0 = 2 * p
        blk_b0 = blk_a0 + 1
        row_a = pl.multiple_of(blk_a0 * n, n)
        row_b = pl.multiple_of(blk_b0 * n, n)
        qa = q_ref[pl.ds(row_a, n), :]
        qb = q_ref[pl.ds(row_b, n), :]

        blk_a1 = jnp.maximum(blk_a0 - 1, 0)
        blk_b1 = blk_a0
        za0 = scores(qa, blk_a0)
        zb0 = scores(qb, blk_b0)
        za1 = scores(qa, blk_a1)
        zb1 = scores(qb, blk_b1)
        hla0, lba0, spa0 = log_terms(za0, diag_mask)
        sfa0 = suffix(hla0)
        hlb0, lbb0, spb0 = log_terms(zb0, diag_mask)
        sfb0 = suffix(hlb0)
        hla1, lba1, spa1 = log_terms(za1, None)
        aa0, ra = weights(lba0, sfa0, None, spa0, diag_mask)
        sfa1 = suffix(hla1)
        acc_a = values(aa0, blk_a0)
        hlb1, lbb1, spb1 = log_terms(zb1, None)
        ab0, rb = weights(lbb0, sfb0, None, spb0, diag_mask)
        sfb1 = suffix(hlb1)
        acc_b = values(ab0, blk_b0)
        ra = jnp.where(p > 0, ra, SB_NO_KEYS)
        aa1, ra = weights(lba1, sfa1, ra, spa1, None)
        acc_a = acc_a + values(aa1, blk_a1)
        ab1, rb = weights(lbb1, sfb1, rb, spb1, None)
        acc_b = acc_b + values(ab1, blk_b1)

        def alive(ra, rb):
            return jnp.minimum(jnp.min(ra), jnp.min(rb)) < SB_DEAD_LOG2

        def cond(carry):
            i, go = carry[0], carry[1]
            return jnp.logical_and(i <= blk_b0, go)

        def body(carry):
            i, _, ra, rb, acc_a, acc_b = carry
            has_a = i <= blk_a0
            ra = jnp.where(has_a, ra, SB_NO_KEYS)
            ra, rb, acc_a, acc_b = step(qa, qb, jnp.maximum(blk_a0 - i, 0), blk_b0 - i,
                                        ra, rb, acc_a, acc_b, None)
            return i + 1, alive(ra, rb), ra, rb, acc_a, acc_b

        carry = lax.while_loop(cond, body, (jnp.int32(2), alive(ra, rb), ra, rb, acc_a, acc_b))
        o_ref[pl.ds(row_a, n), :] = carry[4].astype(o_ref.dtype)
        o_ref[pl.ds(row_b, n), :] = carry[5].astype(o_ref.dtype)
        return 0

    lax.fori_loop(0, n_pairs, pair_body, 0)


def _sb_attention(q, kv, cast_weights=()):
    B, T, HD = q.shape
    dh = HD // SB_HEADS
    assert dh % LANES == 0 and T % (2 * SB_BLK) == 0
    c_in, c_out, c_shapes, c_bytes = _cast_specs(cast_weights, B * SB_HEADS, lambda b, h: b * SB_HEADS + h)
    blocks = 8 * T * dh * 2 + 24 * SB_BLK * SB_BLK * 4 + c_bytes
    outs = pl.pallas_call(
        functools.partial(_sb_attn_kernel, n_pairs=T // (2 * SB_BLK), n_cast=len(cast_weights)),
        grid=(B, SB_HEADS),
        in_specs=[
            pl.BlockSpec((None, T, dh), lambda b, h: (b, 0, h)),
            pl.BlockSpec((None, T, dh), lambda b, h: (b, 0, h)),
            pl.BlockSpec((None, T, dh), lambda b, h: (b, 0, SB_HEADS + h)),
        ] + c_in,
        out_specs=[pl.BlockSpec((None, T, dh), lambda b, h: (b, 0, h))] + c_out,
        out_shape=[jax.ShapeDtypeStruct((B, T, HD), BF16)] + c_shapes,
        compiler_params=_params(("arbitrary", "arbitrary"), blocks),
        name="stick_breaking_attention",
    )(q, kv, kv, *[w for w, _ in cast_weights])
    return outs[0], tuple(outs[1:])


def kernel(x, c, ada_w, ada_b, ln_g, ln_b, gla_w_in, gla_w_gate_up, gla_b_gate, gla_norm_g, gla_w_out, kv_ada_w, kv_ada_b, w_kv, sb_w_q, sb_w_out, mlp_w1, mlp_b1, mlp_w2, mlp_b2):
    B, T, D = x.shape
    depth = ada_w.shape[0]
    n_a = gla_w_in.shape[0]
    alpha = (2.0 * depth) ** 0.25
    QK = gla_w_gate_up.shape[-1]
    V = gla_w_out.shape[1]
    n_main = 2 * QK + 2 * V
    dh = sb_w_q.shape[-1] // SB_HEADS

    c_pad = jnp.pad(c, ((0, MOD_ROWS - B), (0, 0)))
    mod4 = _mod_vectors(c_pad, ada_w, ada_b).reshape(depth, MOD_ROWS, 1, N_MOD * D)
    kv_mod4 = _mod_vectors(c_pad, kv_ada_w[None], kv_ada_b[None]).reshape(1, MOD_ROWS, 1, 2 * D)

    gla_w_in_bf = gla_w_in.astype(BF16)
    gla_w_out_bf = gla_w_out.astype(BF16)
    w_kv_bf = w_kv.astype(BF16)
    sb_w_q_bf = sb_w_q.astype(BF16)

    kv = None
    for l in range(depth):
        if l < n_a:
            pad = LANES - GLA_GATE_RANK
            w_gate = jnp.pad(gla_w_in_bf[l, :, n_main:], ((0, 0), (0, pad)))
            w_up = jnp.pad(gla_w_gate_up[l], ((0, pad), (0, 0))).astype(BF16)
            proj, gl, (w1, w2) = _mod_matmul(
                x, mod4, l, 0, 1, (gla_w_in_bf, l), F32, n_cols=n_main, w_side=w_gate,
                cast_weights=((mlp_w1, l), (mlp_w2, l)))
            w_out = (gla_w_out_bf, l)
            mixed = _gla_recurrence(proj, gl, w_up, gla_b_gate[l], gla_norm_g[l])
        else:
            j = l - n_a
            if kv is None:
                kv, _, _ = _mod_matmul(x, kv_mod4, 0, 0, 1, w_kv_bf, BF16)
            q, _, _ = _mod_matmul(x, mod4, l, 0, 1, (sb_w_q_bf, j), BF16, out_scale=dh ** -0.5 * LOG2_E)
            mixed, (w_out, w1, w2) = _sb_attention(q, kv, ((sb_w_out, j), (mlp_w1, l), (mlp_w2, l)))
        x = _proj_residual_ln(mixed, w_out, x, mod4, l, 2, ln_g[l, 0], ln_b[l, 0], alpha)
        x = _mlp_block(x, mod4, l, w1, mlp_b1[l], w2, mlp_b2[l], ln_g[l, 1], ln_b[l, 1], alpha)
    return x
```

```python
import functools

import jax
import jax.numpy as jnp
from jax import lax
from jax.experimental import pallas as pl
from jax.experimental.pallas import tpu as pltpu

F32 = jnp.float32
BF16 = jnp.bfloat16

GLA_HEADS = 4
GLA_GATE_RANK = 16
GLA_GATE_TAU = 16.0
GLA_CHUNK = 64
SB_HEADS = 16
N_MOD = 6
LN_EPS = 1e-5
RMS_EPS = 1e-5

LANES = 128
MOD_ROWS = 8
VMEM_LIMIT_CAP = 58 << 20
VMEM_HEADROOM = 6 << 20


def _params(semantics, block_bytes):
    limit = min(int(block_bytes) + VMEM_HEADROOM, VMEM_LIMIT_CAP)
    return pltpu.CompilerParams(dimension_semantics=semantics, vmem_limit_bytes=limit)


def _tile(n, want):
    t = min(n, want)
    assert n % t == 0, (n, want)
    return t


def _row_slices(n_rows, n_split):
    assert n_split >= 1 and n_rows % n_split == 0
    step = n_rows // n_split
    return [slice(i * step, (i + 1) * step) for i in range(n_split)]


def _weight_shape(w):
    return (w[0] if isinstance(w, tuple) else w).shape[-2:]


def _weight_operand(w, block, index_map, **spec_kwargs):
    if isinstance(w, tuple):
        stacked, entry = w
        return stacked, pl.BlockSpec((None,) + block, lambda *g: (entry,) + index_map(*g), **spec_kwargs)
    return w, pl.BlockSpec(block, index_map, **spec_kwargs)


def _dot(a, b):
    return jnp.dot(a, b, preferred_element_type=F32)


def _dot_nt(a, b):
    return lax.dot_general(a, b, (((1,), (1,)), ((), ())), preferred_element_type=F32)


def _dot_tn(a, b):
    return lax.dot_general(a, b, (((0,), (0,)), ((), ())), preferred_element_type=F32)


def _split_bf16(x):
    hi = x.astype(BF16)
    lo = (x - hi.astype(F32)).astype(BF16)
    return hi, lo


BF16_SUBLANES = 16


def _cast_divides(weights, n_steps):
    return all(w.shape[-2] % n_steps == 0 and (w.shape[-2] // n_steps) % BF16_SUBLANES == 0
               for w, _ in weights)


def _cast_specs(weights, n_steps, step_index):
    in_specs, out_specs, out_shapes, nbytes = [], [], [], 0
    for w, layer in weights:
        R, C = w.shape[-2:]
        rows = R // n_steps
        assert R % n_steps == 0 and rows % BF16_SUBLANES == 0, (w.shape, n_steps)
        if layer is None:
            in_specs.append(pl.BlockSpec((rows, C), lambda *g: (step_index(*g), 0)))
        else:
            in_specs.append(pl.BlockSpec((None, rows, C), lambda *g, _l=layer: (_l, step_index(*g), 0)))
        out_specs.append(pl.BlockSpec((rows, C), lambda *g: (step_index(*g), 0)))
        out_shapes.append(jax.ShapeDtypeStruct((R, C), BF16))
        nbytes += 2 * rows * C * (4 + 2)
    return in_specs, out_specs, out_shapes, nbytes


def _cast_slabs(src_refs, dst_refs):
    for src, dst in zip(src_refs, dst_refs):
        dst[...] = src[...].astype(dst.dtype)


def _layer_norm(z, g, b):
    mu = jnp.mean(z, axis=-1, keepdims=True)
    zc = z - mu
    var = jnp.mean(zc * zc, axis=-1, keepdims=True)
    return zc * lax.rsqrt(var + LN_EPS) * g + b


def _mod_kernel(c_ref, w_ref, b_ref, o_ref):
    c = c_ref[...]
    cs = c * (1.0 / (1.0 + jnp.exp(-c)))
    hi, lo = _split_bf16(cs)
    y = _dot(jnp.concatenate([hi, lo], axis=0), w_ref[...].astype(BF16))
    o_ref[...] = y[:MOD_ROWS] + y[MOD_ROWS:] + b_ref[...]


def _mod_vectors(c_pad, w, b):
    L, D, N = w.shape
    tn = _tile(N, 2048)
    blocks = 2 * (D * tn * 4) + MOD_ROWS * D * 4 + 4 * MOD_ROWS * tn * 4
    return pl.pallas_call(
        _mod_kernel,
        grid=(L, N // tn),
        in_specs=[
            pl.BlockSpec((MOD_ROWS, D), lambda l, j: (0, 0)),
            pl.BlockSpec((None, D, tn), lambda l, j: (l, 0, j)),
            pl.BlockSpec((None, 1, tn), lambda l, j: (l, 0, j)),
        ],
        out_specs=pl.BlockSpec((None, MOD_ROWS, tn), lambda l, j: (l, 0, j)),
        out_shape=jax.ShapeDtypeStruct((L, MOD_ROWS, N), F32),
        compiler_params=_params(("arbitrary", "arbitrary"), blocks),
        name="adaln_vectors",
    )(c_pad, w, b.reshape(L, 1, N))


def _mod_spec(layer, slot, D, n_grid):
    if n_grid == 2:
        return pl.BlockSpec((None, None, 1, D), lambda b, i: (layer, b, 0, slot))
    return pl.BlockSpec((None, None, 1, D), lambda b, i, j: (layer, b, 0, slot))


MODMM_ROW_GROUP = 256
MODMM_MIN_HEADROOM = 2 << 20


def _modmm_kernel(x_ref, sh_ref, sc_ref, w_ref, *rest, out_scale, with_side, n_cast):
    rest = list(rest)
    h_ref = rest.pop()
    ws_ref = rest.pop(0) if with_side else None
    cast_src = [rest.pop(0) for _ in range(n_cast)]
    o_ref = rest.pop(0)
    side_ref = rest.pop(0) if with_side else None
    _cast_slabs(cast_src, rest)
    groups = _row_slices(x_ref.shape[0], x_ref.shape[0] // MODMM_ROW_GROUP)

    def emit(sl):
        y = _dot(h_ref[sl, :], w_ref[...])
        if out_scale != 1.0:
            y = y * out_scale
        o_ref[sl, :] = y.astype(o_ref.dtype)

    @pl.when(pl.program_id(2) == 0)
    def _():
        for sl in groups:
            h_ref[sl, :] = (x_ref[sl, :] * (1.0 + sc_ref[...]) + sh_ref[...]).astype(BF16)
            emit(sl)
            if with_side:
                side_ref[sl, :] = _dot(h_ref[sl, :], ws_ref[...])

    @pl.when(pl.program_id(2) > 0)
    def _():
        for sl in groups:
            emit(sl)


def _mod_matmul(x, mod4, layer, sh_slot, sc_slot, w, out_dtype, out_scale=1.0, n_cols=None, w_side=None,
                cast_weights=()):
    B, T, D = x.shape
    N = _weight_shape(w)[1] if n_cols is None else n_cols
    tm = _tile(T, 1024)
    osz = jnp.dtype(out_dtype).itemsize
    S = 0 if w_side is None else w_side.shape[1]
    n_i = T // tm
    col_tiles = [t for t in (2048, 1536, 1024, 512, 256, 128) if N % t == 0]

    def plan(tn):
        n_j = N // tn
        if not _cast_divides(cast_weights, B * n_i * n_j):
            return None, VMEM_LIMIT_CAP
        c = _cast_specs(cast_weights, B * n_i * n_j, lambda b, i, j: (b * n_i + i) * n_j + j)
        nbytes = (2 * tm * D * 4 + 2 * D * tn * 2 + 2 * tm * tn * osz + tm * D * 2
                  + 2 * MODMM_ROW_GROUP * tn * 4 + 2 * D * S * 2 + 2 * tm * S * 4 + c[3])
        return c, nbytes

    tn = next(t for t in col_tiles if plan(t)[1] + MODMM_MIN_HEADROOM <= VMEM_LIMIT_CAP)
    (c_in, c_out, c_shapes, _), blocks = plan(tn)
    w_arr, w_spec = _weight_operand(w, (D, tn), lambda b, i, j: (0, j))
    in_specs = [
        pl.BlockSpec((None, tm, D), lambda b, i, j: (b, i, 0)),
        _mod_spec(layer, sh_slot, D, 3),
        _mod_spec(layer, sc_slot, D, 3),
        w_spec,
    ]
    out_specs = [pl.BlockSpec((None, tm, tn), lambda b, i, j: (b, i, j))]
    out_shape = [jax.ShapeDtypeStruct((B, T, N), out_dtype)]
    args = [x, mod4, mod4, w_arr]
    if w_side is not None:
        in_specs.append(pl.BlockSpec((D, S), lambda b, i, j: (0, 0)))
        out_specs.append(pl.BlockSpec((None, tm, S), lambda b, i, j: (b, i, 0)))
        out_shape.append(jax.ShapeDtypeStruct((B, T, S), F32))
        args.append(w_side)
    outs = pl.pallas_call(
        functools.partial(_modmm_kernel, out_scale=out_scale, with_side=w_side is not None,
                          n_cast=len(cast_weights)),
        grid=(B, n_i, N // tn),
        in_specs=in_specs + c_in,
        out_specs=out_specs + c_out,
        out_shape=out_shape + c_shapes,
        scratch_shapes=[pltpu.VMEM((tm, D), BF16)],
        compiler_params=_params(("arbitrary", "arbitrary", "arbitrary"), blocks),
        name="modulated_matmul",
    )(*args, *[cw for cw, _ in cast_weights])
    n_main = 1 if w_side is None else 2
    return outs[0], (None if w_side is None else outs[1]), tuple(outs[n_main:])


GLA_CUMSUM_ROWS = 2 * GLA_CHUNK


def _cum_log_decay(gl, w_up, b_gate, tri):
    n = GLA_CUMSUM_ROWS
    g = _dot(gl.astype(BF16), w_up) + b_gate
    log_a = (jnp.minimum(g, 0.0) - jnp.log(1.0 + jnp.exp(-jnp.abs(g)))) * (1.0 / GLA_GATE_TAU)
    hi, lo = _split_bf16(log_a)
    parts = [_dot(tri, jnp.concatenate([hi[sl, :], lo[sl, :]], axis=0))
             for sl in _row_slices(gl.shape[0], gl.shape[0] // n)]
    return jnp.concatenate(parts, axis=0)


def _chunk_cumsum_matrix():
    n = GLA_CUMSUM_ROWS
    row = lax.broadcasted_iota(jnp.int32, (n, n), 0)
    col = lax.broadcasted_iota(jnp.int32, (n, n), 1)
    tri = ((col <= row) & (row // GLA_CHUNK == col // GLA_CHUNK)).astype(BF16)
    return jnp.concatenate([tri, tri], axis=1)


GLA_HEADS_PER_STEP = 2


def _gla_rec_kernel(q_ref, k_ref, v_ref, r_ref, gl_ref, wup_ref, bg_ref, tri_ref, ng_ref, o_ref, st_ref,
                    *, n_chunks, dk, dv):
    C = GLA_CHUNK
    bc_all = _cum_log_decay(gl_ref[...], wup_ref[...], bg_ref[...], tri_ref[...])

    @pl.when(pl.program_id(2) == 0)
    def _():
        st_ref[...] = jnp.zeros_like(st_ref)

    crow = lax.broadcasted_iota(jnp.int32, (C, C), 0)
    ccol = lax.broadcasted_iota(jnp.int32, (C, C), 1)
    causal = ccol <= crow
    prow = lax.broadcasted_iota(jnp.int32, (C, 2 * C), 0)
    pcol = lax.broadcasted_iota(jnp.int32, (C, 2 * C), 1)
    causal_second = pcol <= prow + C
    ng = ng_ref[...]

    bc, k, q_dec, k_intra = [], [], [], []
    for h in range(GLA_HEADS_PER_STEP):
        ks = slice(h * dk, (h + 1) * dk)
        bc.append(bc_all[:, ks])
        k.append(k_ref[:, ks])
        q_dec.append(q_ref[:, ks] * (dk ** -0.5) * jnp.exp(bc[h]))
        k_intra.append((k[h] * jnp.exp(-bc[h])).astype(BF16))

    for g in range(n_chunks // 2):
        c1 = slice(2 * g * C, (2 * g + 1) * C)
        c2 = slice((2 * g + 1) * C, (2 * g + 2) * C)
        both = slice(2 * g * C, (2 * g + 2) * C)
        for h in range(GLA_HEADS_PER_STEP):
            vs = slice(h * dv, (h + 1) * dv)
            b1 = bc[h][(2 * g + 1) * C - 1:(2 * g + 1) * C, :]
            b2 = bc[h][(2 * g + 2) * C - 1:(2 * g + 2) * C, :]
            k1_mid = (k[h][c1] * jnp.exp(b1 - bc[h][c1])).astype(BF16)
            k1_end = k[h][c1] * jnp.exp((b1 + b2) - bc[h][c1])
            k2_end = k[h][c2] * jnp.exp(b2 - bc[h][c2])
            k_end = jnp.concatenate([k1_end, k2_end], axis=0).astype(BF16)
            q1 = q_dec[h][c1].astype(BF16)
            q2 = q_dec[h][c2].astype(BF16)
            q2_from_start = (q_dec[h][c2] * jnp.exp(b1)).astype(BF16)
            v_both = v_ref[both, vs].astype(BF16)
            s1 = jnp.where(causal, _dot_nt(q1, k_intra[h][c1]), 0.0)
            s2 = jnp.where(causal_second,
                           _dot_nt(q2, jnp.concatenate([k1_mid, k_intra[h][c2]], axis=0)), 0.0)
            st = st_ref[h]
            st_bf = st.astype(BF16)
            o1 = _dot(s1.astype(BF16), v_both[:C]) + _dot_nt(q1, st_bf)
            o2 = _dot(s2.astype(BF16), v_both) + _dot_nt(q2_from_start, st_bf)
            st_ref[h] = st * jnp.exp(b1 + b2) + _dot_tn(v_both, k_end)
            for sl, o in ((c1, o1), (c2, o2)):
                o = o * lax.rsqrt(jnp.mean(o * o, axis=-1, keepdims=True) + RMS_EPS) * ng
                r = r_ref[sl, vs]
                o_ref[sl, vs] = (o * (r * (1.0 / (1.0 + jnp.exp(-r))))).astype(o_ref.dtype)


def _gla_recurrence(proj, gl, w_up, b_gate, norm_g):
    B, T, _ = proj.shape
    R, QK = w_up.shape
    n = GLA_CUMSUM_ROWS
    dk = QK // GLA_HEADS
    dv = norm_g.shape[-1]
    V = dv * GLA_HEADS
    hp = GLA_HEADS_PER_STEP
    assert dk % LANES == 0 and dv % LANES == 0 and GLA_HEADS % hp == 0
    wk, wv = hp * dk, hp * dv
    assert QK % wk == 0 and (2 * QK) % wv == 0 and (2 * QK + V) % wv == 0
    n_chunks = min(16, T // GLA_CHUNK)
    tb = n_chunks * GLA_CHUNK
    assert T % tb == 0 and n_chunks % 2 == 0 and tb % n == 0
    kk = QK // wk
    kv, kr = 2 * QK // wv, (2 * QK + V) // wv
    blocks = (2 * (2 * tb * wk * 4 + 2 * tb * wv * 4 + tb * wv * 2 + tb * R * 4) + hp * dv * dk * 4
              + 10 * tb * wk * 4)
    return pl.pallas_call(
        functools.partial(_gla_rec_kernel, n_chunks=n_chunks, dk=dk, dv=dv),
        grid=(B, GLA_HEADS // hp, T // tb),
        in_specs=[
            pl.BlockSpec((None, tb, wk), lambda b, h, t: (b, t, h)),
            pl.BlockSpec((None, tb, wk), lambda b, h, t: (b, t, kk + h)),
            pl.BlockSpec((None, tb, wv), lambda b, h, t: (b, t, kv + h)),
            pl.BlockSpec((None, tb, wv), lambda b, h, t: (b, t, kr + h)),
            pl.BlockSpec((None, tb, R), lambda b, h, t: (b, t, 0)),
            pl.BlockSpec((R, wk), lambda b, h, t: (0, h)),
            pl.BlockSpec((1, wk), lambda b, h, t: (0, h)),
            pl.BlockSpec((n, 2 * n), lambda b, h, t: (0, 0)),
            pl.BlockSpec((1, dv), lambda b, h, t: (0, 0)),
        ],
        out_specs=pl.BlockSpec((None, tb, wv), lambda b, h, t: (b, t, h)),
        out_shape=jax.ShapeDtypeStruct((B, T, V), BF16),
        scratch_shapes=[pltpu.VMEM((hp, dv, dk), F32)],
        compiler_params=_params(("arbitrary", "arbitrary", "arbitrary"), blocks),
        name="gla_recurrence",
    )(proj, proj, proj, proj, gl, w_up, b_gate.reshape(1, QK), _chunk_cumsum_matrix(), norm_g.reshape(1, dv))


def _proj_ln_kernel(a_ref, w_ref, x_ref, g_ref, lg_ref, lb_ref, o_ref, *, alpha):
    groups = _row_slices(a_ref.shape[0], 4)

    def finish(sl, y):
        z = alpha * x_ref[sl, :] + (1.0 + g_ref[...]) * y
        o_ref[sl, :] = _layer_norm(z, lg_ref[...], lb_ref[...])

    pending = None
    for sl in groups:
        y = _dot(a_ref[sl, :], w_ref[...])
        if pending is not None:
            finish(*pending)
        pending = (sl, y)
    finish(*pending)


def _proj_residual_ln(a, w, x, mod4, layer, gate_slot, ln_g, ln_b, alpha):
    B, T, D = x.shape
    K = a.shape[-1]
    tm = _tile(T, 1024)
    blocks = 2 * tm * K * 2 + K * D * 2 + 4 * tm * D * 4 + 3 * (tm // 4) * D * 4
    w, w_spec = _weight_operand(w, (K, D), lambda b, i: (0, 0), pipeline_mode=pl.Buffered(1))
    return pl.pallas_call(
        functools.partial(_proj_ln_kernel, alpha=alpha),
        grid=(B, T // tm),
        in_specs=[
            pl.BlockSpec((None, tm, K), lambda b, i: (b, i, 0)),
            w_spec,
            pl.BlockSpec((None, tm, D), lambda b, i: (b, i, 0)),
            _mod_spec(layer, gate_slot, D, 2),
            pl.BlockSpec((1, D), lambda b, i: (0, 0)),
            pl.BlockSpec((1, D), lambda b, i: (0, 0)),
        ],
        out_specs=pl.BlockSpec((None, tm, D), lambda b, i: (b, i, 0)),
        out_shape=jax.ShapeDtypeStruct((B, T, D), F32),
        compiler_params=_params(("arbitrary", "arbitrary"), blocks),
        name="proj_residual_ln",
    )(a, w, x, mod4, ln_g.reshape(1, D), ln_b.reshape(1, D))


MLP_TM = 1024
MLP_TF = 1024
MLP_ROW_GROUP = 256

def _mlp_kernel(x_ref, sh_ref, sc_ref, g_ref, w1_ref, b1_ref, w2_ref, b2_ref, lg_ref, lb_ref,
                o_ref, h_ref, *, alpha):
    f = pl.program_id(2)
    last = pl.num_programs(2) - 1
    groups = _row_slices(x_ref.shape[0], x_ref.shape[0] // MLP_ROW_GROUP)

    def contribution(sl):
        u = jnp.maximum(_dot(h_ref[sl, :], w1_ref[...]) + b1_ref[...], 0.0)
        return _dot((u * u).astype(BF16), w2_ref[...])

    @pl.when(f == 0)
    def _():
        for sl in groups:
            h_ref[sl, :] = (x_ref[sl, :] * (1.0 + sc_ref[...]) + sh_ref[...]).astype(BF16)
            o_ref[sl, :] = contribution(sl)

    @pl.when(jnp.logical_and(f > 0, f < last))
    def _():
        for sl in groups:
            o_ref[sl, :] += contribution(sl)

    @pl.when(f == last)
    def _():
        for sl in groups:
            y = o_ref[sl, :] + contribution(sl) + b2_ref[...]
            z = alpha * x_ref[sl, :] + (1.0 + g_ref[...]) * y
            o_ref[sl, :] = _layer_norm(z, lg_ref[...], lb_ref[...])


def _mlp_block(x, mod4, layer, w1, b1, w2, b2, ln_g, ln_b, alpha):
    B, T, D = x.shape
    F = _weight_shape(w1)[1]
    tm, tf = _tile(T, MLP_TM), _tile(F, MLP_TF)
    assert F // tf >= 2, "the first and the last d_ff step must be distinct grid steps"
    assert tm % MLP_ROW_GROUP == 0
    blocks = (4 * tm * D * 4 + tm * D * 2 + 4 * D * tf * 2 + tm * tf * 6 + 3 * MLP_ROW_GROUP * D * 4)
    w1, w1_spec = _weight_operand(w1, (D, tf), lambda b, i, f: (0, f))
    w2, w2_spec = _weight_operand(w2, (tf, D), lambda b, i, f: (f, 0))
    return pl.pallas_call(
        functools.partial(_mlp_kernel, alpha=alpha),
        grid=(B, T // tm, F // tf),
        in_specs=[
            pl.BlockSpec((None, tm, D), lambda b, i, f: (b, i, 0)),
            _mod_spec(layer, 3, D, 3),
            _mod_spec(layer, 4, D, 3),
            _mod_spec(layer, 5, D, 3),
            w1_spec,
            pl.BlockSpec((1, tf), lambda b, i, f: (0, f)),
            w2_spec,
            pl.BlockSpec((1, D), lambda b, i, f: (0, 0)),
            pl.BlockSpec((1, D), lambda b, i, f: (0, 0)),
            pl.BlockSpec((1, D), lambda b, i, f: (0, 0)),
        ],
        out_specs=pl.BlockSpec((None, tm, D), lambda b, i, f: (b, i, 0)),
        out_shape=jax.ShapeDtypeStruct((B, T, D), F32),
        scratch_shapes=[pltpu.VMEM((tm, D), BF16)],
        compiler_params=_params(("arbitrary", "arbitrary", "arbitrary"), blocks),
        name="mlp_block",
    )(x, mod4, mod4, mod4, w1, b1.reshape(1, F), w2, b2.reshape(1, D), ln_g.reshape(1, D), ln_b.reshape(1, D))


SB_BLK = 256
LOG2_E = 1.4426950408889634
SB_DEAD_LOG2 = 160.0
SB_NO_KEYS = 1e30


def _sb_attn_kernel(q_ref, k_ref, v_ref, *rest, n_pairs, n_cast):
    o_ref = rest[n_cast]
    _cast_slabs(rest[:n_cast], rest[n_cast + 1:])
    _sb_attn_body(q_ref, k_ref, v_ref, o_ref, n_pairs)


def _sb_attn_body(q_ref, k_ref, v_ref, o_ref, n_pairs):
    n = SB_BLK
    dh = q_ref.shape[-1]
    jrow = lax.broadcasted_iota(jnp.int32, (n, n), 0)
    jcol = lax.broadcasted_iota(jnp.int32, (n, n), 1)
    later = (jrow > jcol).astype(BF16)
    diag_mask = jcol < jrow

    def scores(q, blk):
        return _dot_nt(q, k_ref[pl.ds(pl.multiple_of(blk * n, n), n), :])

    def log_terms(z, mask):
        neg_abs = lax.bitcast_convert_type(
            lax.bitcast_convert_type(z, jnp.uint32) | jnp.uint32(0x80000000), F32)
        sp = jnp.maximum(z, 0.0) + jnp.log(1.0 + jnp.exp2(neg_abs)) * LOG2_E
        lb = z - sp
        if mask is not None:
            sp = jnp.where(mask, sp, 0.0)
        return sp.astype(BF16), lb, sp[:, 0:1]

    def suffix(hilo):
        return _dot(hilo, later)

    def weights(lb, sfx, r, sp0, mask):
        a = jnp.exp2(lb - sfx if r is None else lb - (sfx + r))
        if mask is not None:
            a = jnp.where(mask, a, 0.0)
        tot = sfx[:, 0:1] + sp0
        return a.astype(BF16), tot if r is None else r + tot

    def values(a, blk):
        return _dot(a, v_ref[pl.ds(pl.multiple_of(blk * n, n), n), :])

    def step(qa, qb, blk_a, blk_b, ra, rb, acc_a, acc_b, mask):
        za = scores(qa, blk_a)
        zb = scores(qb, blk_b)
        hla, lba, spa0 = log_terms(za, mask)
        sfa = suffix(hla)
        hlb, lbb, spb0 = log_terms(zb, mask)
        aa, ra = weights(lba, sfa, ra, spa0, mask)
        sfb = suffix(hlb)
        acc_a = acc_a + values(aa, blk_a)
        ab, rb = weights(lbb, sfb, rb, spb0, mask)
        acc_b = acc_b + values(ab, blk_b)
        return ra, rb, acc_a, acc_b

    def pair_body(p, _):
        blk_a0 = 2 * p
        blk_b0 = blk_a0 + 1
        row_a = pl.multiple_of(blk_a0 * n, n)
        row_b = pl.multiple_of(blk_b0 * n, n)
        qa = q_ref[pl.ds(row_a, n), :]
        qb = q_ref[pl.ds(row_b, n), :]

        blk_a1 = jnp.maximum(blk_a0 - 1, 0)
        blk_b1 = blk_a0
        za0 = scores(qa, blk_a0)
        zb0 = scores(qb, blk_b0)
        za1 = scores(qa, blk_a1)
        zb1 = scores(qb, blk_b1)
        hla0, lba0, spa0 = log_terms(za0, diag_mask)
        sfa0 = suffix(hla0)
        hlb0, lbb0, spb0 = log_terms(zb0, diag_mask)
        sfb0 = suffix(hlb0)
        hla1, lba1, spa1 = log_terms(za1, None)
        aa0, ra = weights(lba0, sfa0, None, spa0, diag_mask)
        sfa1 = suffix(hla1)
        acc_a = values(aa0, blk_a0)
        hlb1, lbb1, spb1 = log_terms(zb1, None)
        ab0, rb = weights(lbb0, sfb0, None, spb0, diag_mask)
        sfb1 = suffix(hlb1)
        acc_b = values(ab0, blk_b0)
        ra = jnp.where(p > 0, ra, SB_NO_KEYS)
        aa1, ra = weights(lba1, sfa1, ra, spa1, None)
        acc_a = acc_a + values(aa1, blk_a1)
        ab1, rb = weights(lbb1, sfb1, rb, spb1, None)
        acc_b = acc_b + values(ab1, blk_b1)

        def alive(ra, rb):
            return jnp.minimum(jnp.min(ra), jnp.min(rb)) < SB_DEAD_LOG2

        def cond(carry):
            i, go = carry[0], carry[1]
            return jnp.logical_and(i <= blk_b0, go)

        def body(carry):
            i, _, ra, rb, acc_a, acc_b = carry
            has_a = i <= blk_a0
            ra = jnp.where(has_a, ra, SB_NO_KEYS)
            ra, rb, acc_a, acc_b = step(qa, qb, jnp.maximum(blk_a0 - i, 0), blk_b0 - i,
                                        ra, rb, acc_a, acc_b, None)
            return i + 1, alive(ra, rb), ra, rb, acc_a, acc_b

        carry = lax.while_loop(cond, body, (jnp.int32(2), alive(ra, rb), ra, rb, acc_a, acc_b))
        o_ref[pl.ds(row_a, n), :] = carry[4].astype(o_ref.dtype)
        o_ref[pl.ds(row_b, n), :] = carry[5].astype(o_ref.dtype)
        return 0

    lax.fori_loop(0, n_pairs, pair_body, 0)


def _sb_attention(q, kv, cast_weights=()):
    B, T, HD = q.shape
    dh = HD // SB_HEADS
    assert dh % LANES == 0 and T % (2 * SB_BLK) == 0
    c_in, c_out, c_shapes, c_bytes = _cast_specs(cast_weights, B * SB_HEADS, lambda b, h: b * SB_HEADS + h)
    blocks = 8 * T * dh * 2 + 24 * SB_BLK * SB_BLK * 4 + c_bytes
    outs = pl.pallas_call(
        functools.partial(_sb_attn_kernel, n_pairs=T // (2 * SB_BLK), n_cast=len(cast_weights)),
        grid=(B, SB_HEADS),
        in_specs=[
            pl.BlockSpec((None, T, dh), lambda b, h: (b, 0, h)),
            pl.BlockSpec((None, T, dh), lambda b, h: (b, 0, h)),
            pl.BlockSpec((None, T, dh), lambda b, h: (b, 0, SB_HEADS + h)),
        ] + c_in,
        out_specs=[pl.BlockSpec((None, T, dh), lambda b, h: (b, 0, h))] + c_out,
        out_shape=[jax.ShapeDtypeStruct((B, T, HD), BF16)] + c_shapes,
        compiler_params=_params(("arbitrary", "arbitrary"), blocks),
        name="stick_breaking_attention",
    )(q, kv, kv, *[w for w, _ in cast_weights])
    return outs[0], tuple(outs[1:])


def kernel(x, c, ada_w, ada_b, ln_g, ln_b, gla_w_in, gla_w_gate_up, gla_b_gate, gla_norm_g, gla_w_out, kv_ada_w, kv_ada_b, w_kv, sb_w_q, sb_w_out, mlp_w1, mlp_b1, mlp_w2, mlp_b2):
    B, T, D = x.shape
    depth = ada_w.shape[0]
    n_a = gla_w_in.shape[0]
    alpha = (2.0 * depth) ** 0.25
    QK = gla_w_gate_up.shape[-1]
    V = gla_w_out.shape[1]
    n_main = 2 * QK + 2 * V
    dh = sb_w_q.shape[-1] // SB_HEADS

    c_pad = jnp.pad(c, ((0, MOD_ROWS - B), (0, 0)))
    mod4 = _mod_vectors(c_pad, ada_w, ada_b).reshape(depth, MOD_ROWS, 1, N_MOD * D)
    kv_mod4 = _mod_vectors(c_pad, kv_ada_w[None], kv_ada_b[None]).reshape(1, MOD_ROWS, 1, 2 * D)

    gla_w_in_bf = gla_w_in.astype(BF16)
    gla_w_out_bf = gla_w_out.astype(BF16)
    w_kv_bf = w_kv.astype(BF16)
    sb_w_q_bf = sb_w_q.astype(BF16)

    kv = None
    for l in range(depth):
        if l < n_a:
            pad = LANES - GLA_GATE_RANK
            w_gate = jnp.pad(gla_w_in_bf[l, :, n_main:], ((0, 0), (0, pad)))
            w_up = jnp.pad(gla_w_gate_up[l], ((0, pad), (0, 0))).astype(BF16)
            proj, gl, (w1, w2) = _mod_matmul(
                x, mod4, l, 0, 1, (gla_w_in_bf, l), F32, n_cols=n_main, w_side=w_gate,
                cast_weights=((mlp_w1, l), (mlp_w2, l)))
            w_out = (gla_w_out_bf, l)
            mixed = _gla_recurrence(proj, gl, w_up, gla_b_gate[l], gla_norm_g[l])
        else:
            j = l - n_a
            if kv is None:
                kv, _, _ = _mod_matmul(x, kv_mod4, 0, 0, 1, w_kv_bf, BF16)
            q, _, _ = _mod_matmul(x, mod4, l, 0, 1, (sb_w_q_bf, j), BF16, out_scale=dh ** -0.5 * LOG2_E)
            mixed, (w_out, w1, w2) = _sb_attention(q, kv, ((sb_w_out, j), (mlp_w1, l), (mlp_w2, l)))
        x = _proj_residual_ln(mixed, w_out, x, mod4, l, 2, ln_g[l, 0], ln_b[l, 0], alpha)
        x = _mlp_block(x, mod4, l, w1, mlp_b1[l], w2, mlp_b2[l], ln_g[l, 1], ln_b[l, 1], alpha)
    return x
```
